```python
import numpy as np
import jax, jax.numpy as jnp
from jax import lax

D_MODEL = 1024
BATCH = 4
SEQ = 8192
DEPTH = 2

GRID_W = 64
CTX_LEN = 256
EPS = 1e-6
ROPE_BASE = 10000.0

RET_HEADS = 4
RET_DK = 64
RET_DV = 64
RET_CHUNK = 128
NA_HEADS = 8
NA_HD = 64
NA_KH = 8
NA_KW = 16
NA_QBLOCK = 128
MLA_HEADS = 4
MLA_NOPE = 64
MLA_ROPE = 32
MLA_VD = 64
MLA_Q_RANK = 384
MLA_KV_RANK = 256
MLA_QBLOCK = 128
N_EXPERTS = 16
N_GROUPS = 4
EXPERTS_PER_GROUP = N_EXPERTS // N_GROUPS
TOP_K = 2
EXPERT_FF = 256

D_MIX = RET_HEADS * RET_DV + NA_HEADS * NA_HD + MLA_HEADS * MLA_VD
PROJ_SIZES = (RET_HEADS * RET_DK, RET_HEADS * RET_DK, RET_HEADS * RET_DV, RET_HEADS * RET_DV,
              NA_HEADS * NA_HD, NA_HEADS * NA_HD, NA_HEADS * NA_HD,
              MLA_Q_RANK, MLA_KV_RANK, MLA_ROPE)
PROJ_SPLITS = tuple(int(v) for v in np.cumsum(PROJ_SIZES)[:-1])
D_PROJ = int(sum(PROJ_SIZES))

kernel_name = "hybrid_parallel_heads_flow_block"


def rms_norm(x, g):
    xf = x.astype(jnp.float32)
    y = xf * lax.rsqrt(jnp.mean(xf * xf, axis=-1, keepdims=True) + EPS)
    return (y * g.astype(jnp.float32)).astype(x.dtype)


def head_norm(o):
    mu = jnp.mean(o, axis=-1, keepdims=True)
    var = jnp.mean(jnp.square(o - mu), axis=-1, keepdims=True)
    return (o - mu) * lax.rsqrt(var + EPS)


def ada_modulation(cvec, w_mod, b_mod):
    m = jax.nn.silu(cvec) @ w_mod + b_mod
    return jnp.split(m[..., None, :], 6, axis=-1)


def to_heads(a, n):
    b, l, w = a.shape
    return a.reshape(b, l, n, w // n)


def rope_1d(x, pos):
    half = x.shape[-1] // 2
    inv = ROPE_BASE ** (-jnp.arange(half, dtype=jnp.float32) / half)
    ang = pos[:, None] * inv[None, :]
    cos = jnp.cos(ang)[:, None, :]
    sin = jnp.sin(ang)[:, None, :]
    xf = x.astype(jnp.float32)
    x1, x2 = xf[..., :half], xf[..., half:]
    return jnp.concatenate([x1 * cos - x2 * sin, x2 * cos + x1 * sin], axis=-1).astype(x.dtype)


def axial_rope(x, pos_r, pos_c):
    h = x.shape[-1] // 2
    return jnp.concatenate([rope_1d(x[..., :h], pos_r), rope_1d(x[..., h:], pos_c)], axis=-1)


def retention_chunkwise(q, k, v, log_gamma, s0, strict):
    b, h, l, dk = q.shape
    dv = v.shape[-1]
    n = l // RET_CHUNK
    qc = q.reshape(b, h, n, RET_CHUNK, dk)
    kc = k.reshape(b, h, n, RET_CHUNK, dk)
    vc = v.reshape(b, h, n, RET_CHUNK, dv)
    pos = jnp.arange(RET_CHUNK, dtype=jnp.float32)
    diff = pos[:, None] - pos[None, :]
    keep = diff > 0 if strict else diff >= 0
    lg = log_gamma[:, None, None]
    dmat = jnp.where(keep[None], jnp.exp(lg * jnp.maximum(diff, 0.0)[None]), 0.0)
    scores = jnp.einsum('bhncd,bhnkd->bhnck', qc, kc) * dmat[None, :, None]
    intra = jnp.einsum('bhnck,bhnke->bhnce', scores, vc)
    k_w = jnp.exp(log_gamma[:, None] * (RET_CHUNK - 1.0 - pos)[None])
    chunk_kv = jnp.einsum('bhnkd,hk,bhnke->nbhde', kc, k_w, vc)
    g_chunk = jnp.exp(log_gamma * RET_CHUNK)[None, :, None, None]

    def step(state, kv):
        return g_chunk * state + kv, state

    s_final, states = lax.scan(step, s0, chunk_kv)
    q_w = jnp.exp(log_gamma[:, None] * (pos + 1.0)[None])
    inter = jnp.einsum('bhncd,hc,nbhde->bhnce', qc, q_w, states)
    return (intra + inter).reshape(b, h, l, dv), s_final


def bidir_retention(q, k, v, lg_f, lg_b, s0_f, s0_b):
    o_f, s_f = retention_chunkwise(q, k, v, lg_f, s0_f, False)
    o_b, s_b = retention_chunkwise(jnp.flip(q, 2), jnp.flip(k, 2), jnp.flip(v, 2), lg_b, s0_b, True)
    return o_f + jnp.flip(o_b, 2), s_f, s_b


def retention_heads(q_cols, k_cols, v_cols, pos):
    q = to_heads(q_cols, RET_HEADS)
    k = to_heads(k_cols, RET_HEADS)
    v = to_heads(v_cols, RET_HEADS)
    if pos is not None:
        q = axial_rope(q, pos[0], pos[1])
        k = axial_rope(k, pos[0], pos[1])
    q = q.astype(jnp.float32).transpose(0, 2, 1, 3) * (RET_DK ** -0.5)
    k = k.astype(jnp.float32).transpose(0, 2, 1, 3)
    v = v.astype(jnp.float32).transpose(0, 2, 1, 3)
    return q, k, v


def retention_output(o, g):
    b, h, l, dv = o.shape
    on = head_norm(o).transpose(0, 2, 1, 3).reshape(b, l, h * dv)
    return on.astype(g.dtype) * jax.nn.silu(g)


def neighbourhood_tables(rows):
    kh = min(NA_KH, rows)
    s = rows * GRID_W
    t = jnp.arange(s)
    r = t // GRID_W
    c = t % GRID_W
    r0 = jnp.clip(r - kh // 2, 0, rows - kh)
    c0 = jnp.clip(c - NA_KW // 2, 0, GRID_W - NA_KW)
    kr = r0[:, None, None] + jnp.arange(kh)[None, :, None]
    kc = c0[:, None, None] + jnp.arange(NA_KW)[None, None, :]
    idx = (kr * GRID_W + kc).reshape(s, kh * NA_KW)
    dr = kr - r[:, None, None]
    dc = kc - c[:, None, None]
    bidx = ((dr + NA_KH - 1) * (2 * NA_KW - 1) + (dc + NA_KW - 1)).reshape(s, kh * NA_KW)
    return idx.astype(jnp.int32), bidx.astype(jnp.int32)


def neighbourhood_attention(q, k, v, k_ctx, v_ctx, rpb, idx, bidx):
    b, s, h, d = q.shape
    nblk = s // NA_QBLOCK
    nk = idx.shape[-1]
    scale = d ** -0.5
    rpb_flat = rpb.reshape(h, -1)
    qb = jnp.moveaxis(q.reshape(b, nblk, NA_QBLOCK, h, d), 1, 0)
    idx_b = idx.reshape(nblk, NA_QBLOCK, nk)
    bidx_b = bidx.reshape(nblk, NA_QBLOCK, nk)

    def block(args):
        qi, ii, bi = args
        kg = k[:, ii]
        vg = v[:, ii]
        s_loc = jnp.einsum('bqhd,bqkhd->bhqk', qi, kg).astype(jnp.float32) * scale
        s_loc = s_loc + rpb_flat[:, bi].astype(jnp.float32)[None]
        s_ctx = jnp.einsum('bqhd,bkhd->bhqk', qi, k_ctx).astype(jnp.float32) * scale
        p = jax.nn.softmax(jnp.concatenate([s_loc, s_ctx], axis=-1), axis=-1).astype(v.dtype)
        return (jnp.einsum('bhqk,bqkhd->bqhd', p[..., :nk], vg)
                + jnp.einsum('bhqk,bkhd->bqhd', p[..., nk:], v_ctx))

    o = lax.map(block, (qb, idx_b, bidx_b))
    return jnp.moveaxis(o, 0, 1).reshape(b, s, h * d)


def dense_attention(q, k, v):
    b, l, h, dq = q.shape
    s = jnp.einsum('bqhd,bkhd->bhqk', q, k).astype(jnp.float32) * (dq ** -0.5)
    p = jax.nn.softmax(s, axis=-1).astype(v.dtype)
    return jnp.einsum('bhqk,bkhd->bqhd', p, v).reshape(b, l, h * v.shape[-1])


def blocked_attention(q, k, v):
    b, s, h, dq = q.shape
    nblk = s // MLA_QBLOCK
    scale = dq ** -0.5
    qb = jnp.moveaxis(q.reshape(b, nblk, MLA_QBLOCK, h, dq), 1, 0)

    def block(qi):
        sc = jnp.einsum('bqhd,bkhd->bhqk', qi, k).astype(jnp.float32) * scale
        p = jax.nn.softmax(sc, axis=-1).astype(v.dtype)
        return jnp.einsum('bhqk,bkhd->bqhd', p, v)

    o = lax.map(block, qb)
    return jnp.moveaxis(o, 0, 1).reshape(b, s, h * v.shape[-1])


def mla_heads(c_q, c_kv, k_pe, q_norm, kv_norm, w_uq, w_ukv, pos):
    b, l, _ = c_q.shape
    q = (rms_norm(c_q, q_norm) @ w_uq).reshape(b, l, MLA_HEADS, MLA_NOPE + MLA_ROPE)
    kv = (rms_norm(c_kv, kv_norm) @ w_ukv).reshape(b, l, MLA_HEADS, MLA_NOPE + MLA_VD)
    q_nope, q_pe = q[..., :MLA_NOPE], q[..., MLA_NOPE:]
    k_nope, v = kv[..., :MLA_NOPE], kv[..., MLA_NOPE:]
    k_pe = k_pe[:, :, None, :]
    if pos is not None:
        q_pe = axial_rope(q_pe, pos[0], pos[1])
        k_pe = axial_rope(k_pe, pos[0], pos[1])
    k_pe = jnp.broadcast_to(k_pe, (b, l, MLA_HEADS, MLA_ROPE))
    q = jnp.concatenate([q_nope, q_pe], axis=-1)
    k = jnp.concatenate([k_nope, k_pe], axis=-1)
    return q, k, v


def moe_ffn(u, router_w, router_b, w1, w3, w2):
    shp = u.shape
    t = u.reshape(-1, shp[-1])
    scores = jax.nn.sigmoid((t @ router_w).astype(jnp.float32))
    sel = scores + router_b.astype(jnp.float32)
    grp_score = jnp.sum(lax.top_k(sel.reshape(-1, N_GROUPS, EXPERTS_PER_GROUP), TOP_K)[0], axis=-1)
    best = jnp.argmax(grp_score, axis=-1)
    in_grp = (jnp.arange(N_EXPERTS) // EXPERTS_PER_GROUP)[None, :] == best[:, None]
    _, top_idx = lax.top_k(jnp.where(in_grp, sel, -jnp.inf), TOP_K)
    top_w = jnp.take_along_axis(scores, top_idx, axis=-1)
    top_w = top_w / jnp.sum(top_w, axis=-1, keepdims=True)
    gates = jnp.sum(jax.nn.one_hot(top_idx, N_EXPERTS, dtype=jnp.float32) * top_w[..., None], axis=1)
    gates = gates.astype(t.dtype)
    y = jnp.zeros_like(t)
    for e in range(N_EXPERTS):
        hdn = jax.nn.silu(t @ w1[e]) * (t @ w3[e])
        y = y + (hdn @ w2[e]) * gates[:, e:e + 1]
    return y.reshape(shp)


def setup_inputs(seed: int = 0) -> dict:
    key = jax.random.key(seed)
    ks = jax.random.split(key, 24)
    f32 = jnp.float32

    def nrm(k, shape, s):
        return jax.random.normal(k, shape, f32) * s

    base_gamma = 1.0 - 2.0 ** (-5.0 - jnp.arange(RET_HEADS, dtype=f32))
    base_logit = jnp.log(base_gamma) - jnp.log1p(-base_gamma)
    return {
        "x": nrm(ks[0], (BATCH, SEQ, D_MODEL), 1.0),
        "c": nrm(ks[1], (BATCH, D_MODEL), 1.0),
        "ctx": nrm(ks[2], (BATCH, CTX_LEN, D_MODEL), 1.0),
        "c_ctx": nrm(ks[3], (D_MODEL,), 1.0),
        "w_mod": nrm(ks[4], (DEPTH, D_MODEL, 6 * D_MODEL), 0.5 * D_MODEL ** -0.5),
        "b_mod": nrm(ks[5], (DEPTH, 6 * D_MODEL), 0.02),
        "norm1_g": 1.0 + nrm(ks[6], (DEPTH, D_MODEL), 0.02),
        "norm2_g": 1.0 + nrm(ks[7], (DEPTH, D_MODEL), 0.02),
        "w_in": nrm(ks[8], (DEPTH, D_MODEL, D_PROJ), D_MODEL ** -0.5),
        "ret_decay_f": base_logit + nrm(ks[9], (DEPTH, RET_HEADS), 0.1),
        "ret_decay_b": base_logit + nrm(ks[10], (DEPTH, RET_HEADS), 0.1),
        "na_rpb": nrm(ks[11], (DEPTH, NA_HEADS, 2 * NA_KH - 1, 2 * NA_KW - 1), 0.1),
        "mla_q_norm": 1.0 + nrm(ks[12], (DEPTH, MLA_Q_RANK), 0.02),
        "mla_kv_norm": 1.0 + nrm(ks[13], (DEPTH, MLA_KV_RANK), 0.02),
        "w_uq": nrm(ks[14], (DEPTH, MLA_Q_RANK, MLA_HEADS * (MLA_NOPE + MLA_ROPE)), MLA_Q_RANK ** -0.5),
        "w_ukv": nrm(ks[15], (DEPTH, MLA_KV_RANK, MLA_HEADS * (MLA_NOPE + MLA_VD)), MLA_KV_RANK ** -0.5),
        "w_out": nrm(ks[16], (DEPTH, D_MIX, D_MODEL), D_MIX ** -0.5),
        "router_w": nrm(ks[17], (D_MODEL, N_EXPERTS), D_MODEL ** -0.5),
        "router_b": nrm(ks[18], (N_EXPERTS,), 0.01),
        "w1": nrm(ks[19], (DEPTH, N_EXPERTS, D_MODEL, EXPERT_FF), D_MODEL ** -0.5),
        "w3": nrm(ks[20], (DEPTH, N_EXPERTS, D_MODEL, EXPERT_FF), D_MODEL ** -0.5),
        "w2": nrm(ks[21], (DEPTH, N_EXPERTS, EXPERT_FF, D_MODEL), EXPERT_FF ** -0.5),
        "final_norm_g": 1.0 + nrm(ks[22], (D_MODEL,), 0.02),
    }


def reference(x, c, ctx, c_ctx, w_mod, b_mod, norm1_g, norm2_g, w_in, ret_decay_f, ret_decay_b,
              na_rpb, mla_q_norm, mla_kv_norm, w_uq, w_ukv, w_out, router_w, router_b, w1, w3, w2,
              final_norm_g):
    b, s, _ = x.shape
    rows = s // GRID_W
    t = jnp.arange(s)
    pos = ((t // GRID_W).astype(jnp.float32), (t % GRID_W).astype(jnp.float32))
    nb_idx, nb_bias_idx = neighbourhood_tables(rows)
    h, hc = x, ctx
    for l in range(DEPTH):
        last = l == DEPTH - 1
        sh1, sc1, g1, sh2, sc2, g2 = ada_modulation(c, w_mod[l], b_mod[l])
        csh1, csc1, cg1, csh2, csc2, cg2 = ada_modulation(c_ctx, w_mod[l], b_mod[l])

        u = rms_norm(h, norm1_g[l]) * (1.0 + sc1) + sh1
        uc = rms_norm(hc, norm1_g[l]) * (1.0 + csc1) + csh1
        p = jnp.split(u @ w_in[l], PROJ_SPLITS, axis=-1)
        pc = jnp.split(uc @ w_in[l], PROJ_SPLITS, axis=-1)

        lg_f = jax.nn.log_sigmoid(ret_decay_f[l].astype(jnp.float32))
        lg_b = jax.nn.log_sigmoid(ret_decay_b[l].astype(jnp.float32))
        rq_c, rk_c, rv_c = retention_heads(pc[0], pc[1], pc[2], None)
        rq, rk, rv = retention_heads(p[0], p[1], p[2], pos)
        zero_state = jnp.zeros((b, RET_HEADS, RET_DK, RET_DV), jnp.float32)
        ro_c, st_f, st_b = bidir_retention(rq_c, rk_c, rv_c, lg_f, lg_b, zero_state, zero_state)
        ro, _, _ = bidir_retention(rq, rk, rv, lg_f, lg_b, st_f, st_b)
        ret_lat = retention_output(ro, p[3])

        nq_c, nk_c, nv_c = to_heads(pc[4], NA_HEADS), to_heads(pc[5], NA_HEADS), to_heads(pc[6], NA_HEADS)
        nq, nk, nv = to_heads(p[4], NA_HEADS), to_heads(p[5], NA_HEADS), to_heads(p[6], NA_HEADS)
        na_lat = neighbourhood_attention(nq, nk, nv, nk_c, nv_c, na_rpb[l], nb_idx, nb_bias_idx)

        mq_c, mk_c, mv_c = mla_heads(pc[7], pc[8], pc[9], mla_q_norm[l], mla_kv_norm[l], w_uq[l], w_ukv[l], None)
        mq, mk, mv = mla_heads(p[7], p[8], p[9], mla_q_norm[l], mla_kv_norm[l], w_uq[l], w_ukv[l], pos)
        mla_lat = blocked_attention(mq, jnp.concatenate([mk, mk_c], axis=1), jnp.concatenate([mv, mv_c], axis=1))

        h = h + g1 * (jnp.concatenate([ret_lat, na_lat, mla_lat], axis=-1) @ w_out[l])
        u2 = rms_norm(h, norm2_g[l]) * (1.0 + sc2) + sh2
        h = h + g2 * moe_ffn(u2, router_w, router_b, w1[l], w3[l], w2[l])

        if not last:
            ret_ctx = retention_output(ro_c, pc[3])
            na_ctx = dense_attention(nq_c, nk_c, nv_c)
            mla_ctx = dense_attention(mq_c, mk_c, mv_c)
            hc = hc + cg1 * (jnp.concatenate([ret_ctx, na_ctx, mla_ctx], axis=-1) @ w_out[l])
            uc2 = rms_norm(hc, norm2_g[l]) * (1.0 + csc2) + csh2
            hc = hc + cg2 * moe_ffn(uc2, router_w, router_b, w1[l], w3[l], w2[l])
    return rms_norm(h, final_norm_g)
```

```python
import functools

import numpy as np
import jax
import jax.numpy as jnp
from jax import lax
from jax.experimental import pallas as pl
from jax.experimental.pallas import tpu as pltpu

F32 = jnp.float32
BF16 = jnp.bfloat16

GRID_W = 64
EPS = 1e-6
ROPE_BASE = 10000.0
RET_HEADS, RET_DK, RET_DV, RET_CHUNK = 4, 64, 64, 128
NA_HEADS, NA_HD, NA_KH, NA_KW, NA_QBLOCK = 8, 64, 8, 16, 128
MLA_HEADS, MLA_NOPE, MLA_ROPE, MLA_VD = 4, 64, 32, 64
MLA_Q_RANK, MLA_KV_RANK = 384, 256
N_EXPERTS, N_GROUPS, TOP_K, EXPERT_FF = 16, 4, 2, 256
EXPERTS_PER_GROUP = N_EXPERTS // N_GROUPS

LANES = 128
HEAD = 64
NA_WIN_BLOCKS = 5
NEG_BIG = -1e30
VMEM_LIMIT = 48 * 1024 * 1024

_C_RET, _C_NA, _C_CQ, _C_CKV, _C_KPE = 0, 1024, 2560, 2944, 3200
_W_EXT = _C_KPE + MLA_HEADS * LANES


def _cparams(sem):
    return pltpu.CompilerParams(dimension_semantics=sem, vmem_limit_bytes=VMEM_LIMIT)


def _dot(a, b):
    return jnp.dot(a, b, preferred_element_type=F32)


def _dot_nt(a, b):
    return lax.dot_general(a, b, (((1,), (1,)), ((), ())), preferred_element_type=F32)


def _sigmoid(x):
    return 1.0 / (1.0 + jnp.exp(-x))


def _win_roll(x, s, w):
    n = x.shape[-1]
    ax = x.ndim - 1
    lane = lax.broadcasted_iota(jnp.int32, x.shape, ax)
    fwd = pltpu.roll(x, n - s, ax)
    bwd = pltpu.roll(x, w - s, ax)
    return jnp.where((lane & (w - 1)) + s < w, fwd, bwd)


def _low_half(shape):
    lane = lax.broadcasted_iota(jnp.int32, shape, len(shape) - 1)
    return (lane & (LANES - 1)) < HEAD


def _mod_kernel(cv_ref, w_ref, b_ref, o_ref):
    cv = cv_ref[...]
    a = cv * _sigmoid(cv)
    o_ref[0] = jnp.dot(a, w_ref[0], preferred_element_type=F32,
                       precision=lax.Precision.HIGHEST) + b_ref[0]


def _modulation(cv, w_mod, b_mod):
    depth, d, n6 = w_mod.shape
    rows = cv.shape[0]
    tn = 1536
    return pl.pallas_call(
        _mod_kernel,
        grid=(depth, n6 // tn),
        in_specs=[pl.BlockSpec((rows, d), lambda l, j: (0, 0)),
                  pl.BlockSpec((1, d, tn), lambda l, j: (l, 0, j)),
                  pl.BlockSpec((1, 1, tn), lambda l, j: (l, 0, j))],
        out_specs=pl.BlockSpec((1, rows, tn), lambda l, j: (l, 0, j)),
        out_shape=jax.ShapeDtypeStruct((depth, rows, n6), F32),
        compiler_params=_cparams(("arbitrary", "arbitrary")),
        name="modulation",
    )(cv, w_mod, b_mod.reshape(depth, 1, n6))


def _rms(x, g):
    return x * lax.rsqrt(jnp.mean(x * x, axis=-1, keepdims=True) + EPS) * g


def _inproj_kernel(h_ref, sh_ref, sc_ref, g_ref, w_ref, qn_ref, kvn_ref, wuq_ref, wuk_ref, wuv_ref,
                   ra_ref, rb_ref, ma_ref, mb_ref,
                   ret_ref, na_ref, q_ref, k_ref, v_ref):
    x = h_ref[0]
    u = _rms(x, g_ref[...]) * (1.0 + sc_ref[0]) + sh_ref[0]
    ub = u.astype(BF16)

    r = _dot(ub, w_ref[:, _C_RET:_C_NA])
    qk = r[:, :512]
    ra = jnp.concatenate([ra_ref[...]] * 4, axis=1)
    rb = jnp.concatenate([rb_ref[...]] * 4, axis=1)
    qk = qk * ra + _win_roll(qk, 16, 32) * rb
    ret_ref[0, :, 0:256] = qk[:, :256] * (RET_DK ** -0.5)
    ret_ref[0, :, 256:512] = qk[:, 256:]
    ret_ref[0, :, 512:1024] = r[:, 512:]

    n = _dot(ub, w_ref[:, _C_NA:_C_CQ])
    na_ref[0, :, 0:512] = (n[:, :512] * (NA_HD ** -0.5)).astype(BF16)
    na_ref[0, :, 512:1536] = n[:, 512:].astype(BF16)

    ma = jnp.concatenate([ma_ref[...]] * 4, axis=1)
    mb = jnp.concatenate([mb_ref[...]] * 4, axis=1)
    cq = _rms(_dot(ub, w_ref[:, _C_CQ:_C_CKV]), qn_ref[...]).astype(BF16)
    q = _dot(cq, wuq_ref[...])
    q = q * ma + _win_roll(q, 8, 16) * mb
    q_ref[0] = (q * ((MLA_NOPE + MLA_ROPE) ** -0.5)).astype(BF16)
    ckv = _rms(_dot(ub, w_ref[:, _C_CKV:_C_KPE]), kvn_ref[...]).astype(BF16)
    kp = _dot(ub, w_ref[:, _C_KPE:_W_EXT])
    kp = kp * ma + _win_roll(kp, 8, 16) * mb
    k_ref[0] = (_dot(ckv, wuk_ref[...]) + kp).astype(BF16)
    v_ref[0] = _dot(ckv, wuv_ref[...]).astype(BF16)


def _in_projection(h, sh, sc, g, w_ext, qn, kvn, wuq, wuk, wuv, tabs, tm):
    b, l, d = h.shape
    per_batch = sh.shape[0] > 1
    mod_map = (lambda i, j: (i, 0, 0)) if per_batch else (lambda i, j: (0, 0, 0))
    const2 = lambda i, j: (0, 0)
    tab_spec = pl.BlockSpec((tm, LANES), lambda i, j: (j, 0))
    tok = lambda w: pl.BlockSpec((1, tm, w), lambda i, j: (i, j, 0))
    return pl.pallas_call(
        _inproj_kernel,
        grid=(b, l // tm),
        in_specs=[tok(d),
                  pl.BlockSpec((1, 1, d), mod_map), pl.BlockSpec((1, 1, d), mod_map),
                  pl.BlockSpec((1, d), const2),
                  pl.BlockSpec(w_ext.shape, const2),
                  pl.BlockSpec(qn.shape, const2), pl.BlockSpec(kvn.shape, const2),
                  pl.BlockSpec(wuq.shape, const2), pl.BlockSpec(wuk.shape, const2),
                  pl.BlockSpec(wuv.shape, const2),
                  tab_spec, tab_spec, tab_spec, tab_spec],
        out_specs=[tok(1024), tok(1536), tok(512), tok(512), tok(256)],
        out_shape=[jax.ShapeDtypeStruct((b, l, 1024), F32),
                   jax.ShapeDtypeStruct((b, l, 1536), BF16),
                   jax.ShapeDtypeStruct((b, l, 512), BF16),
                   jax.ShapeDtypeStruct((b, l, 512), BF16),
                   jax.ShapeDtypeStruct((b, l, 256), BF16)],
        compiler_params=_cparams(("arbitrary", "arbitrary")),
        name="in_projection",
    )(h, sh, sc, g, w_ext, qn, kvn, wuq, wuk, wuv, *tabs)


def _retention_kernel(q_ref, k_ref, v_ref, g_ref, dm_ref, qw_ref, kw_ref, gc_ref, s0_ref,
                      o_ref, sfin_ref, state, o_fwd, *, n_chunks):
    direction = pl.program_id(1)
    step = pl.program_id(2)
    c = RET_CHUNK

    @pl.when(step == 0)
    def _():
        state[...] = s0_ref[0, 0]

    q, k, v = q_ref[0], k_ref[0], v_ref[0]
    qw, kw = qw_ref[0], kw_ref[0]
    lo = _low_half((c, LANES))
    row_lo = lax.broadcasted_iota(jnp.int32, (LANES, LANES), 0) < HEAD
    same_head = row_lo == _low_half((LANES, LANES))
    outs = []
    for p in range(RET_HEADS // 2):
        sl = slice(LANES * p, LANES * (p + 1))
        qp, kp = q[:, sl], k[:, sl]
        kb, vb = kp.astype(BF16), v[:, sl].astype(BF16)
        qa = jnp.where(lo, qp, 0.0).astype(BF16)
        qb = jnp.where(lo, 0.0, qp).astype(BF16)
        sa = (_dot_nt(qa, kb) * dm_ref[0, 2 * p]).astype(BF16)
        sb = (_dot_nt(qb, kb) * dm_ref[0, 2 * p + 1]).astype(BF16)
        intra = jnp.where(lo, _dot(sa, vb), _dot(sb, vb))
        sp = state[sl, :]
        inter = _dot((qp * qw[:, sl]).astype(BF16), sp.astype(BF16))
        outs.append(intra + inter)
        kv = _dot((kp * kw[:, sl]).T.astype(BF16), vb)
        state[sl, :] = gc_ref[0, sl, :] * sp + jnp.where(same_head, kv, 0.0)
    o = jnp.concatenate(outs, axis=1)

    chunk = jnp.where(direction == 0, step, n_chunks - 1 - step)
    rows = pl.ds(pl.multiple_of(chunk * c, c), c)

    @pl.when(direction == 0)
    def _():
        o_fwd[rows, :] = o

    @pl.when(direction == 1)
    def _():
        tot = o_fwd[rows, :] + o
        g = g_ref[0]
        normed = []
        for p in range(RET_HEADS // 2):
            t = tot[:, LANES * p:LANES * (p + 1)]

            def head_mean(a):
                s_lo = jnp.sum(jnp.where(lo, a, 0.0), axis=-1, keepdims=True)
                s_hi = jnp.sum(jnp.where(lo, 0.0, a), axis=-1, keepdims=True)
                return jnp.where(lo, s_lo, s_hi) * (1.0 / HEAD)

            dlt = t - head_mean(t)
            normed.append(dlt * lax.rsqrt(head_mean(dlt * dlt) + EPS))
        on = jnp.concatenate(normed, axis=1)
        o_ref[0] = (on * (g * _sigmoid(g))).astype(BF16)

    @pl.when(step == n_chunks - 1)
    def _():
        sfin_ref[0, 0] = state[...]


def _retention(ret, tabs, s0):
    b, l, _ = ret.shape
    n = l // RET_CHUNK
    c = RET_CHUNK

    def tok(col):
        return pl.BlockSpec((1, c, 256),
                            lambda i, d, s: (i, jnp.where(d == 0, s, n - 1 - s), col))

    dirspec = lambda shape: pl.BlockSpec((1,) + shape, lambda i, d, s: (d,) + (0,) * len(shape))
    state_spec = pl.BlockSpec((1, 1, 2 * LANES, LANES), lambda i, d, s: (d, i, 0, 0))
    dm, qw, kw, gc = tabs
    return pl.pallas_call(
        functools.partial(_retention_kernel, n_chunks=n),
        grid=(b, 2, n),
        in_specs=[tok(0), tok(1), tok(2), tok(3),
                  dirspec((RET_HEADS, c, c)), dirspec((c, 256)), dirspec((c, 256)),
                  dirspec((2 * LANES, LANES)), state_spec],
        out_specs=[pl.BlockSpec((1, c, 256),
                                lambda i, d, s: (i, jnp.where(d == 0, n - 1, n - 1 - s), 0)),
                   state_spec],
        out_shape=[jax.ShapeDtypeStruct((b, l, 256), BF16),
                   jax.ShapeDtypeStruct((2, b, 2 * LANES, LANES), F32)],
        scratch_shapes=[pltpu.VMEM((2 * LANES, LANES), F32), pltpu.VMEM((l, 256), F32)],
        compiler_params=_cparams(("arbitrary", "arbitrary", "arbitrary")),
        name="retention",
    )(ret, ret, ret, ret, dm, qw, kw, gc, s0)


def _retention_tables(decay_f, decay_b):
    c = RET_CHUNK
    pos = jnp.arange(c, dtype=F32)
    diff = pos[:, None] - pos[None, :]
    lg = jnp.stack([jax.nn.log_sigmoid(decay_f.astype(F32)), jax.nn.log_sigmoid(decay_b.astype(F32))])
    lgh = lg[:, :, None, None]
    dm_f = jnp.where(diff >= 0, jnp.exp(lgh[0] * jnp.maximum(diff, 0.0)), 0.0)
    dm_b = jnp.where(diff < 0, jnp.exp(lgh[1] * jnp.maximum(-diff, 0.0)), 0.0)
    dm = jnp.stack([dm_f, dm_b])
    lanes = lambda t: jnp.repeat(t, RET_DK, axis=-1)
    lgc = lg[:, None, :]
    qw = lanes(jnp.exp(lgc * jnp.stack([pos + 1.0, c - pos])[:, :, None]))
    kw = lanes(jnp.exp(lgc * jnp.stack([c - 1.0 - pos, pos])[:, :, None]))
    gch = jnp.exp(lg * c)
    gc = jnp.broadcast_to(jnp.repeat(gch, RET_DK, axis=-1)[:, :, None], (2, RET_HEADS * RET_DK, LANES))
    return dm, qw, kw, gc


def _na_kernel(*refs, n_ctx):
    q_ref = refs[0]
    k_refs = refs[1:1 + NA_WIN_BLOCKS]
    v_refs = refs[1 + NA_WIN_BLOCKS:1 + 2 * NA_WIN_BLOCKS]
    kc_ref, vc_ref, bias_ref, o_ref = refs[1 + 2 * NA_WIN_BLOCKS:]
    tq = NA_QBLOCK
    lo = _low_half((tq, LANES))
    q = q_ref[0]
    for p in range(NA_HEADS // 2):
        sl = slice(LANES * p, LANES * (p + 1))
        qp = q[:, sl]
        kwin = jnp.concatenate([r[0, :, sl] for r in k_refs], axis=0)
        vwin = jnp.concatenate([r[0, :, sl] for r in v_refs], axis=0)
        kc, vc = kc_ref[0, :, sl], vc_ref[0, :, sl]
        halves = []
        for half in range(2):
            qm = jnp.where(lo, qp, 0) if half == 0 else jnp.where(lo, 0, qp)
            qm = qm.astype(BF16)
            s_loc = _dot_nt(qm, kwin) + bias_ref[0, 2 * p + half]
            s_ctx = _dot_nt(qm, kc)
            m = jnp.maximum(jnp.max(s_loc, axis=-1, keepdims=True),
                            jnp.max(s_ctx, axis=-1, keepdims=True))
            e_loc = jnp.exp(s_loc - m)
            e_ctx = jnp.exp(s_ctx - m)
            den = jnp.sum(e_loc, axis=-1, keepdims=True) + jnp.sum(e_ctx, axis=-1, keepdims=True)
            o = _dot(e_loc.astype(BF16), vwin) + _dot(e_ctx.astype(BF16), vc)
            halves.append(o / den)
        o_ref[0, :, sl] = jnp.where(lo, halves[0], halves[1]).astype(BF16)


def _na_variant(i, nblk):
    return jnp.where(i < 2, i, jnp.where(i <= nblk - 3, 2, i - (nblk - 2) + 3))


def _na_attention(na, na_ctx, bias):
    b, l, _ = na.shape
    n_ctx = na_ctx.shape[1]
    nblk = l // NA_QBLOCK
    tq = NA_QBLOCK

    def win(col, j):
        return pl.BlockSpec((1, tq, 512),
                            lambda bi, i: (bi, jnp.clip(i - 2, 0, nblk - NA_WIN_BLOCKS) + j, col))

    in_specs = ([pl.BlockSpec((1, tq, 512), lambda bi, i: (bi, i, 0))]
                + [win(1, j) for j in range(NA_WIN_BLOCKS)]
                + [win(2, j) for j in range(NA_WIN_BLOCKS)]
                + [pl.BlockSpec((1, n_ctx, 512), lambda bi, i: (bi, 0, 1)),
                   pl.BlockSpec((1, n_ctx, 512), lambda bi, i: (bi, 0, 2)),
                   pl.BlockSpec((1, NA_HEADS, tq, NA_WIN_BLOCKS * tq),
                                lambda bi, i: (_na_variant(i, nblk), 0, 0, 0))])
    args = [na] * (1 + 2 * NA_WIN_BLOCKS) + [na_ctx, na_ctx, bias]
    return pl.pallas_call(
        functools.partial(_na_kernel, n_ctx=n_ctx),
        grid=(b, nblk),
        in_specs=in_specs,
        out_specs=pl.BlockSpec((1, tq, 512), lambda bi, i: (bi, i, 0)),
        out_shape=jax.ShapeDtypeStruct((b, l, 512), BF16),
        compiler_params=_cparams(("arbitrary", "arbitrary")),
        name="neighbourhood_attention",
    )(*args)


def _na_bias_index(rows):
    nblk = rows * GRID_W // NA_QBLOCK
    kh = min(NA_KH, rows)
    reps = [0, 1, 2, nblk - 2, nblk - 1]
    idx = np.zeros((5, NA_QBLOCK, NA_WIN_BLOCKS * NA_QBLOCK), np.int32)
    ok = np.zeros(idx.shape, bool)
    qi = np.arange(NA_QBLOCK)
    kj = np.arange(NA_WIN_BLOCKS * NA_QBLOCK)
    for v, i in enumerate(reps):
        t = NA_QBLOCK * i + qi
        r, c = t // GRID_W, t % GRID_W
        r0 = np.clip(r - kh // 2, 0, rows - kh)
        c0 = np.clip(c - NA_KW // 2, 0, GRID_W - NA_KW)
        kb = int(np.clip(i - 2, 0, nblk - NA_WIN_BLOCKS))
        kt = NA_QBLOCK * kb + kj
        kr, kc = kt // GRID_W, kt % GRID_W
        dr = kr[None, :] - r[:, None]
        dc = kc[None, :] - c[:, None]
        ok[v] = ((kr[None, :] >= r0[:, None]) & (kr[None, :] < r0[:, None] + kh)
                 & (kc[None, :] >= c0[:, None]) & (kc[None, :] < c0[:, None] + NA_KW))
        idx[v] = np.where(ok[v], (dr + NA_KH - 1) * (2 * NA_KW - 1) + (dc + NA_KW - 1), 0)
    return idx, ok


def _na_bias(rpb, idx, ok):
    flat = rpb.reshape(NA_HEADS, -1).astype(F32)
    g = jnp.take(flat, jnp.asarray(idx.reshape(-1)), axis=1).reshape((NA_HEADS,) + idx.shape)
    g = jnp.where(jnp.asarray(ok)[None], g, NEG_BIG)
    return jnp.transpose(g, (1, 0, 2, 3))


def _attn_kernel(q_ref, k_ref, v_ref, o_ref, *, paired, tk):
    tq = q_ref.shape[1]
    n_k = k_ref.shape[1] // tk
    n_pairs = v_ref.shape[2] // LANES
    lo = _low_half((tq, LANES))
    for p in range(n_pairs):
        vsl = slice(LANES * p, LANES * (p + 1))
        halves = []
        for half in range(2):
            if paired:
                ksl = vsl
                qp = q_ref[0, :, vsl]
                qh = jnp.where(lo, qp, 0) if half == 0 else jnp.where(lo, 0, qp)
            else:
                ksl = slice(LANES * (2 * p + half), LANES * (2 * p + half + 1))
                qh = q_ref[0, :, ksl]
            qh = qh.astype(BF16)

            def body(j, carry, qh=qh, ksl=ksl, vsl=vsl):
                m, den, acc = carry
                rows = pl.ds(pl.multiple_of(j * tk, tk), tk)
                s = _dot_nt(qh, k_ref[0, rows, ksl])
                m_new = jnp.maximum(m, jnp.max(s, axis=-1, keepdims=True))
                alpha = jnp.exp(m - m_new)
                e = jnp.exp(s - m_new)
                den = alpha * den + jnp.sum(e, axis=-1, keepdims=True)
                acc = alpha * acc + _dot(e.astype(BF16), v_ref[0, rows, vsl])
                return m_new, den, acc

            init = (jnp.full((tq, 1), -jnp.inf, F32), jnp.zeros((tq, 1), F32),
                    jnp.zeros((tq, LANES), F32))
            _, den, acc = lax.fori_loop(0, n_k, body, init)
            halves.append(acc / den)
        o_ref[0, :, vsl] = jnp.where(lo, halves[0], halves[1]).astype(BF16)


def _attention(q, k, v, *, paired, tq, tk, name):
    b, lq, wq = q.shape
    lk, wv = k.shape[1], v.shape[2]
    return pl.pallas_call(
        functools.partial(_attn_kernel, paired=paired, tk=tk),
        grid=(b, lq // tq),
        in_specs=[pl.BlockSpec((1, tq, wq), lambda bi, i: (bi, i, 0)),
                  pl.BlockSpec((1, lk, wq), lambda bi, i: (bi, 0, 0)),
                  pl.BlockSpec((1, lk, wv), lambda bi, i: (bi, 0, 0))],
        out_specs=pl.BlockSpec((1, tq, wv), lambda bi, i: (bi, i, 0)),
        out_shape=jax.ShapeDtypeStruct((b, lq, wv), BF16),
        compiler_params=_cparams(("arbitrary", "arbitrary")),
        name=name,
    )(q, k, v)


def _gates(logits, bias):
    scores = _sigmoid(logits)
    sel = scores + bias
    lane = lax.broadcasted_iota(jnp.int32, logits.shape, 1)
    e4 = lane & (EXPERTS_PER_GROUP - 1)
    grp = (lane >> 2) & (N_GROUPS - 1)
    one, zero = jnp.float32(1.0), jnp.float32(0.0)

    rank = jnp.zeros(logits.shape, F32)
    for d in range(1, EXPERTS_PER_GROUP):
        other = _win_roll(sel, d, EXPERTS_PER_GROUP)
        other_first = ((e4 + d) & (EXPERTS_PER_GROUP - 1)) < e4
        ahead = jnp.logical_or(other > sel, jnp.logical_and(other == sel, other_first))
        rank = rank + jnp.where(ahead, one, zero)
    top2 = rank < float(TOP_K)

    def group_sum(a):
        tot = a
        for d in range(1, EXPERTS_PER_GROUP):
            tot = tot + _win_roll(a, d, EXPERTS_PER_GROUP)
        return tot

    gscore = group_sum(jnp.where(top2, sel, zero))
    beaten = jnp.zeros(logits.shape, F32)
    for d in range(1, N_GROUPS):
        other = _win_roll(gscore, d * EXPERTS_PER_GROUP, N_EXPERTS)
        other_first = ((grp + d) & (N_GROUPS - 1)) < grp
        ahead = jnp.logical_or(other > gscore, jnp.logical_and(other == gscore, other_first))
        beaten = beaten + jnp.where(ahead, one, zero)
    chosen = jnp.logical_and(jnp.logical_and(beaten < 0.5, top2), lane < N_EXPERTS)
    w = jnp.where(chosen, scores, zero)
    den = group_sum(w)
    return jnp.where(chosen, w / den, zero)


def _outproj_kernel(ret_ref, na_ref, mla_ref, h_ref, w_ref, g1_ref, sh_ref, sc_ref, n2_ref,
                    rw_ref, rb_ref, hn_ref, u_ref, gate_ref):
    mix = (_dot(ret_ref[0], w_ref[0:256, :]) + _dot(na_ref[0], w_ref[256:768, :])
           + _dot(mla_ref[0], w_ref[768:1024, :]))
    hn = h_ref[0] + g1_ref[0] * mix
    hn_ref[0] = hn
    u = _rms(hn, n2_ref[...]) * (1.0 + sc_ref[0]) + sh_ref[0]
    u_ref[0] = u.astype(BF16)
    logits = jnp.dot(u, rw_ref[...], preferred_element_type=F32, precision=lax.Precision.HIGHEST)
    gate_ref[0] = _gates(logits, rb_ref[...])


def _out_projection(ret_o, na_o, mla_o, h, w_out, g1, sh2, sc2, n2, rw, rb, tm):
    b, l, d = h.shape
    per_batch = g1.shape[0] > 1
    mod_map = (lambda i, j: (i, 0, 0)) if per_batch else (lambda i, j: (0, 0, 0))
    const2 = lambda i, j: (0, 0)
    tok = lambda w: pl.BlockSpec((1, tm, w), lambda i, j: (i, j, 0))
    mod = pl.BlockSpec((1, 1, d), mod_map)
    return pl.pallas_call(
        _outproj_kernel,
        grid=(b, l // tm),
        in_specs=[tok(256), tok(512), tok(256), tok(d),
                  pl.BlockSpec(w_out.shape, const2), mod, mod, mod,
                  pl.BlockSpec((1, d), const2),
                  pl.BlockSpec(rw.shape, const2), pl.BlockSpec(rb.shape, const2)],
        out_specs=[tok(d), tok(d), tok(LANES)],
        out_shape=[jax.ShapeDtypeStruct((b, l, d), F32),
                   jax.ShapeDtypeStruct((b, l, d), BF16),
                   jax.ShapeDtypeStruct((b, l, LANES), F32)],
        compiler_params=_cparams(("arbitrary", "arbitrary")),
        name="out_projection",
    )(ret_o, na_o, mla_o, h, w_out, g1, sh2, sc2, n2, rw, rb)


def _moe_kernel(u_ref, gate_ref, h_ref, w13_ref, w2_ref, g2_ref, fg_ref, o_ref, acc, *, final_norm):
    e = pl.program_id(1)

    @pl.when(e == 0)
    def _():
        acc[...] = jnp.zeros_like(acc)

    a = _dot(u_ref[...], w13_ref[0])
    a1, a3 = a[:, :EXPERT_FF], a[:, EXPERT_FF:]
    gates = gate_ref[...]
    lane = lax.broadcasted_iota(jnp.int32, gates.shape, 1)
    gcol = jnp.sum(jnp.where(lane == e, gates, 0.0), axis=-1, keepdims=True)
    hdn = (a1 * _sigmoid(a1)) * a3 * gcol
    acc[...] += _dot(hdn.astype(BF16), w2_ref[0])

    @pl.when(e == N_EXPERTS - 1)
    def _():
        hn = h_ref[...] + g2_ref[0] * acc[...]
        if final_norm:
            hn = _rms(hn, fg_ref[...])
        o_ref[...] = hn


def _moe(u, gates, h, w13, w2, g2, fg, tokens_per_mod, tm, final_norm):
    t, d = h.shape
    per_batch = g2.shape[0] > 1
    tiles_per_mod = tokens_per_mod // tm
    mod_map = (lambda i, e: (i // tiles_per_mod, 0, 0)) if per_batch else (lambda i, e: (0, 0, 0))
    tok = lambda w: pl.BlockSpec((tm, w), lambda i, e: (i, 0))
    return pl.pallas_call(
        functools.partial(_moe_kernel, final_norm=final_norm),
        grid=(t // tm, N_EXPERTS),
        in_specs=[tok(d), tok(LANES), tok(d),
                  pl.BlockSpec((1, d, 2 * EXPERT_FF), lambda i, e: (e, 0, 0)),
                  pl.BlockSpec((1, EXPERT_FF, d), lambda i, e: (e, 0, 0)),
                  pl.BlockSpec((1, 1, d), mod_map),
                  pl.BlockSpec((1, d), lambda i, e: (0, 0))],
        out_specs=tok(d),
        out_shape=jax.ShapeDtypeStruct((t, d), F32),
        scratch_shapes=[pltpu.VMEM((tm, d), F32)],
        compiler_params=_cparams(("arbitrary", "arbitrary")),
        name="mixture_of_experts",
    )(u, gates, h, w13, w2, g2, fg)


def _rope_tables(seq, positional):
    if not positional:
        ones = jnp.ones((seq, LANES), F32)
        zeros = jnp.zeros((seq, LANES), F32)
        return ones, zeros, ones, zeros
    t = jnp.arange(seq)
    pos_r = (t // GRID_W).astype(F32)[:, None]
    pos_c = (t % GRID_W).astype(F32)[:, None]

    def axial(width):
        h = width // 2
        half = h // 2
        inv = ROPE_BASE ** (-jnp.arange(half, dtype=F32) / half)
        inv2 = jnp.concatenate([inv, inv])[None, :]
        ang = jnp.concatenate([pos_r * inv2, pos_c * inv2], axis=1)
        sign = jnp.tile(jnp.concatenate([-jnp.ones(half), jnp.ones(half)]), 2)[None, :]
        return jnp.cos(ang), jnp.sin(ang) * sign

    rc, rs = axial(RET_DK)
    ra = jnp.concatenate([rc, rc], axis=1)
    rb = jnp.concatenate([rs, rs], axis=1)
    mc, ms = axial(MLA_ROPE)
    ma = jnp.concatenate([jnp.ones((seq, MLA_NOPE), F32), mc, jnp.zeros((seq, 32), F32)], axis=1)
    mb = jnp.concatenate([jnp.zeros((seq, MLA_NOPE), F32), ms, jnp.zeros((seq, 32), F32)], axis=1)
    return ra, rb, ma, mb


def _pad_heads(w, heads, width):
    r = w.shape[0]
    w = w.reshape(r, heads, width)
    return jnp.pad(w, ((0, 0), (0, 0), (0, LANES - width))).reshape(r, heads * LANES)


def _layer_weights(w_in, w_uq, w_ukv):
    d = w_in.shape[0]
    kpe = w_in[:, _C_KPE:_C_KPE + MLA_ROPE]
    blk = jnp.concatenate([jnp.zeros((d, MLA_NOPE), w_in.dtype), kpe,
                           jnp.zeros((d, LANES - MLA_NOPE - MLA_ROPE), w_in.dtype)], axis=1)
    w_ext = jnp.concatenate([w_in[:, :_C_KPE]] + [blk] * MLA_HEADS, axis=1).astype(BF16)
    wuq = _pad_heads(w_uq, MLA_HEADS, MLA_NOPE + MLA_ROPE).astype(BF16)
    ukv = w_ukv.reshape(w_ukv.shape[0], MLA_HEADS, MLA_NOPE + MLA_VD)
    wuk = _pad_heads(ukv[:, :, :MLA_NOPE].reshape(w_ukv.shape[0], -1), MLA_HEADS, MLA_NOPE).astype(BF16)
    wuv = ukv[:, :, MLA_NOPE:].reshape(w_ukv.shape[0], -1).astype(BF16)
    return w_ext, wuq, wuk, wuv


def kernel(x, c, ctx, c_ctx, w_mod, b_mod, norm1_g, norm2_g, w_in, ret_decay_f, ret_decay_b, na_rpb,
           mla_q_norm, mla_kv_norm, w_uq, w_ukv, w_out, router_w, router_b, w1, w3, w2, final_norm_g):
    b, s, d = x.shape
    lc = ctx.shape[1]
    depth = w_mod.shape[0]
    rows = s // GRID_W
    tm_lat = 256
    tm_ctx = min(256, lc)

    n_mod = -(-(b + 1) // 8) * 8
    cv = jnp.concatenate([c, c_ctx[None, :], jnp.zeros((n_mod - b - 1, d), F32)], axis=0)
    mods = _modulation(cv, w_mod, b_mod).reshape(depth, n_mod, 6, d)

    tabs_lat = _rope_tables(s, True)
    tabs_ctx = _rope_tables(lc, False)
    na_idx, na_ok = _na_bias_index(rows)
    rw = jnp.pad(router_w.astype(F32), ((0, 0), (0, LANES - N_EXPERTS)))
    rb = jnp.pad(router_b.astype(F32), (0, LANES - N_EXPERTS)).reshape(1, LANES)
    zero_state = jnp.zeros((2, b, 2 * LANES, LANES), F32)

    h, hc = x, ctx
    for l in range(depth):
        last = l == depth - 1
        m_lat = [mods[l, :b, i][:, None, :] for i in range(6)]
        m_ctx = [mods[l, b:b + 1, i][:, None, :] for i in range(6)]
        w_ext, wuq, wuk, wuv = _layer_weights(w_in[l], w_uq[l], w_ukv[l])
        n1 = norm1_g[l].reshape(1, d)
        n2 = norm2_g[l].reshape(1, d)
        qn = mla_q_norm[l].reshape(1, -1)
        kvn = mla_kv_norm[l].reshape(1, -1)
        wo = w_out[l].astype(BF16)
        w13 = jnp.concatenate([w1[l], w3[l]], axis=-1).astype(BF16)
        w2b = w2[l].astype(BF16)
        fg = final_norm_g.reshape(1, d)

        ret_c, na_c, mq_c, mk_c, mv_c = _in_projection(
            hc, m_ctx[0], m_ctx[1], n1, w_ext, qn, kvn, wuq, wuk, wuv, tabs_ctx, tm_ctx)
        ret_l, na_l, mq_l, mk_l, mv_l = _in_projection(
            h, m_lat[0], m_lat[1], n1, w_ext, qn, kvn, wuq, wuk, wuv, tabs_lat, tm_lat)

        rtabs = _retention_tables(ret_decay_f[l], ret_decay_b[l])
        ro_c, st = _retention(ret_c, rtabs, zero_state)
        ro_l, _ = _retention(ret_l, rtabs, st)

        na_o = _na_attention(na_l, na_c, _na_bias(na_rpb[l], na_idx, na_ok))
        mla_o = _attention(mq_l, jnp.concatenate([mk_l, mk_c], axis=1),
                           jnp.concatenate([mv_l, mv_c], axis=1),
                           paired=False, tq=256, tk=768 if (s + lc) % 768 == 0 else 128,
                           name="latent_attention")

        hn, u2, gates = _out_projection(ro_l, na_o, mla_o, h, wo, m_lat[2], m_lat[3], m_lat[4],
                                        n2, rw, rb, tm_lat)
        h = _moe(u2.reshape(b * s, d), gates.reshape(b * s, LANES), hn.reshape(b * s, d),
                 w13, w2b, m_lat[5], fg, s, 512, last).reshape(b, s, d)

        if not last:
            nac_o = _attention(na_c[:, :, 0:512], na_c[:, :, 512:1024], na_c[:, :, 1024:1536],
                               paired=True, tq=tm_ctx, tk=min(256, lc), name="context_na_attention")
            mlac_o = _attention(mq_c, mk_c, mv_c, paired=False, tq=tm_ctx, tk=min(256, lc),
                                name="context_latent_attention")
            hcn, uc2, gates_c = _out_projection(ro_c, nac_o, mlac_o, hc, wo, m_ctx[2], m_ctx[3],
                                                m_ctx[4], n2, rw, rb, tm_ctx)
            hc = _moe(uc2.reshape(b * lc, d), gates_c.reshape(b * lc, LANES), hcn.reshape(b * lc, d),
                      w13, w2b, m_ctx[5], fg, lc, min(512, lc), False).reshape(b, lc, d)
    return h
```

```python
import functools

import numpy as np
import jax
import jax.numpy as jnp
from jax import lax
from jax.experimental import pallas as pl
from jax.experimental.pallas import tpu as pltpu

F32 = jnp.float32
BF16 = jnp.bfloat16

GRID_W = 64
EPS = 1e-6
ROPE_BASE = 10000.0
RET_HEADS, RET_DK, RET_DV, RET_CHUNK = 4, 64, 64, 128
NA_HEADS, NA_HD, NA_KH, NA_KW, NA_QBLOCK = 8, 64, 8, 16, 128
MLA_HEADS, MLA_NOPE, MLA_ROPE, MLA_VD = 4, 64, 32, 64
MLA_Q_RANK, MLA_KV_RANK = 384, 256
N_EXPERTS, N_GROUPS, TOP_K, EXPERT_FF = 16, 4, 2, 256
EXPERTS_PER_GROUP = N_EXPERTS // N_GROUPS

LANES = 128
HEAD = 64
NA_WIN_BLOCKS = 5
KV_BLOCK = 256
NEG_BIG = -1e30
VMEM_LIMIT = 48 * 1024 * 1024

_C_RET, _C_NA, _C_CQ, _C_CKV, _C_KPE = 0, 1024, 2560, 2944, 3200
_W_EXT = _C_KPE + MLA_HEADS * LANES
MLA_Q_SCALE = (MLA_NOPE + MLA_ROPE) ** -0.5 * float(np.log2(np.e))


def _cparams(sem):
    return pltpu.CompilerParams(dimension_semantics=sem, vmem_limit_bytes=VMEM_LIMIT)


def _dot(a, b):
    return jnp.dot(a, b, preferred_element_type=F32)


def _dot_nt(a, b):
    return lax.dot_general(a, b, (((1,), (1,)), ((), ())), preferred_element_type=F32)


def _sigmoid(x):
    return 1.0 / (1.0 + jnp.exp(-x))


def _win_roll(x, s, w):
    n = x.shape[-1]
    ax = x.ndim - 1
    lane = lax.broadcasted_iota(jnp.int32, x.shape, ax)
    fwd = pltpu.roll(x, n - s, ax)
    bwd = pltpu.roll(x, w - s, ax)
    return jnp.where((lane & (w - 1)) + s < w, fwd, bwd)


def _low_half(shape):
    lane = lax.broadcasted_iota(jnp.int32, shape, len(shape) - 1)
    return (lane & (LANES - 1)) < HEAD


def _mod_kernel(cv_ref, w_ref, b_ref, o_ref):
    cv = cv_ref[...]
    a = cv * _sigmoid(cv)
    o_ref[0] = jnp.dot(a, w_ref[0], preferred_element_type=F32,
                       precision=lax.Precision.HIGHEST) + b_ref[0]


def _modulation(cv, w_mod, b_mod):
    depth, d, n6 = w_mod.shape
    rows = cv.shape[0]
    tn = 1536
    return pl.pallas_call(
        _mod_kernel,
        grid=(depth, n6 // tn),
        in_specs=[pl.BlockSpec((rows, d), lambda l, j: (0, 0)),
                  pl.BlockSpec((1, d, tn), lambda l, j: (l, 0, j)),
                  pl.BlockSpec((1, 1, tn), lambda l, j: (l, 0, j))],
        out_specs=pl.BlockSpec((1, rows, tn), lambda l, j: (l, 0, j)),
        out_shape=jax.ShapeDtypeStruct((depth, rows, n6), F32),
        compiler_params=_cparams(("arbitrary", "arbitrary")),
        name="modulation",
    )(cv, w_mod, b_mod.reshape(depth, 1, n6))


def _rms(x, g):
    return x * lax.rsqrt(jnp.mean(x * x, axis=-1, keepdims=True) + EPS) * g


def _inproj_kernel(h_ref, sh_ref, sc_ref, g_ref, w_ref, qn_ref, kvn_ref, wuq_ref, wuk_ref, wuv_ref,
                   ra_ref, rb_ref, ma_ref, mb_ref,
                   ret_ref, na_ref, q_ref, k_ref, v_ref):
    x = h_ref[0]
    u = _rms(x, g_ref[...]) * (1.0 + sc_ref[0]) + sh_ref[0]
    ub = u.astype(BF16)

    r = _dot(ub, w_ref[:, _C_RET:_C_NA])
    qk = r[:, :512]
    ra = jnp.concatenate([ra_ref[...]] * 4, axis=1)
    rb = jnp.concatenate([rb_ref[...]] * 4, axis=1)
    qk = qk * ra + _win_roll(qk, 16, 32) * rb
    ret_ref[0, :, 0:256] = qk[:, :256] * (RET_DK ** -0.5)
    ret_ref[0, :, 256:512] = qk[:, 256:]
    ret_ref[0, :, 512:1024] = r[:, 512:]

    n = _dot(ub, w_ref[:, _C_NA:_C_CQ])
    na_ref[0, :, 0:512] = (n[:, :512] * (NA_HD ** -0.5)).astype(BF16)
    na_ref[0, :, 512:1536] = n[:, 512:].astype(BF16)

    ma = jnp.concatenate([ma_ref[...]] * 4, axis=1)
    mb = jnp.concatenate([mb_ref[...]] * 4, axis=1)
    cq = _rms(_dot(ub, w_ref[:, _C_CQ:_C_CKV]), qn_ref[...]).astype(BF16)
    q = _dot(cq, wuq_ref[...])
    q = q * ma + _win_roll(q, 8, 16) * mb
    q_ref[0] = (q * MLA_Q_SCALE).astype(BF16)
    ckv = _rms(_dot(ub, w_ref[:, _C_CKV:_C_KPE]), kvn_ref[...]).astype(BF16)
    kp = _dot(ub, w_ref[:, _C_KPE:_W_EXT])
    kp = kp * ma + _win_roll(kp, 8, 16) * mb
    k_ref[0] = (_dot(ckv, wuk_ref[...]) + kp).astype(BF16)
    v = _dot(ckv, wuv_ref[...])
    lane = lax.broadcasted_iota(jnp.int32, v.shape, 1)
    v_ref[0] = jnp.where((lane & (LANES - 1)) == HEAD, 1.0, v).astype(BF16)


def _in_projection(h, sh, sc, g, w_ext, qn, kvn, wuq, wuk, wuv, tabs, tm):
    b, l, d = h.shape
    per_batch = sh.shape[0] > 1
    mod_map = (lambda i, j: (i, 0, 0)) if per_batch else (lambda i, j: (0, 0, 0))
    const2 = lambda i, j: (0, 0)
    tab_spec = pl.BlockSpec((tm, LANES), lambda i, j: (j, 0))
    tok = lambda w: pl.BlockSpec((1, tm, w), lambda i, j: (i, j, 0))
    return pl.pallas_call(
        _inproj_kernel,
        grid=(b, l // tm),
        in_specs=[tok(d),
                  pl.BlockSpec((1, 1, d), mod_map), pl.BlockSpec((1, 1, d), mod_map),
                  pl.BlockSpec((1, d), const2),
                  pl.BlockSpec(w_ext.shape, const2),
                  pl.BlockSpec(qn.shape, const2), pl.BlockSpec(kvn.shape, const2),
                  pl.BlockSpec(wuq.shape, const2), pl.BlockSpec(wuk.shape, const2),
                  pl.BlockSpec(wuv.shape, const2),
                  tab_spec, tab_spec, tab_spec, tab_spec],
        out_specs=[tok(1024), tok(1536), tok(512), tok(512), tok(512)],
        out_shape=[jax.ShapeDtypeStruct((b, l, 1024), F32),
                   jax.ShapeDtypeStruct((b, l, 1536), BF16),
                   jax.ShapeDtypeStruct((b, l, 512), BF16),
                   jax.ShapeDtypeStruct((b, l, 512), BF16),
                   jax.ShapeDtypeStruct((b, l, 512), BF16)],
        compiler_params=_cparams(("arbitrary", "arbitrary")),
        name="in_projection",
    )(h, sh, sc, g, w_ext, qn, kvn, wuq, wuk, wuv, *tabs)


def _retention_kernel(q_ref, k_ref, v_ref, g_ref, dm_ref, qw_ref, kw_ref, gc_ref, s0_ref,
                      o_ref, sfin_ref, state, o_fwd, *, n_chunks):
    direction = pl.program_id(1)
    step = pl.program_id(2)
    c = RET_CHUNK

    @pl.when(step == 0)
    def _():
        state[...] = s0_ref[0, 0]

    q, k, v = q_ref[0], k_ref[0], v_ref[0]
    qw, kw = qw_ref[0], kw_ref[0]
    lo = _low_half((c, LANES))
    row_lo = lax.broadcasted_iota(jnp.int32, (LANES, LANES), 0) < HEAD
    same_head = row_lo == _low_half((LANES, LANES))
    outs = []
    for p in range(RET_HEADS // 2):
        sl = slice(LANES * p, LANES * (p + 1))
        qp, kp = q[:, sl], k[:, sl]
        kb, vb = kp.astype(BF16), v[:, sl].astype(BF16)
        qa = jnp.where(lo, qp, 0.0).astype(BF16)
        qb = jnp.where(lo, 0.0, qp).astype(BF16)
        sa = (_dot_nt(qa, kb) * dm_ref[0, 2 * p]).astype(BF16)
        sb = (_dot_nt(qb, kb) * dm_ref[0, 2 * p + 1]).astype(BF16)
        intra = jnp.where(lo, _dot(sa, vb), _dot(sb, vb))
        sp = state[sl, :]
        inter = _dot((qp * qw[:, sl]).astype(BF16), sp.astype(BF16))
        outs.append(intra + inter)
        kv = _dot((kp * kw[:, sl]).T.astype(BF16), vb)
        state[sl, :] = gc_ref[0, sl, :] * sp + jnp.where(same_head, kv, 0.0)
    o = jnp.concatenate(outs, axis=1)

    chunk = jnp.where(direction == 0, step, n_chunks - 1 - step)
    rows = pl.ds(pl.multiple_of(chunk * c, c), c)

    @pl.when(direction == 0)
    def _():
        o_fwd[rows, :] = o

    @pl.when(direction == 1)
    def _():
        tot = o_fwd[rows, :] + o
        g = g_ref[0]
        normed = []
        for p in range(RET_HEADS // 2):
            t = tot[:, LANES * p:LANES * (p + 1)]

            def head_mean(a):
                s_lo = jnp.sum(jnp.where(lo, a, 0.0), axis=-1, keepdims=True)
                s_hi = jnp.sum(jnp.where(lo, 0.0, a), axis=-1, keepdims=True)
                return jnp.where(lo, s_lo, s_hi) * (1.0 / HEAD)

            dlt = t - head_mean(t)
            normed.append(dlt * lax.rsqrt(head_mean(dlt * dlt) + EPS))
        on = jnp.concatenate(normed, axis=1)
        o_ref[0] = (on * (g * _sigmoid(g))).astype(BF16)

    @pl.when(step == n_chunks - 1)
    def _():
        sfin_ref[0, 0] = state[...]


def _retention(ret, tabs, s0):
    b, l, _ = ret.shape
    n = l // RET_CHUNK
    c = RET_CHUNK

    def tok(col):
        return pl.BlockSpec((1, c, 256),
                            lambda i, d, s: (i, jnp.where(d == 0, s, n - 1 - s), col))

    dirspec = lambda shape: pl.BlockSpec((1,) + shape, lambda i, d, s: (d,) + (0,) * len(shape))
    state_spec = pl.BlockSpec((1, 1, 2 * LANES, LANES), lambda i, d, s: (d, i, 0, 0))
    dm, qw, kw, gc = tabs
    return pl.pallas_call(
        functools.partial(_retention_kernel, n_chunks=n),
        grid=(b, 2, n),
        in_specs=[tok(0), tok(1), tok(2), tok(3),
                  dirspec((RET_HEADS, c, c)), dirspec((c, 256)), dirspec((c, 256)),
                  dirspec((2 * LANES, LANES)), state_spec],
        out_specs=[pl.BlockSpec((1, c, 256),
                                lambda i, d, s: (i, jnp.where(d == 0, n - 1, n - 1 - s), 0)),
                   state_spec],
        out_shape=[jax.ShapeDtypeStruct((b, l, 256), BF16),
                   jax.ShapeDtypeStruct((2, b, 2 * LANES, LANES), F32)],
        scratch_shapes=[pltpu.VMEM((2 * LANES, LANES), F32), pltpu.VMEM((l, 256), F32)],
        compiler_params=_cparams(("arbitrary", "arbitrary", "arbitrary")),
        name="retention",
    )(ret, ret, ret, ret, dm, qw, kw, gc, s0)


def _retention_tables(decay_f, decay_b):
    c = RET_CHUNK
    pos = jnp.arange(c, dtype=F32)
    diff = pos[:, None] - pos[None, :]
    lg = jnp.stack([jax.nn.log_sigmoid(decay_f.astype(F32)), jax.nn.log_sigmoid(decay_b.astype(F32))])
    lgh = lg[:, :, None, None]
    dm_f = jnp.where(diff >= 0, jnp.exp(lgh[0] * jnp.maximum(diff, 0.0)), 0.0)
    dm_b = jnp.where(diff < 0, jnp.exp(lgh[1] * jnp.maximum(-diff, 0.0)), 0.0)
    dm = jnp.stack([dm_f, dm_b])
    lanes = lambda t: jnp.repeat(t, RET_DK, axis=-1)
    lgc = lg[:, None, :]
    qw = lanes(jnp.exp(lgc * jnp.stack([pos + 1.0, c - pos])[:, :, None]))
    kw = lanes(jnp.exp(lgc * jnp.stack([c - 1.0 - pos, pos])[:, :, None]))
    gch = jnp.exp(lg * c)
    gc = jnp.broadcast_to(jnp.repeat(gch, RET_DK, axis=-1)[:, :, None], (2, RET_HEADS * RET_DK, LANES))
    return dm, qw, kw, gc


def _na_kernel(*refs, n_ctx):
    q_ref = refs[0]
    k_refs = refs[1:1 + NA_WIN_BLOCKS]
    v_refs = refs[1 + NA_WIN_BLOCKS:1 + 2 * NA_WIN_BLOCKS]
    kc_ref, vc_ref, bias_ref, o_ref = refs[1 + 2 * NA_WIN_BLOCKS:]
    tq = NA_QBLOCK
    lo = _low_half((tq, LANES))
    q = q_ref[0]
    for p in range(NA_HEADS // 2):
        sl = slice(LANES * p, LANES * (p + 1))
        qp = q[:, sl]
        kwin = jnp.concatenate([r[0, :, sl] for r in k_refs], axis=0)
        vwin = jnp.concatenate([r[0, :, sl] for r in v_refs], axis=0)
        kc, vc = kc_ref[0, :, sl], vc_ref[0, :, sl]
        halves = []
        for half in range(2):
            qm = jnp.where(lo, qp, 0) if half == 0 else jnp.where(lo, 0, qp)
            qm = qm.astype(BF16)
            s_loc = _dot_nt(qm, kwin) + bias_ref[0, 2 * p + half]
            s_ctx = _dot_nt(qm, kc)
            m = jnp.maximum(jnp.max(s_loc, axis=-1, keepdims=True),
                            jnp.max(s_ctx, axis=-1, keepdims=True))
            e_loc = jnp.exp(s_loc - m)
            e_ctx = jnp.exp(s_ctx - m)
            den = jnp.sum(e_loc, axis=-1, keepdims=True) + jnp.sum(e_ctx, axis=-1, keepdims=True)
            o = _dot(e_loc.astype(BF16), vwin) + _dot(e_ctx.astype(BF16), vc)
            halves.append(o / den)
        o_ref[0, :, sl] = jnp.where(lo, halves[0], halves[1]).astype(BF16)


def _na_variant(i, nblk):
    return jnp.where(i < 2, i, jnp.where(i <= nblk - 3, 2, i - (nblk - 2) + 3))


def _na_attention(na, na_ctx, bias):
    b, l, _ = na.shape
    n_ctx = na_ctx.shape[1]
    nblk = l // NA_QBLOCK
    tq = NA_QBLOCK

    def win(col, j):
        return pl.BlockSpec((1, tq, 512),
                            lambda bi, i: (bi, jnp.clip(i - 2, 0, nblk - NA_WIN_BLOCKS) + j, col))

    in_specs = ([pl.BlockSpec((1, tq, 512), lambda bi, i: (bi, i, 0))]
                + [win(1, j) for j in range(NA_WIN_BLOCKS)]
                + [win(2, j) for j in range(NA_WIN_BLOCKS)]
                + [pl.BlockSpec((1, n_ctx, 512), lambda bi, i: (bi, 0, 1)),
                   pl.BlockSpec((1, n_ctx, 512), lambda bi, i: (bi, 0, 2)),
                   pl.BlockSpec((1, NA_HEADS, tq, NA_WIN_BLOCKS * tq),
                                lambda bi, i: (_na_variant(i, nblk), 0, 0, 0))])
    args = [na] * (1 + 2 * NA_WIN_BLOCKS) + [na_ctx, na_ctx, bias]
    return pl.pallas_call(
        functools.partial(_na_kernel, n_ctx=n_ctx),
        grid=(b, nblk),
        in_specs=in_specs,
        out_specs=pl.BlockSpec((1, tq, 512), lambda bi, i: (bi, i, 0)),
        out_shape=jax.ShapeDtypeStruct((b, l, 512), BF16),
        compiler_params=_cparams(("arbitrary", "arbitrary")),
        name="neighbourhood_attention",
    )(*args)


def _na_bias_plan(rows):
    nblk = rows * GRID_W // NA_QBLOCK
    kh = min(NA_KH, rows)
    q_rows = NA_QBLOCK // GRID_W
    k_rows = NA_WIN_BLOCKS * q_rows
    plan = []
    for i in (0, 1, 2, nblk - 2, nblk - 1):
        kb = min(max(i - 2, 0), nblk - NA_WIN_BLOCKS)
        per_q = []
        for rr in range(q_rows):
            r = q_rows * i + rr
            r0 = min(max(r - kh // 2, 0), rows - kh)
            per_q.append([(q_rows * kb + kj - r + NA_KH - 1) if r0 <= q_rows * kb + kj < r0 + kh else None
                          for kj in range(k_rows)])
        plan.append(per_q)
    return plan


def _na_bias(rpb, plan):
    c = np.arange(GRID_W)
    c0 = np.clip(c - NA_KW // 2, 0, GRID_W - NA_KW)
    kc = np.arange(GRID_W)
    ok = (kc[None, :] >= c0[:, None]) & (kc[None, :] < c0[:, None] + NA_KW)
    idx = np.where(ok, kc[None, :] - c[:, None] + NA_KW - 1, 0).astype(np.int32)
    cols = jnp.take(rpb.astype(F32), jnp.asarray(idx.reshape(-1)), axis=2)
    cols = cols.reshape(NA_HEADS, 2 * NA_KH - 1, GRID_W, GRID_W)
    cols = jnp.where(jnp.asarray(ok)[None, None], cols, NEG_BIG)
    outside = jnp.full((NA_HEADS, GRID_W, GRID_W), NEG_BIG, F32)
    variants = []
    for per_q in plan:
        q_parts = [jnp.concatenate([outside if dr is None else cols[:, dr] for dr in per_k], axis=-1)
                   for per_k in per_q]
        variants.append(jnp.concatenate(q_parts, axis=1))
    return jnp.stack(variants)


def _latent_attn_kernel(q_ref, k_ref, v_ref, o_ref, s_scr, p_scr, al_scr, m_scr, acc_scr, *, tk):
    n_k = k_ref.shape[1] // tk
    n_heads = v_ref.shape[2] // LANES
    n_lane_blocks = tk // LANES
    assert n_k % 2 == 1, "the two-slot hand-over below assumes an odd number of key tiles"

    def key_rows(j):
        return pl.ds(pl.multiple_of(j * tk, tk), tk)

    def scores(j, slot):
        for h in range(n_heads):
            sl = slice(LANES * h, LANES * (h + 1))
            s_scr[slot, h] = _dot_nt(q_ref[0, :, sl], k_ref[0, key_rows(j), sl])

    def softmax(slot):
        for h in range(n_heads):
            part = s_scr[slot, h, :, 0:LANES]
            for c in range(1, n_lane_blocks):
                part = jnp.maximum(part, s_scr[slot, h, :, LANES * c:LANES * (c + 1)])
            m_old = m_scr[h]
            m_new = jnp.maximum(m_old, jnp.max(part, axis=-1, keepdims=True))
            al_scr[slot, h] = jnp.exp2(m_old - m_new)
            m_scr[h] = m_new
            for c in range(n_lane_blocks):
                cs = slice(LANES * c, LANES * (c + 1))
                p_scr[slot, h, :, cs] = jnp.exp2(s_scr[slot, h, :, cs] - m_new).astype(BF16)

    def values(j, slot):
        for h in range(n_heads):
            sl = slice(LANES * h, LANES * (h + 1))
            acc_scr[h] = al_scr[slot, h] * acc_scr[h] + _dot(p_scr[slot, h], v_ref[0, key_rows(j), sl])

    m_scr[...] = jnp.full(m_scr.shape, -jnp.inf, F32)
    acc_scr[...] = jnp.zeros(acc_scr.shape, F32)
    p_scr[1] = jnp.zeros(p_scr.shape[1:], BF16)
    al_scr[1] = jnp.ones(al_scr.shape[1:], F32)
    scores(0, 0)

    def body(i, carry):
        j = 2 * i
        scores(j + 1, 1)
        softmax(0)
        values(jnp.maximum(j - 1, 0), 1)
        scores(j + 2, 0)
        softmax(1)
        values(j, 0)
        return carry

    lax.fori_loop(0, (n_k - 1) // 2, body, 0)
    softmax(0)
    if n_k > 1:
        values(n_k - 2, 1)
    values(n_k - 1, 0)
    outs = []
    for h in range(n_heads):
        a = acc_scr[h]
        outs.append(a[:, 0:HEAD] / a[:, HEAD:HEAD + 1])
    o_ref[0] = jnp.concatenate(outs, axis=1).astype(BF16)


def _latent_attention(q, k, v, *, tq, tk, name):
    b, lq, w = q.shape
    lk = k.shape[1]
    n_heads = w // LANES
    resident = lambda: pl.BlockSpec((1, lk, w), lambda bi, i: (bi, 0, 0), pipeline_mode=pl.Buffered(1))
    return pl.pallas_call(
        functools.partial(_latent_attn_kernel, tk=tk),
        grid=(b, lq // tq),
        in_specs=[pl.BlockSpec((1, tq, w), lambda bi, i: (bi, i, 0)), resident(), resident()],
        out_specs=pl.BlockSpec((1, tq, n_heads * HEAD), lambda bi, i: (bi, i, 0)),
        out_shape=jax.ShapeDtypeStruct((b, lq, n_heads * HEAD), BF16),
        scratch_shapes=[pltpu.VMEM((2, n_heads, tq, tk), F32), pltpu.VMEM((2, n_heads, tq, tk), BF16),
                        pltpu.VMEM((2, n_heads, tq, LANES), F32),
                        pltpu.VMEM((n_heads, tq, LANES), F32), pltpu.VMEM((n_heads, tq, LANES), F32)],
        compiler_params=_cparams(("arbitrary", "arbitrary")),
        name=name,
    )(q, k, v)


def _pair_attn_kernel(q_ref, k_ref, v_ref, o_ref):
    tq = q_ref.shape[1]
    lo = _low_half((tq, LANES))
    for p in range(v_ref.shape[2] // LANES):
        sl = slice(LANES * p, LANES * (p + 1))
        qp, kp, vp = q_ref[0, :, sl], k_ref[0, :, sl], v_ref[0, :, sl]
        halves = []
        for half in range(2):
            qm = jnp.where(lo, qp, 0) if half == 0 else jnp.where(lo, 0, qp)
            s = _dot_nt(qm, kp)
            e = jnp.exp(s - jnp.max(s, axis=-1, keepdims=True))
            halves.append(_dot(e.astype(BF16), vp) / jnp.sum(e, axis=-1, keepdims=True))
        o_ref[0, :, sl] = jnp.where(lo, halves[0], halves[1]).astype(BF16)


def _pair_attention(q, k, v, *, tq, name):
    b, lq, w = q.shape
    lk = k.shape[1]
    return pl.pallas_call(
        _pair_attn_kernel,
        grid=(b, lq // tq),
        in_specs=[pl.BlockSpec((1, tq, w), lambda bi, i: (bi, i, 0)),
                  pl.BlockSpec((1, lk, w), lambda bi, i: (bi, 0, 0)),
                  pl.BlockSpec((1, lk, w), lambda bi, i: (bi, 0, 0))],
        out_specs=pl.BlockSpec((1, tq, w), lambda bi, i: (bi, i, 0)),
        out_shape=jax.ShapeDtypeStruct((b, lq, w), BF16),
        compiler_params=_cparams(("arbitrary", "arbitrary")),
        name=name,
    )(q, k, v)


def _gates(logits, bias):
    scores = _sigmoid(logits)
    sel = scores + bias
    lane = lax.broadcasted_iota(jnp.int32, logits.shape, 1)
    e4 = lane & (EXPERTS_PER_GROUP - 1)
    grp = (lane >> 2) & (N_GROUPS - 1)
    one, zero = jnp.float32(1.0), jnp.float32(0.0)

    rank = jnp.zeros(logits.shape, F32)
    for d in range(1, EXPERTS_PER_GROUP):
        other = _win_roll(sel, d, EXPERTS_PER_GROUP)
        other_first = ((e4 + d) & (EXPERTS_PER_GROUP - 1)) < e4
        ahead = jnp.logical_or(other > sel, jnp.logical_and(other == sel, other_first))
        rank = rank + jnp.where(ahead, one, zero)
    top2 = rank < float(TOP_K)

    def group_sum(a):
        tot = a
        for d in range(1, EXPERTS_PER_GROUP):
            tot = tot + _win_roll(a, d, EXPERTS_PER_GROUP)
        return tot

    gscore = group_sum(jnp.where(top2, sel, zero))
    beaten = jnp.zeros(logits.shape, F32)
    for d in range(1, N_GROUPS):
        other = _win_roll(gscore, d * EXPERTS_PER_GROUP, N_EXPERTS)
        other_first = ((grp + d) & (N_GROUPS - 1)) < grp
        ahead = jnp.logical_or(other > gscore, jnp.logical_and(other == gscore, other_first))
        beaten = beaten + jnp.where(ahead, one, zero)
    chosen = jnp.logical_and(jnp.logical_and(beaten < 0.5, top2), lane < N_EXPERTS)
    w = jnp.where(chosen, scores, zero)
    den = group_sum(w)
    return jnp.where(chosen, w / den, zero)


def _outproj_kernel(ret_ref, na_ref, mla_ref, h_ref, w_ref, g1_ref, sh_ref, sc_ref, n2_ref,
                    rw_ref, rb_ref, hn_ref, u_ref, gate_ref):
    mix = (_dot(ret_ref[0], w_ref[0:256, :]) + _dot(na_ref[0], w_ref[256:768, :])
           + _dot(mla_ref[0], w_ref[768:1024, :]))
    hn = h_ref[0] + g1_ref[0] * mix
    hn_ref[0] = hn
    u = _rms(hn, n2_ref[...]) * (1.0 + sc_ref[0]) + sh_ref[0]
    u_ref[0] = u.astype(BF16)
    logits = jnp.dot(u, rw_ref[...], preferred_element_type=F32, precision=lax.Precision.HIGHEST)
    gate_ref[0] = _gates(logits, rb_ref[...])


def _out_projection(ret_o, na_o, mla_o, h, w_out, g1, sh2, sc2, n2, rw, rb, tm):
    b, l, d = h.shape
    per_batch = g1.shape[0] > 1
    mod_map = (lambda i, j: (i, 0, 0)) if per_batch else (lambda i, j: (0, 0, 0))
    const2 = lambda i, j: (0, 0)
    tok = lambda w: pl.BlockSpec((1, tm, w), lambda i, j: (i, j, 0))
    mod = pl.BlockSpec((1, 1, d), mod_map)
    return pl.pallas_call(
        _outproj_kernel,
        grid=(b, l // tm),
        in_specs=[tok(256), tok(512), tok(256), tok(d),
                  pl.BlockSpec(w_out.shape, const2), mod, mod, mod,
                  pl.BlockSpec((1, d), const2),
                  pl.BlockSpec(rw.shape, const2), pl.BlockSpec(rb.shape, const2)],
        out_specs=[tok(d), tok(d), tok(LANES)],
        out_shape=[jax.ShapeDtypeStruct((b, l, d), F32),
                   jax.ShapeDtypeStruct((b, l, d), BF16),
                   jax.ShapeDtypeStruct((b, l, LANES), F32)],
        compiler_params=_cparams(("arbitrary", "arbitrary")),
        name="out_projection",
    )(ret_o, na_o, mla_o, h, w_out, g1, sh2, sc2, n2, rw, rb)


def _moe_kernel(u_ref, gate_ref, h_ref, w13_ref, w2_ref, g2_ref, fg_ref, o_ref, acc, *, final_norm):
    e = pl.program_id(1)

    @pl.when(e == 0)
    def _():
        acc[...] = jnp.zeros_like(acc)

    a = _dot(u_ref[...], w13_ref[0])
    a1, a3 = a[:, :EXPERT_FF], a[:, EXPERT_FF:]
    gates = gate_ref[...]
    lane = lax.broadcasted_iota(jnp.int32, gates.shape, 1)
    gcol = jnp.sum(jnp.where(lane == e, gates, 0.0), axis=-1, keepdims=True)
    hdn = (a1 * _sigmoid(a1)) * a3 * gcol
    acc[...] += _dot(hdn.astype(BF16), w2_ref[0])

    @pl.when(e == N_EXPERTS - 1)
    def _():
        hn = h_ref[...] + g2_ref[0] * acc[...]
        if final_norm:
            hn = _rms(hn, fg_ref[...])
        o_ref[...] = hn


def _moe(u, gates, h, w13, w2, g2, fg, tokens_per_mod, tm, final_norm):
    t, d = h.shape
    per_batch = g2.shape[0] > 1
    tiles_per_mod = tokens_per_mod // tm
    mod_map = (lambda i, e: (i // tiles_per_mod, 0, 0)) if per_batch else (lambda i, e: (0, 0, 0))
    tok = lambda w: pl.BlockSpec((tm, w), lambda i, e: (i, 0))
    return pl.pallas_call(
        functools.partial(_moe_kernel, final_norm=final_norm),
        grid=(t // tm, N_EXPERTS),
        in_specs=[tok(d), tok(LANES), tok(d),
                  pl.BlockSpec((1, d, 2 * EXPERT_FF), lambda i, e: (e, 0, 0)),
                  pl.BlockSpec((1, EXPERT_FF, d), lambda i, e: (e, 0, 0)),
                  pl.BlockSpec((1, 1, d), mod_map),
                  pl.BlockSpec((1, d), lambda i, e: (0, 0))],
        out_specs=tok(d),
        out_shape=jax.ShapeDtypeStruct((t, d), F32),
        scratch_shapes=[pltpu.VMEM((tm, d), F32)],
        compiler_params=_cparams(("arbitrary", "arbitrary")),
        name="mixture_of_experts",
    )(u, gates, h, w13, w2, g2, fg)


def _rope_tables(seq, positional):
    if not positional:
        ones = jnp.ones((seq, LANES), F32)
        zeros = jnp.zeros((seq, LANES), F32)
        return ones, zeros, ones, zeros
    t = jnp.arange(seq)
    pos_r = (t // GRID_W).astype(F32)[:, None]
    pos_c = (t % GRID_W).astype(F32)[:, None]

    def axial(width):
        h = width // 2
        half = h // 2
        inv = ROPE_BASE ** (-jnp.arange(half, dtype=F32) / half)
        inv2 = jnp.concatenate([inv, inv])[None, :]
        ang = jnp.concatenate([pos_r * inv2, pos_c * inv2], axis=1)
        sign = jnp.tile(jnp.concatenate([-jnp.ones(half), jnp.ones(half)]), 2)[None, :]
        return jnp.cos(ang), jnp.sin(ang) * sign

    rc, rs = axial(RET_DK)
    ra = jnp.concatenate([rc, rc], axis=1)
    rb = jnp.concatenate([rs, rs], axis=1)
    mc, ms = axial(MLA_ROPE)
    ma = jnp.concatenate([jnp.ones((seq, MLA_NOPE), F32), mc, jnp.zeros((seq, 32), F32)], axis=1)
    mb = jnp.concatenate([jnp.zeros((seq, MLA_NOPE), F32), ms, jnp.zeros((seq, 32), F32)], axis=1)
    return ra, rb, ma, mb


def _pad_heads(w, heads, width):
    r = w.shape[0]
    w = w.reshape(r, heads, width)
    return jnp.pad(w, ((0, 0), (0, 0), (0, LANES - width))).reshape(r, heads * LANES)


def _layer_weights(w_in, w_uq, w_ukv):
    d = w_in.shape[0]
    kpe = w_in[:, _C_KPE:_C_KPE + MLA_ROPE]
    blk = jnp.concatenate([jnp.zeros((d, MLA_NOPE), w_in.dtype), kpe,
                           jnp.zeros((d, LANES - MLA_NOPE - MLA_ROPE), w_in.dtype)], axis=1)
    w_ext = jnp.concatenate([w_in[:, :_C_KPE]] + [blk] * MLA_HEADS, axis=1).astype(BF16)
    wuq = _pad_heads(w_uq, MLA_HEADS, MLA_NOPE + MLA_ROPE).astype(BF16)
    ukv = w_ukv.reshape(w_ukv.shape[0], MLA_HEADS, MLA_NOPE + MLA_VD)
    wuk = _pad_heads(ukv[:, :, :MLA_NOPE].reshape(w_ukv.shape[0], -1), MLA_HEADS, MLA_NOPE).astype(BF16)
    wuv = _pad_heads(ukv[:, :, MLA_NOPE:].reshape(w_ukv.shape[0], -1), MLA_HEADS, MLA_VD).astype(BF16)
    return w_ext, wuq, wuk, wuv


def _latent_key_tile(n_keys):
    for tk in range(768, 0, -LANES):
        if n_keys % tk == 0 and (n_keys // tk) % 2 == 1:
            return tk
    raise ValueError(f"no odd tiling of {n_keys} keys")


def kernel(x, c, ctx, c_ctx, w_mod, b_mod, norm1_g, norm2_g, w_in, ret_decay_f, ret_decay_b, na_rpb,
           mla_q_norm, mla_kv_norm, w_uq, w_ukv, w_out, router_w, router_b, w1, w3, w2, final_norm_g):
    b, s, d = x.shape
    lc = ctx.shape[1]
    depth = w_mod.shape[0]
    rows = s // GRID_W
    tm_lat = 256
    tm_ctx = min(256, lc)

    n_mod = -(-(b + 1) // 8) * 8
    cv = jnp.concatenate([c, c_ctx[None, :], jnp.zeros((n_mod - b - 1, d), F32)], axis=0)
    mods = _modulation(cv, w_mod, b_mod).reshape(depth, n_mod, 6, d)

    tabs_lat = _rope_tables(s, True)
    tabs_ctx = _rope_tables(lc, False)
    na_plan = _na_bias_plan(rows)
    rw = jnp.pad(router_w.astype(F32), ((0, 0), (0, LANES - N_EXPERTS)))
    rb = jnp.pad(router_b.astype(F32), (0, LANES - N_EXPERTS)).reshape(1, LANES)
    zero_state = jnp.zeros((2, b, 2 * LANES, LANES), F32)

    h, hc = x, ctx
    for l in range(depth):
        last = l == depth - 1
        m_lat = [mods[l, :b, i][:, None, :] for i in range(6)]
        m_ctx = [mods[l, b:b + 1, i][:, None, :] for i in range(6)]
        w_ext, wuq, wuk, wuv = _layer_weights(w_in[l], w_uq[l], w_ukv[l])
        n1 = norm1_g[l].reshape(1, d)
        n2 = norm2_g[l].reshape(1, d)
        qn = mla_q_norm[l].reshape(1, -1)
        kvn = mla_kv_norm[l].reshape(1, -1)
        wo = w_out[l].astype(BF16)
        w13 = jnp.concatenate([w1[l], w3[l]], axis=-1).astype(BF16)
        w2b = w2[l].astype(BF16)
        fg = final_norm_g.reshape(1, d)

        ret_c, na_c, mq_c, mk_c, mv_c = _in_projection(
            hc, m_ctx[0], m_ctx[1], n1, w_ext, qn, kvn, wuq, wuk, wuv, tabs_ctx, tm_ctx)
        ret_l, na_l, mq_l, mk_l, mv_l = _in_projection(
            h, m_lat[0], m_lat[1], n1, w_ext, qn, kvn, wuq, wuk, wuv, tabs_lat, tm_lat)

        rtabs = _retention_tables(ret_decay_f[l], ret_decay_b[l])
        ro_c, st = _retention(ret_c, rtabs, zero_state)
        ro_l, _ = _retention(ret_l, rtabs, st)

        na_o = _na_attention(na_l, na_c, _na_bias(na_rpb[l], na_plan))
        mla_o = _latent_attention(mq_l, jnp.concatenate([mk_l, mk_c], axis=1),
                                  jnp.concatenate([mv_l, mv_c], axis=1),
                                  tq=256, tk=_latent_key_tile(s + lc), name="latent_attention")

        hn, u2, gates = _out_projection(ro_l, na_o, mla_o, h, wo, m_lat[2], m_lat[3], m_lat[4],
                                        n2, rw, rb, tm_lat)
        h = _moe(u2.reshape(b * s, d), gates.reshape(b * s, LANES), hn.reshape(b * s, d),
                 w13, w2b, m_lat[5], fg, s, 512, last).reshape(b, s, d)

        if not last:
            nac_o = _pair_attention(na_c[:, :, 0:512], na_c[:, :, 512:1024], na_c[:, :, 1024:1536],
                                    tq=tm_ctx, name="context_na_attention")
            mlac_o = _latent_attention(mq_c, mk_c, mv_c, tq=tm_ctx, tk=lc,
                                       name="context_latent_attention")
            hcn, uc2, gates_c = _out_projection(ro_c, nac_o, mlac_o, hc, wo, m_ctx[2], m_ctx[3],
                                                m_ctx[4], n2, rw, rb, tm_ctx)
            hc = _moe(uc2.reshape(b * lc, d), gates_c.reshape(b * lc, LANES), hcn.reshape(b * lc, d),
                      w13, w2b, m_ctx[5], fg, lc, min(512, lc), False).reshape(b, lc, d)
    return h
```

```python
import functools

import numpy as np
import jax
import jax.numpy as jnp
from jax import lax
from jax.experimental import pallas as pl
from jax.experimental.pallas import tpu as pltpu

F32 = jnp.float32
BF16 = jnp.bfloat16

GRID_W = 64
EPS = 1e-6
ROPE_BASE = 10000.0
RET_HEADS, RET_DK, RET_DV, RET_CHUNK = 4, 64, 64, 128
NA_HEADS, NA_HD, NA_KH, NA_KW, NA_QBLOCK = 8, 64, 8, 16, 128
MLA_HEADS, MLA_NOPE, MLA_ROPE, MLA_VD = 4, 64, 32, 64
MLA_Q_RANK, MLA_KV_RANK = 384, 256
N_EXPERTS, N_GROUPS, TOP_K, EXPERT_FF = 16, 4, 2, 256
EXPERTS_PER_GROUP = N_EXPERTS // N_GROUPS

LANES = 128
HEAD = 64
NA_WIN_BLOCKS = 5
KV_BLOCK = 256
NEG_BIG = -1e30
VMEM_LIMIT = 48 * 1024 * 1024

_C_RET, _C_NA, _C_CQ, _C_CKV, _C_KPE = 0, 1024, 2560, 2944, 3200
_W_EXT = _C_KPE + MLA_HEADS * LANES
MLA_Q_SCALE = (MLA_NOPE + MLA_ROPE) ** -0.5 * float(np.log2(np.e))


def _cparams(sem):
    return pltpu.CompilerParams(dimension_semantics=sem, vmem_limit_bytes=VMEM_LIMIT)


def _dot(a, b):
    return jnp.dot(a, b, preferred_element_type=F32)


def _dot_nt(a, b):
    return lax.dot_general(a, b, (((1,), (1,)), ((), ())), preferred_element_type=F32)


def _sigmoid(x):
    return 1.0 / (1.0 + jnp.exp(-x))


def _win_roll(x, s, w):
    n = x.shape[-1]
    ax = x.ndim - 1
    lane = lax.broadcasted_iota(jnp.int32, x.shape, ax)
    fwd = pltpu.roll(x, n - s, ax)
    bwd = pltpu.roll(x, w - s, ax)
    return jnp.where((lane & (w - 1)) + s < w, fwd, bwd)


def _low_half(shape):
    lane = lax.broadcasted_iota(jnp.int32, shape, len(shape) - 1)
    return (lane & (LANES - 1)) < HEAD


def _mod_kernel(cv_ref, w_ref, b_ref, o_ref):
    cv = cv_ref[...]
    a = cv * _sigmoid(cv)
    o_ref[0] = jnp.dot(a, w_ref[0], preferred_element_type=F32,
                       precision=lax.Precision.HIGHEST) + b_ref[0]


def _modulation(cv, w_mod, b_mod):
    depth, d, n6 = w_mod.shape
    rows = cv.shape[0]
    tn = 1536
    return pl.pallas_call(
        _mod_kernel,
        grid=(depth, n6 // tn),
        in_specs=[pl.BlockSpec((rows, d), lambda l, j: (0, 0)),
                  pl.BlockSpec((1, d, tn), lambda l, j: (l, 0, j)),
                  pl.BlockSpec((1, 1, tn), lambda l, j: (l, 0, j))],
        out_specs=pl.BlockSpec((1, rows, tn), lambda l, j: (l, 0, j)),
        out_shape=jax.ShapeDtypeStruct((depth, rows, n6), F32),
        compiler_params=_cparams(("arbitrary", "arbitrary")),
        name="modulation",
    )(cv, w_mod, b_mod.reshape(depth, 1, n6))


def _rms(x, g):
    return x * lax.rsqrt(jnp.mean(x * x, axis=-1, keepdims=True) + EPS) * g


def _inproj_kernel(h_ref, sh_ref, sc_ref, g_ref, w_ref, qn_ref, kvn_ref, wuq_ref, wuk_ref, wuv_ref,
                   ra_ref, rb_ref, ma_ref, mb_ref,
                   ret_ref, na_ref, q_ref, k_ref, v_ref):
    x = h_ref[0]
    u = _rms(x, g_ref[...]) * (1.0 + sc_ref[0]) + sh_ref[0]
    ub = u.astype(BF16)

    r = _dot(ub, w_ref[:, _C_RET:_C_NA])
    qk = r[:, :512]
    ra = jnp.concatenate([ra_ref[...]] * 4, axis=1)
    rb = jnp.concatenate([rb_ref[...]] * 4, axis=1)
    qk = qk * ra + _win_roll(qk, 16, 32) * rb
    ret_ref[0, :, 0:256] = qk[:, :256] * (RET_DK ** -0.5)
    ret_ref[0, :, 256:512] = qk[:, 256:]
    ret_ref[0, :, 512:1024] = r[:, 512:]

    n = _dot(ub, w_ref[:, _C_NA:_C_CQ])
    na_ref[0, :, 0:512] = (n[:, :512] * (NA_HD ** -0.5)).astype(BF16)
    na_ref[0, :, 512:1536] = n[:, 512:].astype(BF16)

    ma = jnp.concatenate([ma_ref[...]] * 4, axis=1)
    mb = jnp.concatenate([mb_ref[...]] * 4, axis=1)
    cq = _rms(_dot(ub, w_ref[:, _C_CQ:_C_CKV]), qn_ref[...]).astype(BF16)
    q = _dot(cq, wuq_ref[...])
    q = q * ma + _win_roll(q, 8, 16) * mb
    q_ref[0] = (q * MLA_Q_SCALE).astype(BF16)
    ckv = _rms(_dot(ub, w_ref[:, _C_CKV:_C_KPE]), kvn_ref[...]).astype(BF16)
    kp = _dot(ub, w_ref[:, _C_KPE:_W_EXT])
    kp = kp * ma + _win_roll(kp, 8, 16) * mb
    k_ref[0] = (_dot(ckv, wuk_ref[...]) + kp).astype(BF16)
    v = _dot(ckv, wuv_ref[...])
    lane = lax.broadcasted_iota(jnp.int32, v.shape, 1)
    v_ref[0] = jnp.where((lane & (LANES - 1)) == HEAD, 1.0, v).astype(BF16)


def _in_projection(h, sh, sc, g, w_ext, qn, kvn, wuq, wuk, wuv, tabs, tm):
    b, l, d = h.shape
    per_batch = sh.shape[0] > 1
    mod_map = (lambda i, j: (i, 0, 0)) if per_batch else (lambda i, j: (0, 0, 0))
    const2 = lambda i, j: (0, 0)
    tab_spec = pl.BlockSpec((tm, LANES), lambda i, j: (j, 0))
    tok = lambda w: pl.BlockSpec((1, tm, w), lambda i, j: (i, j, 0))
    return pl.pallas_call(
        _inproj_kernel,
        grid=(b, l // tm),
        in_specs=[tok(d),
                  pl.BlockSpec((1, 1, d), mod_map), pl.BlockSpec((1, 1, d), mod_map),
                  pl.BlockSpec((1, d), const2),
                  pl.BlockSpec(w_ext.shape, const2),
                  pl.BlockSpec(qn.shape, const2), pl.BlockSpec(kvn.shape, const2),
                  pl.BlockSpec(wuq.shape, const2), pl.BlockSpec(wuk.shape, const2),
                  pl.BlockSpec(wuv.shape, const2),
                  tab_spec, tab_spec, tab_spec, tab_spec],
        out_specs=[tok(1024), tok(1536), tok(512), tok(512), tok(512)],
        out_shape=[jax.ShapeDtypeStruct((b, l, 1024), F32),
                   jax.ShapeDtypeStruct((b, l, 1536), BF16),
                   jax.ShapeDtypeStruct((b, l, 512), BF16),
                   jax.ShapeDtypeStruct((b, l, 512), BF16),
                   jax.ShapeDtypeStruct((b, l, 512), BF16)],
        compiler_params=_cparams(("arbitrary", "arbitrary")),
        name="in_projection",
    )(h, sh, sc, g, w_ext, qn, kvn, wuq, wuk, wuv, *tabs)


def _retention_kernel(q_ref, k_ref, v_ref, g_ref, dm_ref, qw_ref, kw_ref, gc_ref, s0_ref,
                      o_ref, sfin_ref, state, o_fwd, *, n_chunks):
    direction = pl.program_id(1)
    step = pl.program_id(2)
    c = RET_CHUNK

    @pl.when(step == 0)
    def _():
        state[...] = s0_ref[0, 0]

    q, k, v = q_ref[0], k_ref[0], v_ref[0]
    qw, kw = qw_ref[0], kw_ref[0]
    lo = _low_half((c, LANES))
    row_lo = lax.broadcasted_iota(jnp.int32, (LANES, LANES), 0) < HEAD
    same_head = row_lo == _low_half((LANES, LANES))
    outs = []
    for p in range(RET_HEADS // 2):
        sl = slice(LANES * p, LANES * (p + 1))
        qp, kp = q[:, sl], k[:, sl]
        kb, vb = kp.astype(BF16), v[:, sl].astype(BF16)
        qa = jnp.where(lo, qp, 0.0).astype(BF16)
        qb = jnp.where(lo, 0.0, qp).astype(BF16)
        sa = (_dot_nt(qa, kb) * dm_ref[0, 2 * p]).astype(BF16)
        sb = (_dot_nt(qb, kb) * dm_ref[0, 2 * p + 1]).astype(BF16)
        intra = jnp.where(lo, _dot(sa, vb), _dot(sb, vb))
        sp = state[sl, :]
        inter = _dot((qp * qw[:, sl]).astype(BF16), sp.astype(BF16))
        outs.append(intra + inter)
        kv = _dot((kp * kw[:, sl]).T.astype(BF16), vb)
        state[sl, :] = gc_ref[0, sl, :] * sp + jnp.where(same_head, kv, 0.0)
    o = jnp.concatenate(outs, axis=1)

    chunk = jnp.where(direction == 0, step, n_chunks - 1 - step)
    rows = pl.ds(pl.multiple_of(chunk * c, c), c)

    @pl.when(direction == 0)
    def _():
        o_fwd[rows, :] = o

    @pl.when(direction == 1)
    def _():
        tot = o_fwd[rows, :] + o
        g = g_ref[0]
        normed = []
        for p in range(RET_HEADS // 2):
            t = tot[:, LANES * p:LANES * (p + 1)]

            def head_mean(a):
                s_lo = jnp.sum(jnp.where(lo, a, 0.0), axis=-1, keepdims=True)
                s_hi = jnp.sum(jnp.where(lo, 0.0, a), axis=-1, keepdims=True)
                return jnp.where(lo, s_lo, s_hi) * (1.0 / HEAD)

            dlt = t - head_mean(t)
            normed.append(dlt * lax.rsqrt(head_mean(dlt * dlt) + EPS))
        on = jnp.concatenate(normed, axis=1)
        o_ref[0] = (on * (g * _sigmoid(g))).astype(BF16)

    @pl.when(step == n_chunks - 1)
    def _():
        sfin_ref[0, 0] = state[...]


def _retention(ret, tabs, s0):
    b, l, _ = ret.shape
    n = l // RET_CHUNK
    c = RET_CHUNK

    def tok(col):
        return pl.BlockSpec((1, c, 256),
                            lambda i, d, s: (i, jnp.where(d == 0, s, n - 1 - s), col))

    dirspec = lambda shape: pl.BlockSpec((1,) + shape, lambda i, d, s: (d,) + (0,) * len(shape))
    state_spec = pl.BlockSpec((1, 1, 2 * LANES, LANES), lambda i, d, s: (d, i, 0, 0))
    dm, qw, kw, gc = tabs
    return pl.pallas_call(
        functools.partial(_retention_kernel, n_chunks=n),
        grid=(b, 2, n),
        in_specs=[tok(0), tok(1), tok(2), tok(3),
                  dirspec((RET_HEADS, c, c)), dirspec((c, 256)), dirspec((c, 256)),
                  dirspec((2 * LANES, LANES)), state_spec],
        out_specs=[pl.BlockSpec((1, c, 256),
                                lambda i, d, s: (i, jnp.where(d == 0, n - 1, n - 1 - s), 0)),
                   state_spec],
        out_shape=[jax.ShapeDtypeStruct((b, l, 256), BF16),
                   jax.ShapeDtypeStruct((2, b, 2 * LANES, LANES), F32)],
        scratch_shapes=[pltpu.VMEM((2 * LANES, LANES), F32), pltpu.VMEM((l, 256), F32)],
        compiler_params=_cparams(("arbitrary", "arbitrary", "arbitrary")),
        name="retention",
    )(ret, ret, ret, ret, dm, qw, kw, gc, s0)


def _retention_tables(decay_f, decay_b):
    c = RET_CHUNK
    pos = jnp.arange(c, dtype=F32)
    diff = pos[:, None] - pos[None, :]
    lg = jnp.stack([jax.nn.log_sigmoid(decay_f.astype(F32)), jax.nn.log_sigmoid(decay_b.astype(F32))])
    lgh = lg[:, :, None, None]
    dm_f = jnp.where(diff >= 0, jnp.exp(lgh[0] * jnp.maximum(diff, 0.0)), 0.0)
    dm_b = jnp.where(diff < 0, jnp.exp(lgh[1] * jnp.maximum(-diff, 0.0)), 0.0)
    dm = jnp.stack([dm_f, dm_b])
    lanes = lambda t: jnp.repeat(t, RET_DK, axis=-1)
    lgc = lg[:, None, :]
    qw = lanes(jnp.exp(lgc * jnp.stack([pos + 1.0, c - pos])[:, :, None]))
    kw = lanes(jnp.exp(lgc * jnp.stack([c - 1.0 - pos, pos])[:, :, None]))
    gch = jnp.exp(lg * c)
    gc = jnp.broadcast_to(jnp.repeat(gch, RET_DK, axis=-1)[:, :, None], (2, RET_HEADS * RET_DK, LANES))
    return dm, qw, kw, gc


def _na_kernel(*refs, n_ctx):
    q_ref = refs[0]
    k_refs = refs[1:1 + NA_WIN_BLOCKS]
    v_refs = refs[1 + NA_WIN_BLOCKS:1 + 2 * NA_WIN_BLOCKS]
    kc_ref, vc_ref, bias_ref, o_ref = refs[1 + 2 * NA_WIN_BLOCKS:]
    tq = NA_QBLOCK
    lo = _low_half((tq, LANES))
    q = q_ref[0]
    for p in range(NA_HEADS // 2):
        sl = slice(LANES * p, LANES * (p + 1))
        qp = q[:, sl]
        kwin = jnp.concatenate([r[0, :, sl] for r in k_refs], axis=0)
        vwin = jnp.concatenate([r[0, :, sl] for r in v_refs], axis=0)
        kc, vc = kc_ref[0, :, sl], vc_ref[0, :, sl]
        halves = []
        for half in range(2):
            qm = jnp.where(lo, qp, 0) if half == 0 else jnp.where(lo, 0, qp)
            qm = qm.astype(BF16)
            s_loc = _dot_nt(qm, kwin) + bias_ref[0, 2 * p + half]
            s_ctx = _dot_nt(qm, kc)
            m = jnp.maximum(jnp.max(s_loc, axis=-1, keepdims=True),
                            jnp.max(s_ctx, axis=-1, keepdims=True))
            e_loc = jnp.exp(s_loc - m)
            e_ctx = jnp.exp(s_ctx - m)
            den = jnp.sum(e_loc, axis=-1, keepdims=True) + jnp.sum(e_ctx, axis=-1, keepdims=True)
            o = _dot(e_loc.astype(BF16), vwin) + _dot(e_ctx.astype(BF16), vc)
            halves.append(o / den)
        o_ref[0, :, sl] = jnp.where(lo, halves[0], halves[1]).astype(BF16)


def _na_variant(i, nblk):
    return jnp.where(i < 2, i, jnp.where(i <= nblk - 3, 2, i - (nblk - 2) + 3))


def _na_attention(na, na_ctx, bias):
    b, l, _ = na.shape
    n_ctx = na_ctx.shape[1]
    nblk = l // NA_QBLOCK
    tq = NA_QBLOCK

    def win(col, j):
        return pl.BlockSpec((1, tq, 512),
                            lambda bi, i: (bi, jnp.clip(i - 2, 0, nblk - NA_WIN_BLOCKS) + j, col))

    in_specs = ([pl.BlockSpec((1, tq, 512), lambda bi, i: (bi, i, 0))]
                + [win(1, j) for j in range(NA_WIN_BLOCKS)]
                + [win(2, j) for j in range(NA_WIN_BLOCKS)]
                + [pl.BlockSpec((1, n_ctx, 512), lambda bi, i: (bi, 0, 1)),
                   pl.BlockSpec((1, n_ctx, 512), lambda bi, i: (bi, 0, 2)),
                   pl.BlockSpec((1, NA_HEADS, tq, NA_WIN_BLOCKS * tq),
                                lambda bi, i: (_na_variant(i, nblk), 0, 0, 0))])
    args = [na] * (1 + 2 * NA_WIN_BLOCKS) + [na_ctx, na_ctx, bias]
    return pl.pallas_call(
        functools.partial(_na_kernel, n_ctx=n_ctx),
        grid=(b, nblk),
        in_specs=in_specs,
        out_specs=pl.BlockSpec((1, tq, 512), lambda bi, i: (bi, i, 0)),
        out_shape=jax.ShapeDtypeStruct((b, l, 512), BF16),
        compiler_params=_cparams(("arbitrary", "arbitrary")),
        name="neighbourhood_attention",
    )(*args)


def _na_bias_plan(rows):
    nblk = rows * GRID_W // NA_QBLOCK
    kh = min(NA_KH, rows)
    q_rows = NA_QBLOCK // GRID_W
    k_rows = NA_WIN_BLOCKS * q_rows
    plan = []
    for i in (0, 1, 2, nblk - 2, nblk - 1):
        kb = min(max(i - 2, 0), nblk - NA_WIN_BLOCKS)
        per_q = []
        for rr in range(q_rows):
            r = q_rows * i + rr
            r0 = min(max(r - kh // 2, 0), rows - kh)
            per_q.append([(q_rows * kb + kj - r + NA_KH - 1) if r0 <= q_rows * kb + kj < r0 + kh else None
                          for kj in range(k_rows)])
        plan.append(per_q)
    return plan


def _na_bias(rpb, plan):
    c = np.arange(GRID_W)
    c0 = np.clip(c - NA_KW // 2, 0, GRID_W - NA_KW)
    kc = np.arange(GRID_W)
    ok = (kc[None, :] >= c0[:, None]) & (kc[None, :] < c0[:, None] + NA_KW)
    idx = np.where(ok, kc[None, :] - c[:, None] + NA_KW - 1, 0).astype(np.int32)
    cols = jnp.take(rpb.astype(F32), jnp.asarray(idx.reshape(-1)), axis=2)
    cols = cols.reshape(NA_HEADS, 2 * NA_KH - 1, GRID_W, GRID_W)
    cols = jnp.where(jnp.asarray(ok)[None, None], cols, NEG_BIG)
    outside = jnp.full((NA_HEADS, GRID_W, GRID_W), NEG_BIG, F32)
    variants = []
    for per_q in plan:
        q_parts = [jnp.concatenate([outside if dr is None else cols[:, dr] for dr in per_k], axis=-1)
                   for per_k in per_q]
        variants.append(jnp.concatenate(q_parts, axis=1))
    return jnp.stack(variants)


def _latent_attn_kernel(q_ref, k_ref, v_ref, o_ref, s_scr, p_scr, al_scr, m_scr, acc_scr, *, tk):
    n_k = k_ref.shape[1] // tk
    n_heads = v_ref.shape[2] // LANES
    n_lane_blocks = tk // LANES
    assert n_k % 2 == 1, "the two-slot hand-over below assumes an odd number of key tiles"

    def key_rows(j):
        return pl.ds(pl.multiple_of(j * tk, tk), tk)

    def scores(j, slot):
        for h in range(n_heads):
            sl = slice(LANES * h, LANES * (h + 1))
            s_scr[slot, h] = _dot_nt(q_ref[0, :, sl], k_ref[0, key_rows(j), sl])

    def softmax(slot):
        for h in range(n_heads):
            part = s_scr[slot, h, :, 0:LANES]
            for c in range(1, n_lane_blocks):
                part = jnp.maximum(part, s_scr[slot, h, :, LANES * c:LANES * (c + 1)])
            m_old = m_scr[h]
            m_new = jnp.maximum(m_old, jnp.max(part, axis=-1, keepdims=True))
            al_scr[slot, h] = jnp.exp2(m_old - m_new)
            m_scr[h] = m_new
            for c in range(n_lane_blocks):
                cs = slice(LANES * c, LANES * (c + 1))
                p_scr[slot, h, :, cs] = jnp.exp2(s_scr[slot, h, :, cs] - m_new).astype(BF16)

    def values(j, slot):
        for h in range(n_heads):
            sl = slice(LANES * h, LANES * (h + 1))
            acc_scr[h] = al_scr[slot, h] * acc_scr[h] + _dot(p_scr[slot, h], v_ref[0, key_rows(j), sl])

    m_scr[...] = jnp.full(m_scr.shape, -jnp.inf, F32)
    acc_scr[...] = jnp.zeros(acc_scr.shape, F32)
    p_scr[1] = jnp.zeros(p_scr.shape[1:], BF16)
    al_scr[1] = jnp.ones(al_scr.shape[1:], F32)
    scores(0, 0)

    def body(i, carry):
        j = 2 * i
        scores(j + 1, 1)
        softmax(0)
        values(jnp.maximum(j - 1, 0), 1)
        scores(j + 2, 0)
        softmax(1)
        values(j, 0)
        return carry

    lax.fori_loop(0, (n_k - 1) // 2, body, 0)
    softmax(0)
    if n_k > 1:
        values(n_k - 2, 1)
    values(n_k - 1, 0)
    outs = []
    for h in range(n_heads):
        a = acc_scr[h]
        outs.append(a[:, 0:HEAD] / a[:, HEAD:HEAD + 1])
    o_ref[0] = jnp.concatenate(outs, axis=1).astype(BF16)


def _latent_attention(q, k, v, *, tq, tk, name):
    b, lq, w = q.shape
    lk = k.shape[1]
    n_heads = w // LANES
    resident = lambda: pl.BlockSpec((1, lk, w), lambda bi, i: (bi, 0, 0), pipeline_mode=pl.Buffered(1))
    return pl.pallas_call(
        functools.partial(_latent_attn_kernel, tk=tk),
        grid=(b, lq // tq),
        in_specs=[pl.BlockSpec((1, tq, w), lambda bi, i: (bi, i, 0)), resident(), resident()],
        out_specs=pl.BlockSpec((1, tq, n_heads * HEAD), lambda bi, i: (bi, i, 0)),
        out_shape=jax.ShapeDtypeStruct((b, lq, n_heads * HEAD), BF16),
        scratch_shapes=[pltpu.VMEM((2, n_heads, tq, tk), F32), pltpu.VMEM((2, n_heads, tq, tk), BF16),
                        pltpu.VMEM((2, n_heads, tq, LANES), F32),
                        pltpu.VMEM((n_heads, tq, LANES), F32), pltpu.VMEM((n_heads, tq, LANES), F32)],
        compiler_params=_cparams(("arbitrary", "arbitrary")),
        name=name,
    )(q, k, v)


def _pair_attn_kernel(q_ref, k_ref, v_ref, o_ref):
    tq = q_ref.shape[1]
    lo = _low_half((tq, LANES))
    for p in range(v_ref.shape[2] // LANES):
        sl = slice(LANES * p, LANES * (p + 1))
        qp, kp, vp = q_ref[0, :, sl], k_ref[0, :, sl], v_ref[0, :, sl]
        halves = []
        for half in range(2):
            qm = jnp.where(lo, qp, 0) if half == 0 else jnp.where(lo, 0, qp)
            s = _dot_nt(qm, kp)
            e = jnp.exp(s - jnp.max(s, axis=-1, keepdims=True))
            halves.append(_dot(e.astype(BF16), vp) / jnp.sum(e, axis=-1, keepdims=True))
        o_ref[0, :, sl] = jnp.where(lo, halves[0], halves[1]).astype(BF16)


def _pair_attention(q, k, v, *, tq, name):
    b, lq, w = q.shape
    lk = k.shape[1]
    return pl.pallas_call(
        _pair_attn_kernel,
        grid=(b, lq // tq),
        in_specs=[pl.BlockSpec((1, tq, w), lambda bi, i: (bi, i, 0)),
                  pl.BlockSpec((1, lk, w), lambda bi, i: (bi, 0, 0)),
                  pl.BlockSpec((1, lk, w), lambda bi, i: (bi, 0, 0))],
        out_specs=pl.BlockSpec((1, tq, w), lambda bi, i: (bi, i, 0)),
        out_shape=jax.ShapeDtypeStruct((b, lq, w), BF16),
        compiler_params=_cparams(("arbitrary", "arbitrary")),
        name=name,
    )(q, k, v)


def _gates(logits, bias):
    scores = _sigmoid(logits)
    sel = scores + bias
    lane = lax.broadcasted_iota(jnp.int32, logits.shape, 1)
    e4 = lane & (EXPERTS_PER_GROUP - 1)
    grp = (lane >> 2) & (N_GROUPS - 1)
    one, zero = jnp.float32(1.0), jnp.float32(0.0)

    rank = jnp.zeros(logits.shape, F32)
    for d in range(1, EXPERTS_PER_GROUP):
        other = _win_roll(sel, d, EXPERTS_PER_GROUP)
        other_first = ((e4 + d) & (EXPERTS_PER_GROUP - 1)) < e4
        ahead = jnp.logical_or(other > sel, jnp.logical_and(other == sel, other_first))
        rank = rank + jnp.where(ahead, one, zero)
    top2 = rank < float(TOP_K)

    def group_sum(a):
        tot = a
        for d in range(1, EXPERTS_PER_GROUP):
            tot = tot + _win_roll(a, d, EXPERTS_PER_GROUP)
        return tot

    gscore = group_sum(jnp.where(top2, sel, zero))
    beaten = jnp.zeros(logits.shape, F32)
    for d in range(1, N_GROUPS):
        other = _win_roll(gscore, d * EXPERTS_PER_GROUP, N_EXPERTS)
        other_first = ((grp + d) & (N_GROUPS - 1)) < grp
        ahead = jnp.logical_or(other > gscore, jnp.logical_and(other == gscore, other_first))
        beaten = beaten + jnp.where(ahead, one, zero)
    chosen = jnp.logical_and(jnp.logical_and(beaten < 0.5, top2), lane < N_EXPERTS)
    w = jnp.where(chosen, scores, zero)
    den = group_sum(w)
    return jnp.where(chosen, w / den, zero)


def _outproj_kernel(ret_ref, na_ref, mla_ref, h_ref, w_ref, g1_ref, sh_ref, sc_ref, n2_ref,
                    rw_ref, rb_ref, hn_ref, u_ref, gate_ref):
    mix = (_dot(ret_ref[0], w_ref[0:256, :]) + _dot(na_ref[0], w_ref[256:768, :])
           + _dot(mla_ref[0], w_ref[768:1024, :]))
    hn = h_ref[0] + g1_ref[0] * mix
    hn_ref[0] = hn
    u = _rms(hn, n2_ref[...]) * (1.0 + sc_ref[0]) + sh_ref[0]
    u_hi = u.astype(BF16)
    u_ref[0] = u_hi
    tm = u.shape[0]
    u_lo = (u - u_hi.astype(F32)).astype(BF16)
    both = _dot(jnp.concatenate([u_hi, u_lo], axis=0), rw_ref[...])
    logits = both[:tm, :LANES] + both[:tm, LANES:] + both[tm:, :LANES]
    gate_ref[0] = _gates(logits, rb_ref[...])


def _out_projection(ret_o, na_o, mla_o, h, w_out, g1, sh2, sc2, n2, rw, rb, tm):
    b, l, d = h.shape
    per_batch = g1.shape[0] > 1
    mod_map = (lambda i, j: (i, 0, 0)) if per_batch else (lambda i, j: (0, 0, 0))
    const2 = lambda i, j: (0, 0)
    tok = lambda w: pl.BlockSpec((1, tm, w), lambda i, j: (i, j, 0))
    mod = pl.BlockSpec((1, 1, d), mod_map)
    return pl.pallas_call(
        _outproj_kernel,
        grid=(b, l // tm),
        in_specs=[tok(256), tok(512), tok(256), tok(d),
                  pl.BlockSpec(w_out.shape, const2), mod, mod, mod,
                  pl.BlockSpec((1, d), const2),
                  pl.BlockSpec(rw.shape, const2), pl.BlockSpec(rb.shape, const2)],
        out_specs=[tok(d), tok(d), tok(LANES)],
        out_shape=[jax.ShapeDtypeStruct((b, l, d), F32),
                   jax.ShapeDtypeStruct((b, l, d), BF16),
                   jax.ShapeDtypeStruct((b, l, LANES), F32)],
        compiler_params=_cparams(("arbitrary", "arbitrary")),
        name="out_projection",
    )(ret_o, na_o, mla_o, h, w_out, g1, sh2, sc2, n2, rw, rb)


def _moe_kernel(u_ref, gate_ref, h_ref, w13_ref, w2_ref, g2_ref, fg_ref, o_ref, *, final_norm):
    u = u_ref[...]
    gates = gate_ref[...]
    lane = lax.broadcasted_iota(jnp.int32, gates.shape, 1)
    y = None
    for e in range(N_EXPERTS):
        a = _dot(u, w13_ref[e])
        a1, a3 = a[:, :EXPERT_FF], a[:, EXPERT_FF:]
        gcol = jnp.sum(jnp.where(lane == e, gates, 0.0), axis=-1, keepdims=True)
        hdn = ((a1 * _sigmoid(a1)) * a3 * gcol).astype(BF16)
        ye = _dot(hdn, w2_ref[e])
        y = ye if y is None else y + ye
    hn = h_ref[...] + g2_ref[0] * y
    if final_norm:
        hn = _rms(hn, fg_ref[...])
    o_ref[...] = hn


def _moe(u, gates, h, w13, w2, g2, fg, tokens_per_mod, tm, final_norm):
    t, d = h.shape
    per_batch = g2.shape[0] > 1
    tiles_per_mod = tokens_per_mod // tm
    mod_map = (lambda i: (i // tiles_per_mod, 0, 0)) if per_batch else (lambda i: (0, 0, 0))
    tok = lambda w: pl.BlockSpec((tm, w), lambda i: (i, 0))
    resident = lambda a: pl.BlockSpec(a.shape, lambda i: (0, 0, 0), pipeline_mode=pl.Buffered(1))
    return pl.pallas_call(
        functools.partial(_moe_kernel, final_norm=final_norm),
        grid=(t // tm,),
        in_specs=[tok(d), tok(LANES), tok(d), resident(w13), resident(w2),
                  pl.BlockSpec((1, 1, d), mod_map),
                  pl.BlockSpec((1, d), lambda i: (0, 0))],
        out_specs=tok(d),
        out_shape=jax.ShapeDtypeStruct((t, d), F32),
        compiler_params=_cparams(("arbitrary",)),
        name="mixture_of_experts",
    )(u, gates, h, w13, w2, g2, fg)


def _rope_tables(seq, positional):
    if not positional:
        ones = jnp.ones((seq, LANES), F32)
        zeros = jnp.zeros((seq, LANES), F32)
        return ones, zeros, ones, zeros
    t = jnp.arange(seq)
    pos_r = (t // GRID_W).astype(F32)[:, None]
    pos_c = (t % GRID_W).astype(F32)[:, None]

    def axial(width):
        h = width // 2
        half = h // 2
        inv = ROPE_BASE ** (-jnp.arange(half, dtype=F32) / half)
        inv2 = jnp.concatenate([inv, inv])[None, :]
        ang = jnp.concatenate([pos_r * inv2, pos_c * inv2], axis=1)
        sign = jnp.tile(jnp.concatenate([-jnp.ones(half), jnp.ones(half)]), 2)[None, :]
        return jnp.cos(ang), jnp.sin(ang) * sign

    rc, rs = axial(RET_DK)
    ra = jnp.concatenate([rc, rc], axis=1)
    rb = jnp.concatenate([rs, rs], axis=1)
    mc, ms = axial(MLA_ROPE)
    ma = jnp.concatenate([jnp.ones((seq, MLA_NOPE), F32), mc, jnp.zeros((seq, 32), F32)], axis=1)
    mb = jnp.concatenate([jnp.zeros((seq, MLA_NOPE), F32), ms, jnp.zeros((seq, 32), F32)], axis=1)
    return ra, rb, ma, mb


def _pad_heads(w, heads, width):
    r = w.shape[0]
    w = w.reshape(r, heads, width)
    return jnp.pad(w, ((0, 0), (0, 0), (0, LANES - width))).reshape(r, heads * LANES)


def _layer_weights(w_in, w_uq, w_ukv):
    d = w_in.shape[0]
    kpe = w_in[:, _C_KPE:_C_KPE + MLA_ROPE]
    blk = jnp.concatenate([jnp.zeros((d, MLA_NOPE), w_in.dtype), kpe,
                           jnp.zeros((d, LANES - MLA_NOPE - MLA_ROPE), w_in.dtype)], axis=1)
    w_ext = jnp.concatenate([w_in[:, :_C_KPE]] + [blk] * MLA_HEADS, axis=1).astype(BF16)
    wuq = _pad_heads(w_uq, MLA_HEADS, MLA_NOPE + MLA_ROPE).astype(BF16)
    ukv = w_ukv.reshape(w_ukv.shape[0], MLA_HEADS, MLA_NOPE + MLA_VD)
    wuk = _pad_heads(ukv[:, :, :MLA_NOPE].reshape(w_ukv.shape[0], -1), MLA_HEADS, MLA_NOPE).astype(BF16)
    wuv = _pad_heads(ukv[:, :, MLA_NOPE:].reshape(w_ukv.shape[0], -1), MLA_HEADS, MLA_VD).astype(BF16)
    return w_ext, wuq, wuk, wuv


def _latent_key_tile(n_keys):
    for tk in range(768, 0, -LANES):
        if n_keys % tk == 0 and (n_keys // tk) % 2 == 1:
            return tk
    raise ValueError(f"no odd tiling of {n_keys} keys")


def kernel(x, c, ctx, c_ctx, w_mod, b_mod, norm1_g, norm2_g, w_in, ret_decay_f, ret_decay_b, na_rpb,
           mla_q_norm, mla_kv_norm, w_uq, w_ukv, w_out, router_w, router_b, w1, w3, w2, final_norm_g):
    b, s, d = x.shape
    lc = ctx.shape[1]
    depth = w_mod.shape[0]
    rows = s // GRID_W
    tm_lat = 256
    tm_ctx = min(256, lc)

    n_mod = -(-(b + 1) // 8) * 8
    cv = jnp.concatenate([c, c_ctx[None, :], jnp.zeros((n_mod - b - 1, d), F32)], axis=0)
    mods = _modulation(cv, w_mod, b_mod).reshape(depth, n_mod, 6, d)

    tabs_lat = _rope_tables(s, True)
    tabs_ctx = _rope_tables(lc, False)
    na_plan = _na_bias_plan(rows)
    rw_f32 = jnp.pad(router_w.astype(F32), ((0, 0), (0, LANES - N_EXPERTS)))
    rw_hi = rw_f32.astype(BF16)
    rw = jnp.concatenate([rw_hi, (rw_f32 - rw_hi.astype(F32)).astype(BF16)], axis=1)
    rb = jnp.pad(router_b.astype(F32), (0, LANES - N_EXPERTS)).reshape(1, LANES)
    zero_state = jnp.zeros((2, b, 2 * LANES, LANES), F32)

    h, hc = x, ctx
    for l in range(depth):
        last = l == depth - 1
        m_lat = [mods[l, :b, i][:, None, :] for i in range(6)]
        m_ctx = [mods[l, b:b + 1, i][:, None, :] for i in range(6)]
        w_ext, wuq, wuk, wuv = _layer_weights(w_in[l], w_uq[l], w_ukv[l])
        n1 = norm1_g[l].reshape(1, d)
        n2 = norm2_g[l].reshape(1, d)
        qn = mla_q_norm[l].reshape(1, -1)
        kvn = mla_kv_norm[l].reshape(1, -1)
        wo = w_out[l].astype(BF16)
        w13 = jnp.concatenate([w1[l], w3[l]], axis=-1).astype(BF16)
        w2b = w2[l].astype(BF16)
        fg = final_norm_g.reshape(1, d)

        ret_c, na_c, mq_c, mk_c, mv_c = _in_projection(
            hc, m_ctx[0], m_ctx[1], n1, w_ext, qn, kvn, wuq, wuk, wuv, tabs_ctx, tm_ctx)
        ret_l, na_l, mq_l, mk_l, mv_l = _in_projection(
            h, m_lat[0], m_lat[1], n1, w_ext, qn, kvn, wuq, wuk, wuv, tabs_lat, tm_lat)

        rtabs = _retention_tables(ret_decay_f[l], ret_decay_b[l])
        ro_c, st = _retention(ret_c, rtabs, zero_state)
        ro_l, _ = _retention(ret_l, rtabs, st)

        na_o = _na_attention(na_l, na_c, _na_bias(na_rpb[l], na_plan))
        mla_o = _latent_attention(mq_l, jnp.concatenate([mk_l, mk_c], axis=1),
                                  jnp.concatenate([mv_l, mv_c], axis=1),
                                  tq=256, tk=_latent_key_tile(s + lc), name="latent_attention")

        hn, u2, gates = _out_projection(ro_l, na_o, mla_o, h, wo, m_lat[2], m_lat[3], m_lat[4],
                                        n2, rw, rb, tm_lat)
        h = _moe(u2.reshape(b * s, d), gates.reshape(b * s, LANES), hn.reshape(b * s, d),
                 w13, w2b, m_lat[5], fg, s, 512, last).reshape(b, s, d)

        if not last:
            nac_o = _pair_attention(na_c[:, :, 0:512], na_c[:, :, 512:1024], na_c[:, :, 1024:1536],
                                    tq=tm_ctx, name="context_na_attention")
            mlac_o = _latent_attention(mq_c, mk_c, mv_c, tq=tm_ctx, tk=lc,
                                       name="context_latent_attention")
            hcn, uc2, gates_c = _out_projection(ro_c, nac_o, mlac_o, hc, wo, m_ctx[2], m_ctx[3],
                                                m_ctx[4], n2, rw, rb, tm_ctx)
            hc = _moe(uc2.reshape(b * lc, d), gates_c.reshape(b * lc, LANES), hcn.reshape(b * lc, d),
                      w13, w2b, m_ctx[5], fg, lc, min(512, lc), False).reshape(b, lc, d)
    return h
```

```python
import functools

import numpy as np
import jax
import jax.numpy as jnp
from jax import lax
from jax.experimental import pallas as pl
from jax.experimental.pallas import tpu as pltpu

F32 = jnp.float32
BF16 = jnp.bfloat16

GRID_W = 64
EPS = 1e-6
ROPE_BASE = 10000.0
RET_HEADS, RET_DK, RET_DV, RET_CHUNK = 4, 64, 64, 128
NA_HEADS, NA_HD, NA_KH, NA_KW, NA_QBLOCK = 8, 64, 8, 16, 128
MLA_HEADS, MLA_NOPE, MLA_ROPE, MLA_VD = 4, 64, 32, 64
MLA_Q_RANK, MLA_KV_RANK = 384, 256
N_EXPERTS, N_GROUPS, TOP_K, EXPERT_FF = 16, 4, 2, 256
EXPERTS_PER_GROUP = N_EXPERTS // N_GROUPS

LANES = 128
HEAD = 64
NA_WIN_BLOCKS = 5
RET_GROUP = 4
NEG_BIG = -1e30
VMEM_LIMIT = 48 * 1024 * 1024

_C_RET, _C_NA, _C_CQ, _C_CKV, _C_KPE = 0, 1024, 2560, 2944, 3200
_W_EXT = _C_KPE + LANES
MLA_Q_SCALE = (MLA_NOPE + MLA_ROPE) ** -0.5 * float(np.log2(np.e))


def _cparams(sem):
    return pltpu.CompilerParams(dimension_semantics=sem, vmem_limit_bytes=VMEM_LIMIT)


def _dot(a, b):
    return jnp.dot(a, b, preferred_element_type=F32)


def _dot_nt(a, b):
    return lax.dot_general(a, b, (((1,), (1,)), ((), ())), preferred_element_type=F32)


def _sigmoid(x):
    return 1.0 / (1.0 + jnp.exp(-x))


def _win_roll(x, s, w):
    n = x.shape[-1]
    ax = x.ndim - 1
    lane = lax.broadcasted_iota(jnp.int32, x.shape, ax)
    fwd = pltpu.roll(x, n - s, ax)
    bwd = pltpu.roll(x, w - s, ax)
    return jnp.where((lane & (w - 1)) + s < w, fwd, bwd)


def _low_half(shape):
    lane = lax.broadcasted_iota(jnp.int32, shape, len(shape) - 1)
    return (lane & (LANES - 1)) < HEAD


def _mod_kernel(cv_ref, w_ref, b_ref, o_ref):
    cv = cv_ref[...]
    a = cv * _sigmoid(cv)
    o_ref[0] = jnp.dot(a, w_ref[0], preferred_element_type=F32,
                       precision=lax.Precision.HIGHEST) + b_ref[0]


def _modulation(cv, w_mod, b_mod):
    depth, d, n6 = w_mod.shape
    rows = cv.shape[0]
    tn = 1536
    return pl.pallas_call(
        _mod_kernel,
        grid=(depth, n6 // tn),
        in_specs=[pl.BlockSpec((rows, d), lambda l, j: (0, 0)),
                  pl.BlockSpec((1, d, tn), lambda l, j: (l, 0, j)),
                  pl.BlockSpec((1, 1, tn), lambda l, j: (l, 0, j))],
        out_specs=pl.BlockSpec((1, rows, tn), lambda l, j: (l, 0, j)),
        out_shape=jax.ShapeDtypeStruct((depth, rows, n6), F32),
        compiler_params=_cparams(("arbitrary", "arbitrary")),
        name="modulation",
    )(cv, w_mod, b_mod.reshape(depth, 1, n6))


def _rms(x, g):
    return x * lax.rsqrt(jnp.mean(x * x, axis=-1, keepdims=True) + EPS) * g


def _inproj_kernel(h_ref, sh_ref, sc_ref, g_ref, w_ref, qn_ref, kvn_ref, wuq_ref, wuk_ref, wuv_ref,
                   ra_ref, rb_ref, ma_ref, mb_ref,
                   ret_ref, na_ref, q_ref, k_ref, v_ref):
    x = h_ref[0]
    u = _rms(x, g_ref[...]) * (1.0 + sc_ref[0]) + sh_ref[0]
    ub = u.astype(BF16)

    r = _dot(ub, w_ref[:, _C_RET:_C_NA])
    qk = r[:, :512]
    ra = jnp.concatenate([ra_ref[...]] * 4, axis=1)
    rb = jnp.concatenate([rb_ref[...]] * 4, axis=1)
    qk = qk * ra + _win_roll(qk, 16, 32) * rb
    ret_ref[0, :, 0:256] = qk[:, :256] * (RET_DK ** -0.5)
    ret_ref[0, :, 256:512] = qk[:, 256:]
    ret_ref[0, :, 512:1024] = r[:, 512:]

    n = _dot(ub, w_ref[:, _C_NA:_C_CQ])
    na_ref[0, :, 0:512] = (n[:, :512] * (NA_HD ** -0.5)).astype(BF16)
    na_ref[0, :, 512:1536] = n[:, 512:].astype(BF16)

    ma = jnp.concatenate([ma_ref[...]] * 4, axis=1)
    mb = jnp.concatenate([mb_ref[...]] * 4, axis=1)
    cq = _rms(_dot(ub, w_ref[:, _C_CQ:_C_CKV]), qn_ref[...]).astype(BF16)
    q = _dot(cq, wuq_ref[...])
    q = q * ma + _win_roll(q, 8, 16) * mb
    q_ref[0] = (q * MLA_Q_SCALE).astype(BF16)
    ckv = _rms(_dot(ub, w_ref[:, _C_CKV:_C_KPE]), kvn_ref[...]).astype(BF16)
    kp = _dot(ub, w_ref[:, _C_KPE:_W_EXT])
    kp = kp * ma_ref[...] + _win_roll(kp, 8, 16) * mb_ref[...]
    kp = jnp.concatenate([kp] * MLA_HEADS, axis=1)
    k_ref[0] = (_dot(ckv, wuk_ref[...]) + kp).astype(BF16)
    v = _dot(ckv, wuv_ref[...])
    lane = lax.broadcasted_iota(jnp.int32, v.shape, 1)
    v_ref[0] = jnp.where((lane & (LANES - 1)) == HEAD, 1.0, v).astype(BF16)


def _in_projection(h, sh, sc, g, w_ext, qn, kvn, wuq, wuk, wuv, tabs, tm):
    b, l, d = h.shape
    per_batch = sh.shape[0] > 1
    mod_map = (lambda i, j: (i, 0, 0)) if per_batch else (lambda i, j: (0, 0, 0))
    const2 = lambda i, j: (0, 0)
    tab_spec = pl.BlockSpec((tm, LANES), lambda i, j: (j, 0))
    tok = lambda w: pl.BlockSpec((1, tm, w), lambda i, j: (i, j, 0))
    return pl.pallas_call(
        _inproj_kernel,
        grid=(b, l // tm),
        in_specs=[tok(d),
                  pl.BlockSpec((1, 1, d), mod_map), pl.BlockSpec((1, 1, d), mod_map),
                  pl.BlockSpec((1, d), const2),
                  pl.BlockSpec(w_ext.shape, const2),
                  pl.BlockSpec(qn.shape, const2), pl.BlockSpec(kvn.shape, const2),
                  pl.BlockSpec(wuq.shape, const2), pl.BlockSpec(wuk.shape, const2),
                  pl.BlockSpec(wuv.shape, const2),
                  tab_spec, tab_spec, tab_spec, tab_spec],
        out_specs=[tok(1024), tok(1536), tok(512), tok(512), tok(512)],
        out_shape=[jax.ShapeDtypeStruct((b, l, 1024), F32),
                   jax.ShapeDtypeStruct((b, l, 1536), BF16),
                   jax.ShapeDtypeStruct((b, l, 512), BF16),
                   jax.ShapeDtypeStruct((b, l, 512), BF16),
                   jax.ShapeDtypeStruct((b, l, 512), BF16)],
        compiler_params=_cparams(("arbitrary", "arbitrary")),
        name="in_projection",
    )(h, sh, sc, g, w_ext, qn, kvn, wuq, wuk, wuv, *tabs)


def _retention_kernel(*refs, backward, group):
    if backward:
        (q_ref, k_ref, v_ref, g_ref, of_ref, dm_ref, qw_ref, kw_ref, gc_ref, s0_ref,
         o_ref, sfin_ref, state) = refs
    else:
        q_ref, k_ref, v_ref, dm_ref, qw_ref, kw_ref, gc_ref, s0_ref, o_ref, sfin_ref, state = refs
    step = pl.program_id(1)
    c = RET_CHUNK
    n_pairs = RET_HEADS // 2

    @pl.when(step == 0)
    def _():
        state[...] = s0_ref[0]

    lo = _low_half((c, LANES))
    row_lo = lax.broadcasted_iota(jnp.int32, (LANES, LANES), 0) < HEAD
    same_head = row_lo == _low_half((LANES, LANES))
    qw, kw = qw_ref[...], kw_ref[...]

    intra, kv, qd = {}, {}, {}
    for ci in range(group):
        rows = slice(c * ci, c * (ci + 1))
        for p in range(n_pairs):
            sl = slice(LANES * p, LANES * (p + 1))
            qp, kp = q_ref[0, rows, sl], k_ref[0, rows, sl]
            kb, vb = kp.astype(BF16), v_ref[0, rows, sl].astype(BF16)
            qa = jnp.where(lo, qp, 0.0).astype(BF16)
            qb = jnp.where(lo, 0.0, qp).astype(BF16)
            sa = (_dot_nt(qa, kb) * dm_ref[2 * p]).astype(BF16)
            sb = (_dot_nt(qb, kb) * dm_ref[2 * p + 1]).astype(BF16)
            intra[ci, p] = jnp.where(lo, _dot(sa, vb), _dot(sb, vb))
            kv[ci, p] = jnp.where(same_head, _dot((kp * kw[:, sl]).T.astype(BF16), vb), 0.0)
            qd[ci, p] = (qp * qw[:, sl]).astype(BF16)

    order = range(group - 1, -1, -1) if backward else range(group)
    out = {}
    for p in range(n_pairs):
        sl = slice(LANES * p, LANES * (p + 1))
        sp = state[sl, :]
        gc = gc_ref[sl, :]
        for ci in order:
            out[ci, p] = intra[ci, p] + _dot(qd[ci, p], sp.astype(BF16))
            sp = gc * sp + kv[ci, p]
        state[sl, :] = sp

    for ci in range(group):
        rows = slice(c * ci, c * (ci + 1))
        o = jnp.concatenate([out[ci, p] for p in range(n_pairs)], axis=1)
        if not backward:
            o_ref[0, rows, :] = o
            continue
        tot = of_ref[0, rows, :] + o
        g = g_ref[0, rows, :]
        normed = []
        for p in range(n_pairs):
            t = tot[:, LANES * p:LANES * (p + 1)]

            def head_mean(a):
                s_lo = jnp.sum(jnp.where(lo, a, 0.0), axis=-1, keepdims=True)
                s_hi = jnp.sum(jnp.where(lo, 0.0, a), axis=-1, keepdims=True)
                return jnp.where(lo, s_lo, s_hi) * (1.0 / HEAD)

            dlt = t - head_mean(t)
            normed.append(dlt * lax.rsqrt(head_mean(dlt * dlt) + EPS))
        on = jnp.concatenate(normed, axis=1)
        o_ref[0, rows, :] = (on * (g * _sigmoid(g))).astype(BF16)

    @pl.when(step == pl.num_programs(1) - 1)
    def _():
        sfin_ref[0] = state[...]


def _retention_sweep(ret, o_fwd, tabs, s0, *, backward):
    b, l, _ = ret.shape
    c = RET_CHUNK
    group = min(RET_GROUP, l // c)
    n = l // (c * group)
    gidx = (lambda s: n - 1 - s) if backward else (lambda s: s)
    tok = lambda col: pl.BlockSpec((1, c * group, 256), lambda i, s: (i, gidx(s), col))
    const = lambda a: pl.BlockSpec(a.shape, lambda i, s: (0,) * a.ndim)
    state_spec = pl.BlockSpec((1, 2 * LANES, LANES), lambda i, s: (i, 0, 0))
    dm, qw, kw, gc = tabs
    if backward:
        in_specs = [tok(0), tok(1), tok(2), tok(3), tok(0)]
        args = [ret, ret, ret, ret, o_fwd]
    else:
        in_specs = [tok(0), tok(1), tok(2)]
        args = [ret, ret, ret]
    return pl.pallas_call(
        functools.partial(_retention_kernel, backward=backward, group=group),
        grid=(b, n),
        in_specs=in_specs + [const(dm), const(qw), const(kw), const(gc), state_spec],
        out_specs=[tok(0), state_spec],
        out_shape=[jax.ShapeDtypeStruct((b, l, 256), BF16 if backward else F32),
                   jax.ShapeDtypeStruct((b, 2 * LANES, LANES), F32)],
        scratch_shapes=[pltpu.VMEM((2 * LANES, LANES), F32)],
        compiler_params=_cparams(("arbitrary", "arbitrary")),
        name="retention_backward" if backward else "retention_forward",
    )(*args, dm, qw, kw, gc, s0)


def _retention(ret, tabs, s0):
    fwd_tabs = tuple(t[0] for t in tabs)
    bwd_tabs = tuple(t[1] for t in tabs)
    o_f, s_f = _retention_sweep(ret, None, fwd_tabs, s0[0], backward=False)
    o, s_b = _retention_sweep(ret, o_f, bwd_tabs, s0[1], backward=True)
    return o, (s_f, s_b)


def _retention_tables(decay_f, decay_b):
    c = RET_CHUNK
    pos = jnp.arange(c, dtype=F32)
    diff = pos[:, None] - pos[None, :]
    lg = jnp.stack([jax.nn.log_sigmoid(decay_f.astype(F32)), jax.nn.log_sigmoid(decay_b.astype(F32))])
    lgh = lg[:, :, None, None]
    dm_f = jnp.where(diff >= 0, jnp.exp(lgh[0] * jnp.maximum(diff, 0.0)), 0.0)
    dm_b = jnp.where(diff < 0, jnp.exp(lgh[1] * jnp.maximum(-diff, 0.0)), 0.0)
    dm = jnp.stack([dm_f, dm_b])
    lanes = lambda t: jnp.repeat(t, RET_DK, axis=-1)
    lgc = lg[:, None, :]
    qw = lanes(jnp.exp(lgc * jnp.stack([pos + 1.0, c - pos])[:, :, None]))
    kw = lanes(jnp.exp(lgc * jnp.stack([c - 1.0 - pos, pos])[:, :, None]))
    gch = jnp.exp(lg * c)
    gc = jnp.broadcast_to(jnp.repeat(gch, RET_DK, axis=-1)[:, :, None], (2, RET_HEADS * RET_DK, LANES))
    return dm, qw, kw, gc


def _na_kernel(*refs, n_ctx):
    q_ref = refs[0]
    k_refs = refs[1:1 + NA_WIN_BLOCKS]
    v_refs = refs[1 + NA_WIN_BLOCKS:1 + 2 * NA_WIN_BLOCKS]
    kc_ref, vc_ref, bias_ref, o_ref, s_scr, p_scr, den_scr = refs[1 + 2 * NA_WIN_BLOCKS:]
    tq = NA_QBLOCK
    n_loc = NA_WIN_BLOCKS * tq
    n_lane_blocks = (n_loc + n_ctx) // LANES
    lo = _low_half((tq, LANES))
    for h in range(NA_HEADS):
        sl = slice(LANES * (h // 2), LANES * (h // 2 + 1))
        qp = q_ref[0, :, sl]
        qm = jnp.where(lo, qp, 0) if h % 2 == 0 else jnp.where(lo, 0, qp)
        for j in range(NA_WIN_BLOCKS):
            cs = slice(tq * j, tq * (j + 1))
            s_scr[h, :, cs] = _dot_nt(qm, k_refs[j][0, :, sl]) + bias_ref[0, h, :, cs]
        s_scr[h, :, n_loc:] = _dot_nt(qm, kc_ref[0, :, sl])
    for h in range(NA_HEADS):
        part = s_scr[h, :, 0:LANES]
        for c in range(1, n_lane_blocks):
            part = jnp.maximum(part, s_scr[h, :, LANES * c:LANES * (c + 1)])
        m = jnp.max(part, axis=-1, keepdims=True)
        tot = None
        for c in range(n_lane_blocks):
            cs = slice(LANES * c, LANES * (c + 1))
            e = jnp.exp(s_scr[h, :, cs] - m)
            tot = e if tot is None else tot + e
            p_scr[h, :, cs] = e.astype(BF16)
        den_scr[h] = jnp.broadcast_to(jnp.sum(tot, axis=-1, keepdims=True), (tq, LANES))
    for p in range(NA_HEADS // 2):
        sl = slice(LANES * p, LANES * (p + 1))
        halves = []
        for h in (2 * p, 2 * p + 1):
            o = _dot(p_scr[h, :, n_loc:], vc_ref[0, :, sl])
            for j in range(NA_WIN_BLOCKS):
                o = o + _dot(p_scr[h, :, tq * j:tq * (j + 1)], v_refs[j][0, :, sl])
            halves.append(o / den_scr[h])
        o_ref[0, :, sl] = jnp.where(lo, halves[0], halves[1]).astype(BF16)


def _na_variant(i, nblk):
    return jnp.where(i < 2, i, jnp.where(i <= nblk - 3, 2, i - (nblk - 2) + 3))


def _na_attention(na, na_ctx, bias):
    b, l, _ = na.shape
    n_ctx = na_ctx.shape[1]
    nblk = l // NA_QBLOCK
    tq = NA_QBLOCK

    def win(col, j):
        return pl.BlockSpec((1, tq, 512),
                            lambda bi, i: (bi, jnp.clip(i - 2, 0, nblk - NA_WIN_BLOCKS) + j, col))

    in_specs = ([pl.BlockSpec((1, tq, 512), lambda bi, i: (bi, i, 0))]
                + [win(1, j) for j in range(NA_WIN_BLOCKS)]
                + [win(2, j) for j in range(NA_WIN_BLOCKS)]
                + [pl.BlockSpec((1, n_ctx, 512), lambda bi, i: (bi, 0, 1)),
                   pl.BlockSpec((1, n_ctx, 512), lambda bi, i: (bi, 0, 2)),
                   pl.BlockSpec((1, NA_HEADS, tq, NA_WIN_BLOCKS * tq),
                                lambda bi, i: (_na_variant(i, nblk), 0, 0, 0))])
    args = [na] * (1 + 2 * NA_WIN_BLOCKS) + [na_ctx, na_ctx, bias]
    return pl.pallas_call(
        functools.partial(_na_kernel, n_ctx=n_ctx),
        grid=(b, nblk),
        in_specs=in_specs,
        out_specs=pl.BlockSpec((1, tq, 512), lambda bi, i: (bi, i, 0)),
        out_shape=jax.ShapeDtypeStruct((b, l, 512), BF16),
        scratch_shapes=[pltpu.VMEM((NA_HEADS, tq, NA_WIN_BLOCKS * tq + n_ctx), F32),
                        pltpu.VMEM((NA_HEADS, tq, NA_WIN_BLOCKS * tq + n_ctx), BF16),
                        pltpu.VMEM((NA_HEADS, tq, LANES), F32)],
        compiler_params=_cparams(("arbitrary", "arbitrary")),
        name="neighbourhood_attention",
    )(*args)


def _na_bias_plan(rows):
    nblk = rows * GRID_W // NA_QBLOCK
    kh = min(NA_KH, rows)
    q_rows = NA_QBLOCK // GRID_W
    k_rows = NA_WIN_BLOCKS * q_rows
    plan = []
    for i in (0, 1, 2, nblk - 2, nblk - 1):
        kb = min(max(i - 2, 0), nblk - NA_WIN_BLOCKS)
        per_q = []
        for rr in range(q_rows):
            r = q_rows * i + rr
            r0 = min(max(r - kh // 2, 0), rows - kh)
            per_q.append([(q_rows * kb + kj - r + NA_KH - 1) if r0 <= q_rows * kb + kj < r0 + kh else None
                          for kj in range(k_rows)])
        plan.append(per_q)
    return plan


def _na_bias(rpb, plan):
    c = np.arange(GRID_W)
    c0 = np.clip(c - NA_KW // 2, 0, GRID_W - NA_KW)
    kc = np.arange(GRID_W)
    ok = (kc[None, :] >= c0[:, None]) & (kc[None, :] < c0[:, None] + NA_KW)
    idx = np.where(ok, kc[None, :] - c[:, None] + NA_KW - 1, 0).astype(np.int32)
    cols = jnp.take(rpb.astype(F32), jnp.asarray(idx.reshape(-1)), axis=2)
    cols = cols.reshape(NA_HEADS, 2 * NA_KH - 1, GRID_W, GRID_W)
    cols = jnp.where(jnp.asarray(ok)[None, None], cols, NEG_BIG)
    outside = jnp.full((NA_HEADS, GRID_W, GRID_W), NEG_BIG, F32)
    variants = []
    for per_q in plan:
        q_parts = [jnp.concatenate([outside if dr is None else cols[:, dr] for dr in per_k], axis=-1)
                   for per_k in per_q]
        variants.append(jnp.concatenate(q_parts, axis=1))
    return jnp.stack(variants)


def _latent_attn_kernel(q_ref, k_ref, v_ref, o_ref, s_scr, p_scr, al_scr, m_scr, acc_scr, *, tk):
    n_k = k_ref.shape[1] // tk
    n_heads = v_ref.shape[2] // LANES
    n_lane_blocks = tk // LANES
    assert n_k % 2 == 1, "the two-slot hand-over below assumes an odd number of key tiles"

    def key_rows(j):
        return pl.ds(pl.multiple_of(j * tk, tk), tk)

    def scores(j, slot):
        for h in range(n_heads):
            sl = slice(LANES * h, LANES * (h + 1))
            s_scr[slot, h] = _dot_nt(q_ref[0, :, sl], k_ref[0, key_rows(j), sl])

    def softmax(slot):
        for h in range(n_heads):
            part = s_scr[slot, h, :, 0:LANES]
            for c in range(1, n_lane_blocks):
                part = jnp.maximum(part, s_scr[slot, h, :, LANES * c:LANES * (c + 1)])
            m_old = m_scr[h]
            m_new = jnp.maximum(m_old, jnp.max(part, axis=-1, keepdims=True))
            al_scr[slot, h] = jnp.exp2(m_old - m_new)
            m_scr[h] = m_new
            for c in range(n_lane_blocks):
                cs = slice(LANES * c, LANES * (c + 1))
                p_scr[slot, h, :, cs] = jnp.exp2(s_scr[slot, h, :, cs] - m_new).astype(BF16)

    def values(j, slot):
        for h in range(n_heads):
            sl = slice(LANES * h, LANES * (h + 1))
            acc_scr[h] = al_scr[slot, h] * acc_scr[h] + _dot(p_scr[slot, h], v_ref[0, key_rows(j), sl])

    m_scr[...] = jnp.full(m_scr.shape, -jnp.inf, F32)
    acc_scr[...] = jnp.zeros(acc_scr.shape, F32)
    p_scr[1] = jnp.zeros(p_scr.shape[1:], BF16)
    al_scr[1] = jnp.ones(al_scr.shape[1:], F32)
    scores(0, 0)

    def body(i, carry):
        j = 2 * i
        scores(j + 1, 1)
        softmax(0)
        values(jnp.maximum(j - 1, 0), 1)
        scores(j + 2, 0)
        softmax(1)
        values(j, 0)
        return carry

    lax.fori_loop(0, (n_k - 1) // 2, body, 0)
    softmax(0)
    if n_k > 1:
        values(n_k - 2, 1)
    values(n_k - 1, 0)
    outs = []
    for h in range(n_heads):
        a = acc_scr[h]
        outs.append(a[:, 0:HEAD] / a[:, HEAD:HEAD + 1])
    o_ref[0] = jnp.concatenate(outs, axis=1).astype(BF16)


def _latent_attention(q, k, v, *, tq, tk, name):
    b, lq, w = q.shape
    lk = k.shape[1]
    n_heads = w // LANES
    resident = lambda: pl.BlockSpec((1, lk, w), lambda bi, i: (bi, 0, 0), pipeline_mode=pl.Buffered(1))
    return pl.pallas_call(
        functools.partial(_latent_attn_kernel, tk=tk),
        grid=(b, lq // tq),
        in_specs=[pl.BlockSpec((1, tq, w), lambda bi, i: (bi, i, 0)), resident(), resident()],
        out_specs=pl.BlockSpec((1, tq, n_heads * HEAD), lambda bi, i: (bi, i, 0)),
        out_shape=jax.ShapeDtypeStruct((b, lq, n_heads * HEAD), BF16),
        scratch_shapes=[pltpu.VMEM((2, n_heads, tq, tk), F32), pltpu.VMEM((2, n_heads, tq, tk), BF16),
                        pltpu.VMEM((2, n_heads, tq, LANES), F32),
                        pltpu.VMEM((n_heads, tq, LANES), F32), pltpu.VMEM((n_heads, tq, LANES), F32)],
        compiler_params=_cparams(("arbitrary", "arbitrary")),
        name=name,
    )(q, k, v)


def _pair_attn_kernel(q_ref, k_ref, v_ref, o_ref):
    tq = q_ref.shape[1]
    lo = _low_half((tq, LANES))
    for p in range(v_ref.shape[2] // LANES):
        sl = slice(LANES * p, LANES * (p + 1))
        qp, kp, vp = q_ref[0, :, sl], k_ref[0, :, sl], v_ref[0, :, sl]
        halves = []
        for half in range(2):
            qm = jnp.where(lo, qp, 0) if half == 0 else jnp.where(lo, 0, qp)
            s = _dot_nt(qm, kp)
            e = jnp.exp(s - jnp.max(s, axis=-1, keepdims=True))
            halves.append(_dot(e.astype(BF16), vp) / jnp.sum(e, axis=-1, keepdims=True))
        o_ref[0, :, sl] = jnp.where(lo, halves[0], halves[1]).astype(BF16)


def _pair_attention(q, k, v, *, tq, name):
    b, lq, w = q.shape
    lk = k.shape[1]
    return pl.pallas_call(
        _pair_attn_kernel,
        grid=(b, lq // tq),
        in_specs=[pl.BlockSpec((1, tq, w), lambda bi, i: (bi, i, 0)),
                  pl.BlockSpec((1, lk, w), lambda bi, i: (bi, 0, 0)),
                  pl.BlockSpec((1, lk, w), lambda bi, i: (bi, 0, 0))],
        out_specs=pl.BlockSpec((1, tq, w), lambda bi, i: (bi, i, 0)),
        out_shape=jax.ShapeDtypeStruct((b, lq, w), BF16),
        compiler_params=_cparams(("arbitrary", "arbitrary")),
        name=name,
    )(q, k, v)


def _gates(logits, bias):
    scores = _sigmoid(logits)
    sel = scores + bias
    lane = lax.broadcasted_iota(jnp.int32, logits.shape, 1)
    e4 = lane & (EXPERTS_PER_GROUP - 1)
    grp = (lane >> 2) & (N_GROUPS - 1)
    one, zero = jnp.float32(1.0), jnp.float32(0.0)

    rank = jnp.zeros(logits.shape, F32)
    for d in range(1, EXPERTS_PER_GROUP):
        other = _win_roll(sel, d, EXPERTS_PER_GROUP)
        other_first = ((e4 + d) & (EXPERTS_PER_GROUP - 1)) < e4
        ahead = jnp.logical_or(other > sel, jnp.logical_and(other == sel, other_first))
        rank = rank + jnp.where(ahead, one, zero)
    top2 = rank < float(TOP_K)

    def group_sum(a):
        tot = a
        for d in range(1, EXPERTS_PER_GROUP):
            tot = tot + _win_roll(a, d, EXPERTS_PER_GROUP)
        return tot

    gscore = group_sum(jnp.where(top2, sel, zero))
    beaten = jnp.zeros(logits.shape, F32)
    for d in range(1, N_GROUPS):
        other = _win_roll(gscore, d * EXPERTS_PER_GROUP, N_EXPERTS)
        other_first = ((grp + d) & (N_GROUPS - 1)) < grp
        ahead = jnp.logical_or(other > gscore, jnp.logical_and(other == gscore, other_first))
        beaten = beaten + jnp.where(ahead, one, zero)
    chosen = jnp.logical_and(jnp.logical_and(beaten < 0.5, top2), lane < N_EXPERTS)
    w = jnp.where(chosen, scores, zero)
    den = group_sum(w)
    return jnp.where(chosen, w / den, zero)


def _outproj_kernel(ret_ref, na_ref, mla_ref, h_ref, w_ref, g1_ref, sh_ref, sc_ref, n2_ref,
                    rw_ref, rb_ref, hn_ref, u_ref, gate_ref):
    mix = (_dot(ret_ref[0], w_ref[0:256, :]) + _dot(na_ref[0], w_ref[256:768, :])
           + _dot(mla_ref[0], w_ref[768:1024, :]))
    hn = h_ref[0] + g1_ref[0] * mix
    hn_ref[0] = hn
    u = _rms(hn, n2_ref[...]) * (1.0 + sc_ref[0]) + sh_ref[0]
    u_hi = u.astype(BF16)
    u_ref[0] = u_hi
    tm = u.shape[0]
    u_lo = (u - u_hi.astype(F32)).astype(BF16)
    both = _dot(jnp.concatenate([u_hi, u_lo], axis=0), rw_ref[...])
    logits = both[:tm, :LANES] + both[:tm, LANES:] + both[tm:, :LANES]
    gate_ref[0] = _gates(logits, rb_ref[...])


def _out_projection(ret_o, na_o, mla_o, h, w_out, g1, sh2, sc2, n2, rw, rb, tm):
    b, l, d = h.shape
    per_batch = g1.shape[0] > 1
    mod_map = (lambda i, j: (i, 0, 0)) if per_batch else (lambda i, j: (0, 0, 0))
    const2 = lambda i, j: (0, 0)
    tok = lambda w: pl.BlockSpec((1, tm, w), lambda i, j: (i, j, 0))
    mod = pl.BlockSpec((1, 1, d), mod_map)
    return pl.pallas_call(
        _outproj_kernel,
        grid=(b, l // tm),
        in_specs=[tok(256), tok(512), tok(256), tok(d),
                  pl.BlockSpec(w_out.shape, const2), mod, mod, mod,
                  pl.BlockSpec((1, d), const2),
                  pl.BlockSpec(rw.shape, const2), pl.BlockSpec(rb.shape, const2)],
        out_specs=[tok(d), tok(d), tok(LANES)],
        out_shape=[jax.ShapeDtypeStruct((b, l, d), F32),
                   jax.ShapeDtypeStruct((b, l, d), BF16),
                   jax.ShapeDtypeStruct((b, l, LANES), F32)],
        compiler_params=_cparams(("arbitrary", "arbitrary")),
        name="out_projection",
    )(ret_o, na_o, mla_o, h, w_out, g1, sh2, sc2, n2, rw, rb)


def _moe_kernel(u_ref, gate_ref, h_ref, w13_ref, w2_ref, g2_ref, fg_ref, o_ref, *, final_norm):
    u = u_ref[...]
    gates = gate_ref[...]
    lane = lax.broadcasted_iota(jnp.int32, gates.shape, 1)
    y = None
    for e in range(N_EXPERTS):
        a = _dot(u, w13_ref[e])
        a1, a3 = a[:, :EXPERT_FF], a[:, EXPERT_FF:]
        gcol = jnp.sum(jnp.where(lane == e, gates, 0.0), axis=-1, keepdims=True)
        hdn = ((a1 * _sigmoid(a1)) * a3 * gcol).astype(BF16)
        ye = _dot(hdn, w2_ref[e])
        y = ye if y is None else y + ye
    hn = h_ref[...] + g2_ref[0] * y
    if final_norm:
        hn = _rms(hn, fg_ref[...])
    o_ref[...] = hn


def _moe(u, gates, h, w13, w2, g2, fg, tokens_per_mod, tm, final_norm):
    t, d = h.shape
    per_batch = g2.shape[0] > 1
    tiles_per_mod = tokens_per_mod // tm
    mod_map = (lambda i: (i // tiles_per_mod, 0, 0)) if per_batch else (lambda i: (0, 0, 0))
    tok = lambda w: pl.BlockSpec((tm, w), lambda i: (i, 0))
    resident = lambda a: pl.BlockSpec(a.shape, lambda i: (0, 0, 0), pipeline_mode=pl.Buffered(1))
    return pl.pallas_call(
        functools.partial(_moe_kernel, final_norm=final_norm),
        grid=(t // tm,),
        in_specs=[tok(d), tok(LANES), tok(d), resident(w13), resident(w2),
                  pl.BlockSpec((1, 1, d), mod_map),
                  pl.BlockSpec((1, d), lambda i: (0, 0))],
        out_specs=tok(d),
        out_shape=jax.ShapeDtypeStruct((t, d), F32),
        compiler_params=_cparams(("arbitrary",)),
        name="mixture_of_experts",
    )(u, gates, h, w13, w2, g2, fg)


def _rope_tables(seq, positional):
    if not positional:
        ones = jnp.ones((seq, LANES), F32)
        zeros = jnp.zeros((seq, LANES), F32)
        return ones, zeros, ones, zeros
    t = jnp.arange(seq)
    pos_r = (t // GRID_W).astype(F32)[:, None]
    pos_c = (t % GRID_W).astype(F32)[:, None]

    def axial(width):
        h = width // 2
        half = h // 2
        inv = ROPE_BASE ** (-jnp.arange(half, dtype=F32) / half)
        inv2 = jnp.concatenate([inv, inv])[None, :]
        ang = jnp.concatenate([pos_r * inv2, pos_c * inv2], axis=1)
        sign = jnp.tile(jnp.concatenate([-jnp.ones(half), jnp.ones(half)]), 2)[None, :]
        return jnp.cos(ang), jnp.sin(ang) * sign

    rc, rs = axial(RET_DK)
    ra = jnp.concatenate([rc, rc], axis=1)
    rb = jnp.concatenate([rs, rs], axis=1)
    mc, ms = axial(MLA_ROPE)
    ma = jnp.concatenate([jnp.ones((seq, MLA_NOPE), F32), mc, jnp.zeros((seq, 32), F32)], axis=1)
    mb = jnp.concatenate([jnp.zeros((seq, MLA_NOPE), F32), ms, jnp.zeros((seq, 32), F32)], axis=1)
    return ra, rb, ma, mb


def _pad_heads(w, heads, width):
    r = w.shape[0]
    w = w.reshape(r, heads, width)
    return jnp.pad(w, ((0, 0), (0, 0), (0, LANES - width))).reshape(r, heads * LANES)


def _layer_weights(w_in, w_uq, w_ukv):
    d = w_in.shape[0]
    kpe = w_in[:, _C_KPE:_C_KPE + MLA_ROPE]
    blk = jnp.concatenate([jnp.zeros((d, MLA_NOPE), w_in.dtype), kpe,
                           jnp.zeros((d, LANES - MLA_NOPE - MLA_ROPE), w_in.dtype)], axis=1)
    w_ext = jnp.concatenate([w_in[:, :_C_KPE], blk], axis=1).astype(BF16)
    wuq = _pad_heads(w_uq, MLA_HEADS, MLA_NOPE + MLA_ROPE).astype(BF16)
    ukv = w_ukv.reshape(w_ukv.shape[0], MLA_HEADS, MLA_NOPE + MLA_VD)
    wuk = _pad_heads(ukv[:, :, :MLA_NOPE].reshape(w_ukv.shape[0], -1), MLA_HEADS, MLA_NOPE).astype(BF16)
    wuv = _pad_heads(ukv[:, :, MLA_NOPE:].reshape(w_ukv.shape[0], -1), MLA_HEADS, MLA_VD).astype(BF16)
    return w_ext, wuq, wuk, wuv


def _latent_key_tile(n_keys):
    for tk in range(768, 0, -LANES):
        if n_keys % tk == 0 and (n_keys // tk) % 2 == 1:
            return tk
    raise ValueError(f"no odd tiling of {n_keys} keys")


def kernel(x, c, ctx, c_ctx, w_mod, b_mod, norm1_g, norm2_g, w_in, ret_decay_f, ret_decay_b, na_rpb,
           mla_q_norm, mla_kv_norm, w_uq, w_ukv, w_out, router_w, router_b, w1, w3, w2, final_norm_g):
    b, s, d = x.shape
    lc = ctx.shape[1]
    depth = w_mod.shape[0]
    rows = s // GRID_W
    tm_lat = 256
    tm_ctx = min(256, lc)

    n_mod = -(-(b + 1) // 8) * 8
    cv = jnp.concatenate([c, c_ctx[None, :], jnp.zeros((n_mod - b - 1, d), F32)], axis=0)
    mods = _modulation(cv, w_mod, b_mod).reshape(depth, n_mod, 6, d)

    tabs_lat = _rope_tables(s, True)
    tabs_ctx = _rope_tables(lc, False)
    na_plan = _na_bias_plan(rows)
    rw_f32 = jnp.pad(router_w.astype(F32), ((0, 0), (0, LANES - N_EXPERTS)))
    rw_hi = rw_f32.astype(BF16)
    rw = jnp.concatenate([rw_hi, (rw_f32 - rw_hi.astype(F32)).astype(BF16)], axis=1)
    rb = jnp.pad(router_b.astype(F32), (0, LANES - N_EXPERTS)).reshape(1, LANES)
    zero_state = (jnp.zeros((b, 2 * LANES, LANES), F32),) * 2

    h, hc = x, ctx
    for l in range(depth):
        last = l == depth - 1
        m_lat = [mods[l, :b, i][:, None, :] for i in range(6)]
        m_ctx = [mods[l, b:b + 1, i][:, None, :] for i in range(6)]
        w_ext, wuq, wuk, wuv = _layer_weights(w_in[l], w_uq[l], w_ukv[l])
        n1 = norm1_g[l].reshape(1, d)
        n2 = norm2_g[l].reshape(1, d)
        qn = mla_q_norm[l].reshape(1, -1)
        kvn = mla_kv_norm[l].reshape(1, -1)
        wo = w_out[l].astype(BF16)
        w13 = jnp.concatenate([w1[l], w3[l]], axis=-1).astype(BF16)
        w2b = w2[l].astype(BF16)
        fg = final_norm_g.reshape(1, d)

        ret_c, na_c, mq_c, mk_c, mv_c = _in_projection(
            hc, m_ctx[0], m_ctx[1], n1, w_ext, qn, kvn, wuq, wuk, wuv, tabs_ctx, tm_ctx)
        ret_l, na_l, mq_l, mk_l, mv_l = _in_projection(
            h, m_lat[0], m_lat[1], n1, w_ext, qn, kvn, wuq, wuk, wuv, tabs_lat, tm_lat)

        rtabs = _retention_tables(ret_decay_f[l], ret_decay_b[l])
        ro_c, st = _retention(ret_c, rtabs, zero_state)
        ro_l, _ = _retention(ret_l, rtabs, st)

        na_o = _na_attention(na_l, na_c, _na_bias(na_rpb[l], na_plan))
        mla_o = _latent_attention(mq_l, jnp.concatenate([mk_l, mk_c], axis=1),
                                  jnp.concatenate([mv_l, mv_c], axis=1),
                                  tq=256, tk=_latent_key_tile(s + lc), name="latent_attention")

        hn, u2, gates = _out_projection(ro_l, na_o, mla_o, h, wo, m_lat[2], m_lat[3], m_lat[4],
                                        n2, rw, rb, tm_lat)
        h = _moe(u2.reshape(b * s, d), gates.reshape(b * s, LANES), hn.reshape(b * s, d),
                 w13, w2b, m_lat[5], fg, s, 512, last).reshape(b, s, d)

        if not last:
            nac_o = _pair_attention(na_c[:, :, 0:512], na_c[:, :, 512:1024], na_c[:, :, 1024:1536],
                                    tq=tm_ctx, name="context_na_attention")
            mlac_o = _latent_attention(mq_c, mk_c, mv_c, tq=tm_ctx, tk=lc,
                                       name="context_latent_attention")
            hcn, uc2, gates_c = _out_projection(ro_c, nac_o, mlac_o, hc, wo, m_ctx[2], m_ctx[3],
                                                m_ctx[4], n2, rw, rb, tm_ctx)
            hc = _moe(uc2.reshape(b * lc, d), gates_c.reshape(b * lc, LANES), hcn.reshape(b * lc, d),
                      w13, w2b, m_ctx[5], fg, lc, min(512, lc), False).reshape(b, lc, d)
    return h
```

```python
import functools

import numpy as np
import jax
import jax.numpy as jnp
from jax import lax
from jax.experimental import pallas as pl
from jax.experimental.pallas import tpu as pltpu

F32 = jnp.float32
BF16 = jnp.bfloat16

GRID_W = 64
EPS = 1e-6
ROPE_BASE = 10000.0
RET_HEADS, RET_DK, RET_DV, RET_CHUNK = 4, 64, 64, 128
NA_HEADS, NA_HD, NA_KH, NA_KW, NA_QBLOCK = 8, 64, 8, 16, 128
MLA_HEADS, MLA_NOPE, MLA_ROPE, MLA_VD = 4, 64, 32, 64
MLA_Q_RANK, MLA_KV_RANK = 384, 256
N_EXPERTS, N_GROUPS, TOP_K, EXPERT_FF = 16, 4, 2, 256
EXPERTS_PER_GROUP = N_EXPERTS // N_GROUPS

LANES = 128
HEAD = 64
NA_WIN_BLOCKS = 5
RET_GROUP = 4
NEG_BIG = -1e30
VMEM_LIMIT = 48 * 1024 * 1024

_C_RET, _C_NA, _C_CQ, _C_CKV, _C_KPE = 0, 1024, 2560, 2944, 3200
_W_EXT = _C_KPE + LANES
MLA_Q_SCALE = (MLA_NOPE + MLA_ROPE) ** -0.5 * float(np.log2(np.e))


def _cparams(sem):
    return pltpu.CompilerParams(dimension_semantics=sem, vmem_limit_bytes=VMEM_LIMIT)


def _dot(a, b):
    return jnp.dot(a, b, preferred_element_type=F32)


def _dot_nt(a, b):
    return lax.dot_general(a, b, (((1,), (1,)), ((), ())), preferred_element_type=F32)


def _sigmoid(x):
    return 1.0 / (1.0 + jnp.exp(-x))


def _win_roll(x, s, w):
    n = x.shape[-1]
    ax = x.ndim - 1
    lane = lax.broadcasted_iota(jnp.int32, x.shape, ax)
    fwd = pltpu.roll(x, n - s, ax)
    bwd = pltpu.roll(x, w - s, ax)
    return jnp.where((lane & (w - 1)) + s < w, fwd, bwd)


def _low_half(shape):
    lane = lax.broadcasted_iota(jnp.int32, shape, len(shape) - 1)
    return (lane & (LANES - 1)) < HEAD


def _mod_kernel(cv_ref, w_ref, b_ref, o_ref):
    cv = cv_ref[...]
    a = cv * _sigmoid(cv)
    o_ref[0] = jnp.dot(a, w_ref[0], preferred_element_type=F32,
                       precision=lax.Precision.HIGHEST) + b_ref[0]


def _modulation(cv, w_mod, b_mod):
    depth, d, n6 = w_mod.shape
    rows = cv.shape[0]
    tn = 1536
    return pl.pallas_call(
        _mod_kernel,
        grid=(depth, n6 // tn),
        in_specs=[pl.BlockSpec((rows, d), lambda l, j: (0, 0)),
                  pl.BlockSpec((1, d, tn), lambda l, j: (l, 0, j)),
                  pl.BlockSpec((1, 1, tn), lambda l, j: (l, 0, j))],
        out_specs=pl.BlockSpec((1, rows, tn), lambda l, j: (l, 0, j)),
        out_shape=jax.ShapeDtypeStruct((depth, rows, n6), F32),
        compiler_params=_cparams(("arbitrary", "arbitrary")),
        name="modulation",
    )(cv, w_mod, b_mod.reshape(depth, 1, n6))


def _rms(x, g):
    return x * lax.rsqrt(jnp.mean(x * x, axis=-1, keepdims=True) + EPS) * g


def _inproj_kernel(h_ref, sh_ref, sc_ref, g_ref, w_ref, qn_ref, kvn_ref, wuq_ref, wuk_ref, wuv_ref,
                   ra_ref, rb_ref, ma_ref, mb_ref,
                   ret_ref, na_ref, q_ref, k_ref, v_ref):
    x = h_ref[0]
    u = _rms(x, g_ref[...]) * (1.0 + sc_ref[0]) + sh_ref[0]
    ub = u.astype(BF16)

    r = _dot(ub, w_ref[:, _C_RET:_C_NA])
    qk = r[:, :512]
    ra = jnp.concatenate([ra_ref[...]] * 4, axis=1)
    rb = jnp.concatenate([rb_ref[...]] * 4, axis=1)
    qk = qk * ra + _win_roll(qk, 16, 32) * rb
    ret_ref[0, :, 0:256] = qk[:, :256] * (RET_DK ** -0.5)
    ret_ref[0, :, 256:512] = qk[:, 256:]
    ret_ref[0, :, 512:1024] = r[:, 512:]

    n = _dot(ub, w_ref[:, _C_NA:_C_CQ])
    na_ref[0, :, 0:512] = (n[:, :512] * (NA_HD ** -0.5)).astype(BF16)
    na_ref[0, :, 512:1536] = n[:, 512:].astype(BF16)

    ma = jnp.concatenate([ma_ref[...]] * 4, axis=1)
    mb = jnp.concatenate([mb_ref[...]] * 4, axis=1)
    cq = _rms(_dot(ub, w_ref[:, _C_CQ:_C_CKV]), qn_ref[...]).astype(BF16)
    q = _dot(cq, wuq_ref[...])
    q = q * ma + _win_roll(q, 8, 16) * mb
    q_ref[0] = (q * MLA_Q_SCALE).astype(BF16)
    ckv = _rms(_dot(ub, w_ref[:, _C_CKV:_C_KPE]), kvn_ref[...]).astype(BF16)
    kp = _dot(ub, w_ref[:, _C_KPE:_W_EXT])
    kp = kp * ma_ref[...] + _win_roll(kp, 8, 16) * mb_ref[...]
    kp = jnp.concatenate([kp] * MLA_HEADS, axis=1)
    k_ref[0] = (_dot(ckv, wuk_ref[...]) + kp).astype(BF16)
    v = _dot(ckv, wuv_ref[...])
    lane = lax.broadcasted_iota(jnp.int32, v.shape, 1)
    v_ref[0] = jnp.where((lane & (LANES - 1)) == HEAD, 1.0, v).astype(BF16)


def _in_projection(h, sh, sc, g, w_ext, qn, kvn, wuq, wuk, wuv, tabs, tm):
    b, l, d = h.shape
    per_batch = sh.shape[0] > 1
    mod_map = (lambda i, j: (i, 0, 0)) if per_batch else (lambda i, j: (0, 0, 0))
    const2 = lambda i, j: (0, 0)
    tab_spec = pl.BlockSpec((tm, LANES), lambda i, j: (j, 0))
    tok = lambda w: pl.BlockSpec((1, tm, w), lambda i, j: (i, j, 0))
    return pl.pallas_call(
        _inproj_kernel,
        grid=(b, l // tm),
        in_specs=[tok(d),
                  pl.BlockSpec((1, 1, d), mod_map), pl.BlockSpec((1, 1, d), mod_map),
                  pl.BlockSpec((1, d), const2),
                  pl.BlockSpec(w_ext.shape, const2),
                  pl.BlockSpec(qn.shape, const2), pl.BlockSpec(kvn.shape, const2),
                  pl.BlockSpec(wuq.shape, const2), pl.BlockSpec(wuk.shape, const2),
                  pl.BlockSpec(wuv.shape, const2),
                  tab_spec, tab_spec, tab_spec, tab_spec],
        out_specs=[tok(1024), tok(1536), tok(512), tok(512), tok(512)],
        out_shape=[jax.ShapeDtypeStruct((b, l, 1024), F32),
                   jax.ShapeDtypeStruct((b, l, 1536), BF16),
                   jax.ShapeDtypeStruct((b, l, 512), BF16),
                   jax.ShapeDtypeStruct((b, l, 512), BF16),
                   jax.ShapeDtypeStruct((b, l, 512), BF16)],
        compiler_params=_cparams(("arbitrary", "arbitrary")),
        name="in_projection",
    )(h, sh, sc, g, w_ext, qn, kvn, wuq, wuk, wuv, *tabs)


def _retention_kernel(*refs, backward, group):
    if backward:
        (q_ref, k_ref, v_ref, g_ref, of_ref, dm_ref, qw_ref, kw_ref, gc_ref, s0_ref,
         o_ref, sfin_ref, state) = refs
    else:
        q_ref, k_ref, v_ref, dm_ref, qw_ref, kw_ref, gc_ref, s0_ref, o_ref, sfin_ref, state = refs
    step = pl.program_id(1)
    c = RET_CHUNK
    n_pairs = RET_HEADS // 2

    @pl.when(step == 0)
    def _():
        state[...] = s0_ref[0]

    lo = _low_half((c, LANES))
    row_lo = lax.broadcasted_iota(jnp.int32, (LANES, LANES), 0) < HEAD
    same_head = row_lo == _low_half((LANES, LANES))
    qw, kw = qw_ref[...], kw_ref[...]

    intra, kv, qd = {}, {}, {}
    for ci in range(group):
        rows = slice(c * ci, c * (ci + 1))
        for p in range(n_pairs):
            sl = slice(LANES * p, LANES * (p + 1))
            qp, kp = q_ref[0, rows, sl], k_ref[0, rows, sl]
            kb, vb = kp.astype(BF16), v_ref[0, rows, sl].astype(BF16)
            qa = jnp.where(lo, qp, 0.0).astype(BF16)
            qb = jnp.where(lo, 0.0, qp).astype(BF16)
            sa = (_dot_nt(qa, kb) * dm_ref[2 * p]).astype(BF16)
            sb = (_dot_nt(qb, kb) * dm_ref[2 * p + 1]).astype(BF16)
            intra[ci, p] = jnp.where(lo, _dot(sa, vb), _dot(sb, vb))
            kv[ci, p] = jnp.where(same_head, _dot((kp * kw[:, sl]).T.astype(BF16), vb), 0.0)
            qd[ci, p] = (qp * qw[:, sl]).astype(BF16)

    order = range(group - 1, -1, -1) if backward else range(group)
    out = {}
    for p in range(n_pairs):
        sl = slice(LANES * p, LANES * (p + 1))
        sp = state[sl, :]
        gc = gc_ref[sl, :]
        for ci in order:
            out[ci, p] = intra[ci, p] + _dot(qd[ci, p], sp.astype(BF16))
            sp = gc * sp + kv[ci, p]
        state[sl, :] = sp

    for ci in range(group):
        rows = slice(c * ci, c * (ci + 1))
        o = jnp.concatenate([out[ci, p] for p in range(n_pairs)], axis=1)
        if not backward:
            o_ref[0, rows, :] = o
            continue
        tot = of_ref[0, rows, :] + o
        g = g_ref[0, rows, :]
        normed = []
        for p in range(n_pairs):
            t = tot[:, LANES * p:LANES * (p + 1)]

            def head_mean(a):
                s_lo = jnp.sum(jnp.where(lo, a, 0.0), axis=-1, keepdims=True)
                s_hi = jnp.sum(jnp.where(lo, 0.0, a), axis=-1, keepdims=True)
                return jnp.where(lo, s_lo, s_hi) * (1.0 / HEAD)

            dlt = t - head_mean(t)
            normed.append(dlt * lax.rsqrt(head_mean(dlt * dlt) + EPS))
        on = jnp.concatenate(normed, axis=1)
        o_ref[0, rows, :] = (on * (g * _sigmoid(g))).astype(BF16)

    @pl.when(step == pl.num_programs(1) - 1)
    def _():
        sfin_ref[0] = state[...]


def _retention_sweep(ret, o_fwd, tabs, s0, *, backward):
    b, l, _ = ret.shape
    c = RET_CHUNK
    group = min(RET_GROUP, l // c)
    n = l // (c * group)
    gidx = (lambda s: n - 1 - s) if backward else (lambda s: s)
    tok = lambda col: pl.BlockSpec((1, c * group, 256), lambda i, s: (i, gidx(s), col))
    const = lambda a: pl.BlockSpec(a.shape, lambda i, s: (0,) * a.ndim)
    state_spec = pl.BlockSpec((1, 2 * LANES, LANES), lambda i, s: (i, 0, 0))
    dm, qw, kw, gc = tabs
    if backward:
        in_specs = [tok(0), tok(1), tok(2), tok(3), tok(0)]
        args = [ret, ret, ret, ret, o_fwd]
    else:
        in_specs = [tok(0), tok(1), tok(2)]
        args = [ret, ret, ret]
    return pl.pallas_call(
        functools.partial(_retention_kernel, backward=backward, group=group),
        grid=(b, n),
        in_specs=in_specs + [const(dm), const(qw), const(kw), const(gc), state_spec],
        out_specs=[tok(0), state_spec],
        out_shape=[jax.ShapeDtypeStruct((b, l, 256), BF16 if backward else F32),
                   jax.ShapeDtypeStruct((b, 2 * LANES, LANES), F32)],
        scratch_shapes=[pltpu.VMEM((2 * LANES, LANES), F32)],
        compiler_params=_cparams(("arbitrary", "arbitrary")),
        name="retention_backward" if backward else "retention_forward",
    )(*args, dm, qw, kw, gc, s0)


def _retention(ret, tabs, s0):
    fwd_tabs = tuple(t[0] for t in tabs)
    bwd_tabs = tuple(t[1] for t in tabs)
    o_f, s_f = _retention_sweep(ret, None, fwd_tabs, s0[0], backward=False)
    o, s_b = _retention_sweep(ret, o_f, bwd_tabs, s0[1], backward=True)
    return o, (s_f, s_b)


def _retention_tables(decay_f, decay_b):
    c = RET_CHUNK
    pos = jnp.arange(c, dtype=F32)
    diff = pos[:, None] - pos[None, :]
    lg = jnp.stack([jax.nn.log_sigmoid(decay_f.astype(F32)), jax.nn.log_sigmoid(decay_b.astype(F32))])
    lgh = lg[:, :, None, None]
    dm_f = jnp.where(diff >= 0, jnp.exp(lgh[0] * jnp.maximum(diff, 0.0)), 0.0)
    dm_b = jnp.where(diff < 0, jnp.exp(lgh[1] * jnp.maximum(-diff, 0.0)), 0.0)
    dm = jnp.stack([dm_f, dm_b])
    lanes = lambda t: jnp.repeat(t, RET_DK, axis=-1)
    lgc = lg[:, None, :]
    qw = lanes(jnp.exp(lgc * jnp.stack([pos + 1.0, c - pos])[:, :, None]))
    kw = lanes(jnp.exp(lgc * jnp.stack([c - 1.0 - pos, pos])[:, :, None]))
    gch = jnp.exp(lg * c)
    gc = jnp.broadcast_to(jnp.repeat(gch, RET_DK, axis=-1)[:, :, None], (2, RET_HEADS * RET_DK, LANES))
    return dm, qw, kw, gc


def _na_kernel(*refs, n_ctx):
    q_ref = refs[0]
    k_refs = refs[1:1 + NA_WIN_BLOCKS]
    v_refs = refs[1 + NA_WIN_BLOCKS:1 + 2 * NA_WIN_BLOCKS]
    kc_ref, vc_ref, bias_ref, o_ref, s_scr, p_scr, den_scr = refs[1 + 2 * NA_WIN_BLOCKS:]
    tq = NA_QBLOCK
    n_loc = NA_WIN_BLOCKS * tq
    n_lane_blocks = (n_loc + n_ctx) // LANES
    lo = _low_half((tq, LANES))
    for h in range(NA_HEADS):
        sl = slice(LANES * (h // 2), LANES * (h // 2 + 1))
        qp = q_ref[0, :, sl]
        qm = jnp.where(lo, qp, 0) if h % 2 == 0 else jnp.where(lo, 0, qp)
        for j in range(NA_WIN_BLOCKS):
            cs = slice(tq * j, tq * (j + 1))
            s_scr[h, :, cs] = _dot_nt(qm, k_refs[j][0, :, sl]) + bias_ref[0, h, :, cs]
        s_scr[h, :, n_loc:] = _dot_nt(qm, kc_ref[0, :, sl])
    for h in range(NA_HEADS):
        part = s_scr[h, :, 0:LANES]
        for c in range(1, n_lane_blocks):
            part = jnp.maximum(part, s_scr[h, :, LANES * c:LANES * (c + 1)])
        m = jnp.max(part, axis=-1, keepdims=True)
        tot = None
        for c in range(n_lane_blocks):
            cs = slice(LANES * c, LANES * (c + 1))
            e = jnp.exp(s_scr[h, :, cs] - m)
            tot = e if tot is None else tot + e
            p_scr[h, :, cs] = e.astype(BF16)
        den_scr[h] = jnp.broadcast_to(jnp.sum(tot, axis=-1, keepdims=True), (tq, LANES))
    for p in range(NA_HEADS // 2):
        sl = slice(LANES * p, LANES * (p + 1))
        halves = []
        for h in (2 * p, 2 * p + 1):
            o = _dot(p_scr[h, :, n_loc:], vc_ref[0, :, sl])
            for j in range(NA_WIN_BLOCKS):
                o = o + _dot(p_scr[h, :, tq * j:tq * (j + 1)], v_refs[j][0, :, sl])
            halves.append(o / den_scr[h])
        o_ref[0, :, sl] = jnp.where(lo, halves[0], halves[1]).astype(BF16)


def _na_variant(i, nblk):
    return jnp.where(i < 2, i, jnp.where(i <= nblk - 3, 2, i - (nblk - 2) + 3))


def _na_attention(na, na_ctx, bias):
    b, l, _ = na.shape
    n_ctx = na_ctx.shape[1]
    nblk = l // NA_QBLOCK
    tq = NA_QBLOCK

    def win(col, j):
        return pl.BlockSpec((1, tq, 512),
                            lambda bi, i: (bi, jnp.clip(i - 2, 0, nblk - NA_WIN_BLOCKS) + j, col))

    in_specs = ([pl.BlockSpec((1, tq, 512), lambda bi, i: (bi, i, 0))]
                + [win(1, j) for j in range(NA_WIN_BLOCKS)]
                + [win(2, j) for j in range(NA_WIN_BLOCKS)]
                + [pl.BlockSpec((1, n_ctx, 512), lambda bi, i: (bi, 0, 1)),
                   pl.BlockSpec((1, n_ctx, 512), lambda bi, i: (bi, 0, 2)),
                   pl.BlockSpec((1, NA_HEADS, tq, NA_WIN_BLOCKS * tq),
                                lambda bi, i: (_na_variant(i, nblk), 0, 0, 0))])
    args = [na] * (1 + 2 * NA_WIN_BLOCKS) + [na_ctx, na_ctx, bias]
    return pl.pallas_call(
        functools.partial(_na_kernel, n_ctx=n_ctx),
        grid=(b, nblk),
        in_specs=in_specs,
        out_specs=pl.BlockSpec((1, tq, 512), lambda bi, i: (bi, i, 0)),
        out_shape=jax.ShapeDtypeStruct((b, l, 512), BF16),
        scratch_shapes=[pltpu.VMEM((NA_HEADS, tq, NA_WIN_BLOCKS * tq + n_ctx), F32),
                        pltpu.VMEM((NA_HEADS, tq, NA_WIN_BLOCKS * tq + n_ctx), BF16),
                        pltpu.VMEM((NA_HEADS, tq, LANES), F32)],
        compiler_params=_cparams(("arbitrary", "arbitrary")),
        name="neighbourhood_attention",
    )(*args)


def _na_bias_plan(rows):
    nblk = rows * GRID_W // NA_QBLOCK
    kh = min(NA_KH, rows)
    q_rows = NA_QBLOCK // GRID_W
    k_rows = NA_WIN_BLOCKS * q_rows
    plan = []
    for i in (0, 1, 2, nblk - 2, nblk - 1):
        kb = min(max(i - 2, 0), nblk - NA_WIN_BLOCKS)
        per_q = []
        for rr in range(q_rows):
            r = q_rows * i + rr
            r0 = min(max(r - kh // 2, 0), rows - kh)
            per_q.append([(q_rows * kb + kj - r + NA_KH - 1) if r0 <= q_rows * kb + kj < r0 + kh else None
                          for kj in range(k_rows)])
        plan.append(per_q)
    return plan


def _na_bias(rpb, plan):
    c = np.arange(GRID_W)
    c0 = np.clip(c - NA_KW // 2, 0, GRID_W - NA_KW)
    kc = np.arange(GRID_W)
    ok = (kc[None, :] >= c0[:, None]) & (kc[None, :] < c0[:, None] + NA_KW)
    idx = np.where(ok, kc[None, :] - c[:, None] + NA_KW - 1, 0).astype(np.int32)
    cols = jnp.take(rpb.astype(F32), jnp.asarray(idx.reshape(-1)), axis=2)
    cols = cols.reshape(NA_HEADS, 2 * NA_KH - 1, GRID_W, GRID_W)
    cols = jnp.where(jnp.asarray(ok)[None, None], cols, NEG_BIG)
    outside = jnp.full((NA_HEADS, GRID_W, GRID_W), NEG_BIG, F32)
    variants = []
    for per_q in plan:
        q_parts = [jnp.concatenate([outside if dr is None else cols[:, dr] for dr in per_k], axis=-1)
                   for per_k in per_q]
        variants.append(jnp.concatenate(q_parts, axis=1))
    return jnp.stack(variants)


def _latent_attn_kernel(q_ref, k_ref, v_ref, o_ref, s_scr, p_scr, al_scr, m_scr, acc_scr, *, tk):
    n_k = k_ref.shape[1] // tk
    n_heads = v_ref.shape[2] // LANES
    n_lane_blocks = tk // LANES
    assert n_k % 2 == 1, "the two-slot hand-over below assumes an odd number of key tiles"

    def key_rows(j):
        return pl.ds(pl.multiple_of(j * tk, tk), tk)

    def scores(j, slot):
        for h in range(n_heads):
            sl = slice(LANES * h, LANES * (h + 1))
            s_scr[slot, h] = _dot_nt(q_ref[0, :, sl], k_ref[0, key_rows(j), sl])

    def softmax(slot):
        for h in range(n_heads):
            part = s_scr[slot, h, :, 0:LANES]
            for c in range(1, n_lane_blocks):
                part = jnp.maximum(part, s_scr[slot, h, :, LANES * c:LANES * (c + 1)])
            m_old = m_scr[h]
            m_new = jnp.maximum(m_old, jnp.max(part, axis=-1, keepdims=True))
            al_scr[slot, h] = jnp.exp2(m_old - m_new)
            m_scr[h] = m_new
            for c in range(n_lane_blocks):
                cs = slice(LANES * c, LANES * (c + 1))
                p_scr[slot, h, :, cs] = jnp.exp2(s_scr[slot, h, :, cs] - m_new).astype(BF16)

    def values(j, slot):
        for h in range(n_heads):
            sl = slice(LANES * h, LANES * (h + 1))
            acc_scr[h] = al_scr[slot, h] * acc_scr[h] + _dot(p_scr[slot, h], v_ref[0, key_rows(j), sl])

    m_scr[...] = jnp.full(m_scr.shape, -jnp.inf, F32)
    acc_scr[...] = jnp.zeros(acc_scr.shape, F32)
    p_scr[1] = jnp.zeros(p_scr.shape[1:], BF16)
    al_scr[1] = jnp.ones(al_scr.shape[1:], F32)
    scores(0, 0)

    def body(i, carry):
        j = 2 * i
        scores(j + 1, 1)
        softmax(0)
        values(jnp.maximum(j - 1, 0), 1)
        scores(j + 2, 0)
        softmax(1)
        values(j, 0)
        return carry

    lax.fori_loop(0, (n_k - 1) // 2, body, 0)
    softmax(0)
    if n_k > 1:
        values(n_k - 2, 1)
    values(n_k - 1, 0)
    outs = []
    for h in range(n_heads):
        a = acc_scr[h]
        outs.append(a[:, 0:HEAD] / a[:, HEAD:HEAD + 1])
    o_ref[0] = jnp.concatenate(outs, axis=1).astype(BF16)


def _latent_attention(q, k, v, *, tq, tk, name):
    b, lq, w = q.shape
    lk = k.shape[1]
    n_heads = w // LANES
    resident = lambda: pl.BlockSpec((1, lk, w), lambda bi, i: (bi, 0, 0), pipeline_mode=pl.Buffered(1))
    return pl.pallas_call(
        functools.partial(_latent_attn_kernel, tk=tk),
        grid=(b, lq // tq),
        in_specs=[pl.BlockSpec((1, tq, w), lambda bi, i: (bi, i, 0)), resident(), resident()],
        out_specs=pl.BlockSpec((1, tq, n_heads * HEAD), lambda bi, i: (bi, i, 0)),
        out_shape=jax.ShapeDtypeStruct((b, lq, n_heads * HEAD), BF16),
        scratch_shapes=[pltpu.VMEM((2, n_heads, tq, tk), F32), pltpu.VMEM((2, n_heads, tq, tk), BF16),
                        pltpu.VMEM((2, n_heads, tq, LANES), F32),
                        pltpu.VMEM((n_heads, tq, LANES), F32), pltpu.VMEM((n_heads, tq, LANES), F32)],
        compiler_params=_cparams(("arbitrary", "arbitrary")),
        name=name,
    )(q, k, v)


def _pair_attn_kernel(q_ref, k_ref, v_ref, o_ref):
    tq = q_ref.shape[1]
    lo = _low_half((tq, LANES))
    for p in range(v_ref.shape[2] // LANES):
        sl = slice(LANES * p, LANES * (p + 1))
        qp, kp, vp = q_ref[0, :, sl], k_ref[0, :, sl], v_ref[0, :, sl]
        halves = []
        for half in range(2):
            qm = jnp.where(lo, qp, 0) if half == 0 else jnp.where(lo, 0, qp)
            s = _dot_nt(qm, kp)
            e = jnp.exp(s - jnp.max(s, axis=-1, keepdims=True))
            halves.append(_dot(e.astype(BF16), vp) / jnp.sum(e, axis=-1, keepdims=True))
        o_ref[0, :, sl] = jnp.where(lo, halves[0], halves[1]).astype(BF16)


def _pair_attention(q, k, v, *, tq, name):
    b, lq, w = q.shape
    lk = k.shape[1]
    return pl.pallas_call(
        _pair_attn_kernel,
        grid=(b, lq // tq),
        in_specs=[pl.BlockSpec((1, tq, w), lambda bi, i: (bi, i, 0)),
                  pl.BlockSpec((1, lk, w), lambda bi, i: (bi, 0, 0)),
                  pl.BlockSpec((1, lk, w), lambda bi, i: (bi, 0, 0))],
        out_specs=pl.BlockSpec((1, tq, w), lambda bi, i: (bi, i, 0)),
        out_shape=jax.ShapeDtypeStruct((b, lq, w), BF16),
        compiler_params=_cparams(("arbitrary", "arbitrary")),
        name=name,
    )(q, k, v)


def _gates(logits, bias):
    scores = _sigmoid(logits)
    sel = scores + bias
    lane = lax.broadcasted_iota(jnp.int32, logits.shape, 1)
    e4 = lane & (EXPERTS_PER_GROUP - 1)
    grp = (lane >> 2) & (N_GROUPS - 1)
    one, zero = jnp.float32(1.0), jnp.float32(0.0)

    rank = jnp.zeros(logits.shape, F32)
    for d in range(1, EXPERTS_PER_GROUP):
        other = _win_roll(sel, d, EXPERTS_PER_GROUP)
        other_first = ((e4 + d) & (EXPERTS_PER_GROUP - 1)) < e4
        ahead = jnp.logical_or(other > sel, jnp.logical_and(other == sel, other_first))
        rank = rank + jnp.where(ahead, one, zero)
    top2 = rank < float(TOP_K)

    def group_sum(a):
        tot = a
        for d in range(1, EXPERTS_PER_GROUP):
            tot = tot + _win_roll(a, d, EXPERTS_PER_GROUP)
        return tot

    gscore = group_sum(jnp.where(top2, sel, zero))
    beaten = jnp.zeros(logits.shape, F32)
    for d in range(1, N_GROUPS):
        other = _win_roll(gscore, d * EXPERTS_PER_GROUP, N_EXPERTS)
        other_first = ((grp + d) & (N_GROUPS - 1)) < grp
        ahead = jnp.logical_or(other > gscore, jnp.logical_and(other == gscore, other_first))
        beaten = beaten + jnp.where(ahead, one, zero)
    chosen = jnp.logical_and(jnp.logical_and(beaten < 0.5, top2), lane < N_EXPERTS)
    w = jnp.where(chosen, scores, zero)
    den = group_sum(w)
    return jnp.where(chosen, w / den, zero)


def _outproj_kernel(ret_ref, na_ref, mla_ref, h_ref, w_ref, g1_ref, sh_ref, sc_ref, n2_ref,
                    rw_ref, hn_ref, u_ref, logit_ref):
    mix = (_dot(ret_ref[0], w_ref[0:256, :]) + _dot(na_ref[0], w_ref[256:768, :])
           + _dot(mla_ref[0], w_ref[768:1024, :]))
    hn = h_ref[0] + g1_ref[0] * mix
    hn_ref[0] = hn
    u = _rms(hn, n2_ref[...]) * (1.0 + sc_ref[0]) + sh_ref[0]
    u_hi = u.astype(BF16)
    u_ref[0] = u_hi
    tm = u.shape[0]
    u_lo = (u - u_hi.astype(F32)).astype(BF16)
    both = _dot(jnp.concatenate([u_hi, u_lo], axis=0), rw_ref[...])
    logits = both[:tm, :LANES] + both[:tm, LANES:] + both[tm:, :LANES]
    logit_ref[0] = logits


def _out_projection(ret_o, na_o, mla_o, h, w_out, g1, sh2, sc2, n2, rw, tm):
    b, l, d = h.shape
    per_batch = g1.shape[0] > 1
    mod_map = (lambda i, j: (i, 0, 0)) if per_batch else (lambda i, j: (0, 0, 0))
    const2 = lambda i, j: (0, 0)
    tok = lambda w: pl.BlockSpec((1, tm, w), lambda i, j: (i, j, 0))
    mod = pl.BlockSpec((1, 1, d), mod_map)
    return pl.pallas_call(
        _outproj_kernel,
        grid=(b, l // tm),
        in_specs=[tok(256), tok(512), tok(256), tok(d),
                  pl.BlockSpec(w_out.shape, const2), mod, mod, mod,
                  pl.BlockSpec((1, d), const2),
                  pl.BlockSpec(rw.shape, const2)],
        out_specs=[tok(d), tok(d), tok(LANES)],
        out_shape=[jax.ShapeDtypeStruct((b, l, d), F32),
                   jax.ShapeDtypeStruct((b, l, d), BF16),
                   jax.ShapeDtypeStruct((b, l, LANES), F32)],
        compiler_params=_cparams(("arbitrary", "arbitrary")),
        name="out_projection",
    )(ret_o, na_o, mla_o, h, w_out, g1, sh2, sc2, n2, rw)


def _moe_kernel(u_ref, logit_ref, logit_next_ref, rb_ref, h_ref, w13_ref, w2_ref, g2_ref, fg_ref,
                o_ref, gate_scr, *, final_norm):
    u = u_ref[...]

    @pl.when(pl.program_id(0) == 0)
    def _():
        gate_scr[...] = _gates(logit_ref[...], rb_ref[...])

    gates = gate_scr[...]
    gates_next = _gates(logit_next_ref[...], rb_ref[...])
    lane = lax.broadcasted_iota(jnp.int32, gates.shape, 1)
    y = None
    for e in range(N_EXPERTS):
        a = _dot(u, w13_ref[e])
        a1, a3 = a[:, :EXPERT_FF], a[:, EXPERT_FF:]
        gcol = jnp.sum(jnp.where(lane == e, gates, 0.0), axis=-1, keepdims=True)
        hdn = ((a1 * _sigmoid(a1)) * a3 * gcol).astype(BF16)
        ye = _dot(hdn, w2_ref[e])
        y = ye if y is None else y + ye
    hn = h_ref[...] + g2_ref[0] * y
    if final_norm:
        hn = _rms(hn, fg_ref[...])
    o_ref[...] = hn
    gate_scr[...] = gates_next


def _moe(u, logits, rb, h, w13, w2, g2, fg, tokens_per_mod, tm, final_norm):
    t, d = h.shape
    per_batch = g2.shape[0] > 1
    tiles_per_mod = tokens_per_mod // tm
    n_tiles = t // tm
    mod_map = (lambda i: (i // tiles_per_mod, 0, 0)) if per_batch else (lambda i: (0, 0, 0))
    tok = lambda w: pl.BlockSpec((tm, w), lambda i: (i, 0))
    resident = lambda a: pl.BlockSpec(a.shape, lambda i: (0, 0, 0), pipeline_mode=pl.Buffered(1))
    return pl.pallas_call(
        functools.partial(_moe_kernel, final_norm=final_norm),
        grid=(n_tiles,),
        in_specs=[tok(d), tok(LANES),
                  pl.BlockSpec((tm, LANES), lambda i: (jnp.minimum(i + 1, n_tiles - 1), 0)),
                  pl.BlockSpec(rb.shape, lambda i: (0, 0)), tok(d),
                  resident(w13), resident(w2),
                  pl.BlockSpec((1, 1, d), mod_map),
                  pl.BlockSpec((1, d), lambda i: (0, 0))],
        out_specs=tok(d),
        out_shape=jax.ShapeDtypeStruct((t, d), F32),
        scratch_shapes=[pltpu.VMEM((tm, LANES), F32)],
        compiler_params=_cparams(("arbitrary",)),
        name="mixture_of_experts",
    )(u, logits, logits, rb, h, w13, w2, g2, fg)


def _rope_tables(seq, positional):
    if not positional:
        ones = jnp.ones((seq, LANES), F32)
        zeros = jnp.zeros((seq, LANES), F32)
        return ones, zeros, ones, zeros
    t = jnp.arange(seq)
    pos_r = (t // GRID_W).astype(F32)[:, None]
    pos_c = (t % GRID_W).astype(F32)[:, None]

    def axial(width):
        h = width // 2
        half = h // 2
        inv = ROPE_BASE ** (-jnp.arange(half, dtype=F32) / half)
        inv2 = jnp.concatenate([inv, inv])[None, :]
        ang = jnp.concatenate([pos_r * inv2, pos_c * inv2], axis=1)
        sign = jnp.tile(jnp.concatenate([-jnp.ones(half), jnp.ones(half)]), 2)[None, :]
        return jnp.cos(ang), jnp.sin(ang) * sign

    rc, rs = axial(RET_DK)
    ra = jnp.concatenate([rc, rc], axis=1)
    rb = jnp.concatenate([rs, rs], axis=1)
    mc, ms = axial(MLA_ROPE)
    ma = jnp.concatenate([jnp.ones((seq, MLA_NOPE), F32), mc, jnp.zeros((seq, 32), F32)], axis=1)
    mb = jnp.concatenate([jnp.zeros((seq, MLA_NOPE), F32), ms, jnp.zeros((seq, 32), F32)], axis=1)
    return ra, rb, ma, mb


def _pad_heads(w, heads, width):
    r = w.shape[0]
    w = w.reshape(r, heads, width)
    return jnp.pad(w, ((0, 0), (0, 0), (0, LANES - width))).reshape(r, heads * LANES)


def _layer_weights(w_in, w_uq, w_ukv):
    d = w_in.shape[0]
    kpe = w_in[:, _C_KPE:_C_KPE + MLA_ROPE]
    blk = jnp.concatenate([jnp.zeros((d, MLA_NOPE), w_in.dtype), kpe,
                           jnp.zeros((d, LANES - MLA_NOPE - MLA_ROPE), w_in.dtype)], axis=1)
    w_ext = jnp.concatenate([w_in[:, :_C_KPE], blk], axis=1).astype(BF16)
    wuq = _pad_heads(w_uq, MLA_HEADS, MLA_NOPE + MLA_ROPE).astype(BF16)
    ukv = w_ukv.reshape(w_ukv.shape[0], MLA_HEADS, MLA_NOPE + MLA_VD)
    wuk = _pad_heads(ukv[:, :, :MLA_NOPE].reshape(w_ukv.shape[0], -1), MLA_HEADS, MLA_NOPE).astype(BF16)
    wuv = _pad_heads(ukv[:, :, MLA_NOPE:].reshape(w_ukv.shape[0], -1), MLA_HEADS, MLA_VD).astype(BF16)
    return w_ext, wuq, wuk, wuv


def _latent_key_tile(n_keys):
    for tk in range(768, 0, -LANES):
        if n_keys % tk == 0 and (n_keys // tk) % 2 == 1:
            return tk
    raise ValueError(f"no odd tiling of {n_keys} keys")


def kernel(x, c, ctx, c_ctx, w_mod, b_mod, norm1_g, norm2_g, w_in, ret_decay_f, ret_decay_b, na_rpb,
           mla_q_norm, mla_kv_norm, w_uq, w_ukv, w_out, router_w, router_b, w1, w3, w2, final_norm_g):
    b, s, d = x.shape
    lc = ctx.shape[1]
    depth = w_mod.shape[0]
    rows = s // GRID_W
    tm_lat = 512
    tm_ctx = min(256, lc)

    n_mod = -(-(b + 1) // 8) * 8
    cv = jnp.concatenate([c, c_ctx[None, :], jnp.zeros((n_mod - b - 1, d), F32)], axis=0)
    mods = _modulation(cv, w_mod, b_mod).reshape(depth, n_mod, 6, d)

    tabs_lat = _rope_tables(s, True)
    tabs_ctx = _rope_tables(lc, False)
    na_plan = _na_bias_plan(rows)
    rw_f32 = jnp.pad(router_w.astype(F32), ((0, 0), (0, LANES - N_EXPERTS)))
    rw_hi = rw_f32.astype(BF16)
    rw = jnp.concatenate([rw_hi, (rw_f32 - rw_hi.astype(F32)).astype(BF16)], axis=1)
    rb = jnp.pad(router_b.astype(F32), (0, LANES - N_EXPERTS)).reshape(1, LANES)
    zero_state = (jnp.zeros((b, 2 * LANES, LANES), F32),) * 2

    h, hc = x, ctx
    for l in range(depth):
        last = l == depth - 1
        m_lat = [mods[l, :b, i][:, None, :] for i in range(6)]
        m_ctx = [mods[l, b:b + 1, i][:, None, :] for i in range(6)]
        w_ext, wuq, wuk, wuv = _layer_weights(w_in[l], w_uq[l], w_ukv[l])
        n1 = norm1_g[l].reshape(1, d)
        n2 = norm2_g[l].reshape(1, d)
        qn = mla_q_norm[l].reshape(1, -1)
        kvn = mla_kv_norm[l].reshape(1, -1)
        wo = w_out[l].astype(BF16)
        w13 = jnp.concatenate([w1[l], w3[l]], axis=-1).astype(BF16)
        w2b = w2[l].astype(BF16)
        fg = final_norm_g.reshape(1, d)

        ret_c, na_c, mq_c, mk_c, mv_c = _in_projection(
            hc, m_ctx[0], m_ctx[1], n1, w_ext, qn, kvn, wuq, wuk, wuv, tabs_ctx, tm_ctx)
        ret_l, na_l, mq_l, mk_l, mv_l = _in_projection(
            h, m_lat[0], m_lat[1], n1, w_ext, qn, kvn, wuq, wuk, wuv, tabs_lat, tm_lat)

        rtabs = _retention_tables(ret_decay_f[l], ret_decay_b[l])
        ro_c, st = _retention(ret_c, rtabs, zero_state)
        ro_l, _ = _retention(ret_l, rtabs, st)

        na_o = _na_attention(na_l, na_c, _na_bias(na_rpb[l], na_plan))
        mla_o = _latent_attention(mq_l, jnp.concatenate([mk_l, mk_c], axis=1),
                                  jnp.concatenate([mv_l, mv_c], axis=1),
                                  tq=256, tk=_latent_key_tile(s + lc), name="latent_attention")

        hn, u2, logits = _out_projection(ro_l, na_o, mla_o, h, wo, m_lat[2], m_lat[3], m_lat[4],
                                         n2, rw, tm_lat)
        h = _moe(u2.reshape(b * s, d), logits.reshape(b * s, LANES), rb, hn.reshape(b * s, d),
                 w13, w2b, m_lat[5], fg, s, 512, last).reshape(b, s, d)

        if not last:
            nac_o = _pair_attention(na_c[:, :, 0:512], na_c[:, :, 512:1024], na_c[:, :, 1024:1536],
                                    tq=tm_ctx, name="context_na_attention")
            mlac_o = _latent_attention(mq_c, mk_c, mv_c, tq=tm_ctx, tk=lc,
                                       name="context_latent_attention")
            hcn, uc2, logits_c = _out_projection(ro_c, nac_o, mlac_o, hc, wo, m_ctx[2], m_ctx[3],
                                                 m_ctx[4], n2, rw, tm_ctx)
            hc = _moe(uc2.reshape(b * lc, d), logits_c.reshape(b * lc, LANES), rb,
                      hcn.reshape(b * lc, d), w13, w2b, m_ctx[5], fg, lc, min(512, lc),
                      False).reshape(b, lc, d)
    return h
```

```python
import functools

import numpy as np
import jax
import jax.numpy as jnp
from jax import lax
from jax.experimental import pallas as pl
from jax.experimental.pallas import tpu as pltpu

F32 = jnp.float32
BF16 = jnp.bfloat16

GRID_W = 64
EPS = 1e-6
ROPE_BASE = 10000.0
RET_HEADS, RET_DK, RET_DV, RET_CHUNK = 4, 64, 64, 128
NA_HEADS, NA_HD, NA_KH, NA_KW, NA_QBLOCK = 8, 64, 8, 16, 128
MLA_HEADS, MLA_NOPE, MLA_ROPE, MLA_VD = 4, 64, 32, 64
MLA_Q_RANK, MLA_KV_RANK = 384, 256
N_EXPERTS, N_GROUPS, TOP_K, EXPERT_FF = 16, 4, 2, 256
EXPERTS_PER_GROUP = N_EXPERTS // N_GROUPS

LANES = 128
HEAD = 64
NA_WIN_BLOCKS = 5
NA_SUB = 2
NA_STEP_BLOCKS = NA_WIN_BLOCKS + NA_SUB - 1
RET_GROUP = 8
NEG_BIG = -1e30
VMEM_LIMIT = 48 * 1024 * 1024

_C_RET, _C_NA, _C_CQ, _C_CKV, _C_KPE = 0, 1024, 2560, 2944, 3200
_W_EXT = _C_KPE + LANES
MLA_Q_SCALE = (MLA_NOPE + MLA_ROPE) ** -0.5 * float(np.log2(np.e))


def _cparams(sem):
    return pltpu.CompilerParams(dimension_semantics=sem, vmem_limit_bytes=VMEM_LIMIT)


def _dot(a, b):
    return jnp.dot(a, b, preferred_element_type=F32)


def _dot_nt(a, b):
    return lax.dot_general(a, b, (((1,), (1,)), ((), ())), preferred_element_type=F32)


def _sigmoid(x):
    return 1.0 / (1.0 + jnp.exp(-x))


def _win_roll(x, s, w):
    n = x.shape[-1]
    ax = x.ndim - 1
    lane = lax.broadcasted_iota(jnp.int32, x.shape, ax)
    fwd = pltpu.roll(x, n - s, ax)
    bwd = pltpu.roll(x, w - s, ax)
    return jnp.where((lane & (w - 1)) + s < w, fwd, bwd)


def _low_half(shape):
    lane = lax.broadcasted_iota(jnp.int32, shape, len(shape) - 1)
    return (lane & (LANES - 1)) < HEAD


def _mod_kernel(cv_ref, w_ref, b_ref, o_ref):
    cv = cv_ref[...]
    a = cv * _sigmoid(cv)
    o_ref[0] = jnp.dot(a, w_ref[0], preferred_element_type=F32,
                       precision=lax.Precision.HIGHEST) + b_ref[0]


def _modulation(cv, w_mod, b_mod):
    depth, d, n6 = w_mod.shape
    rows = cv.shape[0]
    tn = 1536
    return pl.pallas_call(
        _mod_kernel,
        grid=(depth, n6 // tn),
        in_specs=[pl.BlockSpec((rows, d), lambda l, j: (0, 0)),
                  pl.BlockSpec((1, d, tn), lambda l, j: (l, 0, j)),
                  pl.BlockSpec((1, 1, tn), lambda l, j: (l, 0, j))],
        out_specs=pl.BlockSpec((1, rows, tn), lambda l, j: (l, 0, j)),
        out_shape=jax.ShapeDtypeStruct((depth, rows, n6), F32),
        compiler_params=_cparams(("arbitrary", "arbitrary")),
        name="modulation",
    )(cv, w_mod, b_mod.reshape(depth, 1, n6))


def _rms(x, g):
    return x * lax.rsqrt(jnp.mean(x * x, axis=-1, keepdims=True) + EPS) * g


def _inproj_kernel(h_ref, sh_ref, sc_ref, g_ref, w_ref, qn_ref, kvn_ref, wuq_ref, wuk_ref, wuv_ref,
                   ra_ref, rb_ref, ma_ref, mb_ref,
                   ret_ref, na_ref, q_ref, k_ref, v_ref):
    x = h_ref[0]
    u = _rms(x, g_ref[...]) * (1.0 + sc_ref[0]) + sh_ref[0]
    ub = u.astype(BF16)

    r = _dot(ub, w_ref[:, _C_RET:_C_NA])
    qk = r[:, :512]
    ra = jnp.concatenate([ra_ref[...]] * 4, axis=1)
    rb = jnp.concatenate([rb_ref[...]] * 4, axis=1)
    qk = qk * ra + _win_roll(qk, 16, 32) * rb
    ret_ref[0, :, 0:256] = qk[:, :256] * (RET_DK ** -0.5)
    ret_ref[0, :, 256:512] = qk[:, 256:]
    ret_ref[0, :, 512:1024] = r[:, 512:]

    n = _dot(ub, w_ref[:, _C_NA:_C_CQ])
    na_ref[0, :, 0:512] = (n[:, :512] * (NA_HD ** -0.5)).astype(BF16)
    na_ref[0, :, 512:1536] = n[:, 512:].astype(BF16)

    ma = jnp.concatenate([ma_ref[...]] * 4, axis=1)
    mb = jnp.concatenate([mb_ref[...]] * 4, axis=1)
    cq = _rms(_dot(ub, w_ref[:, _C_CQ:_C_CKV]), qn_ref[...]).astype(BF16)
    q = _dot(cq, wuq_ref[...])
    q = q * ma + _win_roll(q, 8, 16) * mb
    q_ref[0] = (q * MLA_Q_SCALE).astype(BF16)
    ckv = _rms(_dot(ub, w_ref[:, _C_CKV:_C_KPE]), kvn_ref[...]).astype(BF16)
    kp = _dot(ub, w_ref[:, _C_KPE:_W_EXT])
    kp = kp * ma_ref[...] + _win_roll(kp, 8, 16) * mb_ref[...]
    kp = jnp.concatenate([kp] * MLA_HEADS, axis=1)
    k_ref[0] = (_dot(ckv, wuk_ref[...]) + kp).astype(BF16)
    v = _dot(ckv, wuv_ref[...])
    lane = lax.broadcasted_iota(jnp.int32, v.shape, 1)
    v_ref[0] = jnp.where((lane & (LANES - 1)) == HEAD, 1.0, v).astype(BF16)


def _in_projection(h, sh, sc, g, w_ext, qn, kvn, wuq, wuk, wuv, tabs, tm):
    b, l, d = h.shape
    per_batch = sh.shape[0] > 1
    mod_map = (lambda i, j: (i, 0, 0)) if per_batch else (lambda i, j: (0, 0, 0))
    const2 = lambda i, j: (0, 0)
    tab_spec = pl.BlockSpec((tm, LANES), lambda i, j: (j, 0))
    tok = lambda w: pl.BlockSpec((1, tm, w), lambda i, j: (i, j, 0))
    return pl.pallas_call(
        _inproj_kernel,
        grid=(b, l // tm),
        in_specs=[tok(d),
                  pl.BlockSpec((1, 1, d), mod_map), pl.BlockSpec((1, 1, d), mod_map),
                  pl.BlockSpec((1, d), const2),
                  pl.BlockSpec(w_ext.shape, const2),
                  pl.BlockSpec(qn.shape, const2), pl.BlockSpec(kvn.shape, const2),
                  pl.BlockSpec(wuq.shape, const2), pl.BlockSpec(wuk.shape, const2),
                  pl.BlockSpec(wuv.shape, const2),
                  tab_spec, tab_spec, tab_spec, tab_spec],
        out_specs=[tok(1024), tok(1536), tok(512), tok(512), tok(512)],
        out_shape=[jax.ShapeDtypeStruct((b, l, 1024), F32),
                   jax.ShapeDtypeStruct((b, l, 1536), BF16),
                   jax.ShapeDtypeStruct((b, l, 512), BF16),
                   jax.ShapeDtypeStruct((b, l, 512), BF16),
                   jax.ShapeDtypeStruct((b, l, 512), BF16)],
        compiler_params=_cparams(("arbitrary", "arbitrary")),
        name="in_projection",
    )(h, sh, sc, g, w_ext, qn, kvn, wuq, wuk, wuv, *tabs)


def _retention_kernel(*refs, backward, group):
    if backward:
        (q_ref, k_ref, v_ref, g_ref, of_ref, dm_ref, qw_ref, kw_ref, gc_ref, s0_ref,
         o_ref, sfin_ref, state) = refs
    else:
        q_ref, k_ref, v_ref, dm_ref, qw_ref, kw_ref, gc_ref, s0_ref, o_ref, sfin_ref, state = refs
    step = pl.program_id(1)
    c = RET_CHUNK
    n_pairs = RET_HEADS // 2

    @pl.when(step == 0)
    def _():
        state[...] = s0_ref[0]

    lo = _low_half((c, LANES))
    row_lo = lax.broadcasted_iota(jnp.int32, (LANES, LANES), 0) < HEAD
    same_head = row_lo == _low_half((LANES, LANES))
    qw, kw = qw_ref[...], kw_ref[...]

    intra, kv, qd = {}, {}, {}
    for ci in range(group):
        rows = slice(c * ci, c * (ci + 1))
        for p in range(n_pairs):
            sl = slice(LANES * p, LANES * (p + 1))
            qp, kp = q_ref[0, rows, sl], k_ref[0, rows, sl]
            kb, vb = kp.astype(BF16), v_ref[0, rows, sl].astype(BF16)
            qa = jnp.where(lo, qp, 0.0).astype(BF16)
            qb = jnp.where(lo, 0.0, qp).astype(BF16)
            sa = (_dot_nt(qa, kb) * dm_ref[2 * p]).astype(BF16)
            sb = (_dot_nt(qb, kb) * dm_ref[2 * p + 1]).astype(BF16)
            intra[ci, p] = jnp.where(lo, _dot(sa, vb), _dot(sb, vb))
            kv[ci, p] = jnp.where(same_head, _dot((kp * kw[:, sl]).T.astype(BF16), vb), 0.0)
            qd[ci, p] = (qp * qw[:, sl]).astype(BF16)

    order = range(group - 1, -1, -1) if backward else range(group)
    out = {}
    for p in range(n_pairs):
        sl = slice(LANES * p, LANES * (p + 1))
        sp = state[sl, :]
        gc = gc_ref[sl, :]
        for ci in order:
            out[ci, p] = intra[ci, p] + _dot(qd[ci, p], sp.astype(BF16))
            sp = gc * sp + kv[ci, p]
        state[sl, :] = sp

    for ci in range(group):
        rows = slice(c * ci, c * (ci + 1))
        o = jnp.concatenate([out[ci, p] for p in range(n_pairs)], axis=1)
        if not backward:
            o_ref[0, rows, :] = o
            continue
        tot = of_ref[0, rows, :] + o
        g = g_ref[0, rows, :]
        normed = []
        for p in range(n_pairs):
            t = tot[:, LANES * p:LANES * (p + 1)]

            def head_mean(a):
                s_lo = jnp.sum(jnp.where(lo, a, 0.0), axis=-1, keepdims=True)
                s_hi = jnp.sum(jnp.where(lo, 0.0, a), axis=-1, keepdims=True)
                return jnp.where(lo, s_lo, s_hi) * (1.0 / HEAD)

            dlt = t - head_mean(t)
            normed.append(dlt * lax.rsqrt(head_mean(dlt * dlt) + EPS))
        on = jnp.concatenate(normed, axis=1)
        o_ref[0, rows, :] = (on * (g * _sigmoid(g))).astype(BF16)

    @pl.when(step == pl.num_programs(1) - 1)
    def _():
        sfin_ref[0] = state[...]


def _retention_sweep(ret, o_fwd, tabs, s0, *, backward):
    b, l, _ = ret.shape
    c = RET_CHUNK
    group = min(RET_GROUP, l // c)
    n = l // (c * group)
    gidx = (lambda s: n - 1 - s) if backward else (lambda s: s)
    tok = lambda col: pl.BlockSpec((1, c * group, 256), lambda i, s: (i, gidx(s), col))
    const = lambda a: pl.BlockSpec(a.shape, lambda i, s: (0,) * a.ndim)
    state_spec = pl.BlockSpec((1, 2 * LANES, LANES), lambda i, s: (i, 0, 0))
    dm, qw, kw, gc = tabs
    if backward:
        in_specs = [tok(0), tok(1), tok(2), tok(3), tok(0)]
        args = [ret, ret, ret, ret, o_fwd]
    else:
        in_specs = [tok(0), tok(1), tok(2)]
        args = [ret, ret, ret]
    return pl.pallas_call(
        functools.partial(_retention_kernel, backward=backward, group=group),
        grid=(b, n),
        in_specs=in_specs + [const(dm), const(qw), const(kw), const(gc), state_spec],
        out_specs=[tok(0), state_spec],
        out_shape=[jax.ShapeDtypeStruct((b, l, 256), BF16 if backward else F32),
                   jax.ShapeDtypeStruct((b, 2 * LANES, LANES), F32)],
        scratch_shapes=[pltpu.VMEM((2 * LANES, LANES), F32)],
        compiler_params=_cparams(("arbitrary", "arbitrary")),
        name="retention_backward" if backward else "retention_forward",
    )(*args, dm, qw, kw, gc, s0)


def _retention(ret, tabs, s0):
    fwd_tabs = tuple(t[0] for t in tabs)
    bwd_tabs = tuple(t[1] for t in tabs)
    o_f, s_f = _retention_sweep(ret, None, fwd_tabs, s0[0], backward=False)
    o, s_b = _retention_sweep(ret, o_f, bwd_tabs, s0[1], backward=True)
    return o, (s_f, s_b)


def _retention_tables(decay_f, decay_b):
    c = RET_CHUNK
    pos = jnp.arange(c, dtype=F32)
    diff = pos[:, None] - pos[None, :]
    lg = jnp.stack([jax.nn.log_sigmoid(decay_f.astype(F32)), jax.nn.log_sigmoid(decay_b.astype(F32))])
    lgh = lg[:, :, None, None]
    dm_f = jnp.where(diff >= 0, jnp.exp(lgh[0] * jnp.maximum(diff, 0.0)), 0.0)
    dm_b = jnp.where(diff < 0, jnp.exp(lgh[1] * jnp.maximum(-diff, 0.0)), 0.0)
    dm = jnp.stack([dm_f, dm_b])
    lanes = lambda t: jnp.repeat(t, RET_DK, axis=-1)
    lgc = lg[:, None, :]
    qw = lanes(jnp.exp(lgc * jnp.stack([pos + 1.0, c - pos])[:, :, None]))
    kw = lanes(jnp.exp(lgc * jnp.stack([c - 1.0 - pos, pos])[:, :, None]))
    gch = jnp.exp(lg * c)
    gc = jnp.broadcast_to(jnp.repeat(gch, RET_DK, axis=-1)[:, :, None], (2, RET_HEADS * RET_DK, LANES))
    return dm, qw, kw, gc


def _na_kernel(*refs, n_ctx):
    w = NA_STEP_BLOCKS
    q_ref = refs[0]
    k_refs, v_refs = refs[1:1 + w], refs[1 + w:1 + 2 * w]
    kc_ref, vc_ref, bias_ref, o_ref, s_scr, p_scr, den_scr = refs[1 + 2 * w:]
    tq = q_ref.shape[1]
    n_loc = w * NA_QBLOCK
    n_lane_blocks = (n_loc + n_ctx) // LANES
    lo = _low_half((tq, LANES))
    for h in range(NA_HEADS):
        sl = slice(LANES * (h // 2), LANES * (h // 2 + 1))
        qp = q_ref[0, :, sl]
        qm = jnp.where(lo, qp, 0) if h % 2 == 0 else jnp.where(lo, 0, qp)
        kwin = jnp.concatenate([r[0, :, sl] for r in k_refs], axis=0)
        s_scr[h, :, 0:n_loc] = _dot_nt(qm, kwin) + bias_ref[0, h]
        s_scr[h, :, n_loc:] = _dot_nt(qm, kc_ref[0, :, sl])
    for h in range(NA_HEADS):
        part = s_scr[h, :, 0:LANES]
        for c in range(1, n_lane_blocks):
            part = jnp.maximum(part, s_scr[h, :, LANES * c:LANES * (c + 1)])
        m = jnp.max(part, axis=-1, keepdims=True)
        tot = None
        for c in range(n_lane_blocks):
            cs = slice(LANES * c, LANES * (c + 1))
            e = jnp.exp(s_scr[h, :, cs] - m)
            tot = e if tot is None else tot + e
            p_scr[h, :, cs] = e.astype(BF16)
        den_scr[h] = jnp.broadcast_to(jnp.sum(tot, axis=-1, keepdims=True), (tq, LANES))
    for p in range(NA_HEADS // 2):
        sl = slice(LANES * p, LANES * (p + 1))
        vwin = jnp.concatenate([r[0, :, sl] for r in v_refs], axis=0)
        halves = []
        for h in (2 * p, 2 * p + 1):
            o = _dot(p_scr[h, :, 0:n_loc], vwin) + _dot(p_scr[h, :, n_loc:], vc_ref[0, :, sl])
            halves.append(o / den_scr[h])
        o_ref[0, :, sl] = jnp.where(lo, halves[0], halves[1]).astype(BF16)


def _na_step_variant(i, n_steps):
    return jnp.where(i == 0, 0, jnp.where(i == n_steps - 1, 2, 1))


def _na_attention(na, na_ctx, bias):
    b, l, _ = na.shape
    n_ctx = na_ctx.shape[1]
    nblk = l // NA_QBLOCK
    n_steps = nblk // NA_SUB
    tq = NA_SUB * NA_QBLOCK
    n_loc = NA_STEP_BLOCKS * NA_QBLOCK
    n_keys = n_loc + n_ctx

    def win(col, j):
        return pl.BlockSpec((1, NA_QBLOCK, 512), lambda bi, i: (
            bi, jnp.clip(NA_SUB * i - 2, 0, nblk - NA_STEP_BLOCKS) + j, col))

    in_specs = ([pl.BlockSpec((1, tq, 512), lambda bi, i: (bi, i, 0))]
                + [win(1, j) for j in range(NA_STEP_BLOCKS)]
                + [win(2, j) for j in range(NA_STEP_BLOCKS)]
                + [pl.BlockSpec((1, n_ctx, 512), lambda bi, i: (bi, 0, 1)),
                   pl.BlockSpec((1, n_ctx, 512), lambda bi, i: (bi, 0, 2)),
                   pl.BlockSpec((1, NA_HEADS, tq, n_loc),
                                lambda bi, i: (_na_step_variant(i, n_steps), 0, 0, 0))])
    args = [na] * (1 + 2 * NA_STEP_BLOCKS) + [na_ctx, na_ctx, bias]
    return pl.pallas_call(
        functools.partial(_na_kernel, n_ctx=n_ctx),
        grid=(b, n_steps),
        in_specs=in_specs,
        out_specs=pl.BlockSpec((1, tq, 512), lambda bi, i: (bi, i, 0)),
        out_shape=jax.ShapeDtypeStruct((b, l, 512), BF16),
        scratch_shapes=[pltpu.VMEM((NA_HEADS, tq, n_keys), F32),
                        pltpu.VMEM((NA_HEADS, tq, n_keys), BF16),
                        pltpu.VMEM((NA_HEADS, tq, LANES), F32)],
        compiler_params=_cparams(("arbitrary", "arbitrary")),
        name="neighbourhood_attention",
    )(*args)


def _na_bias_plan(rows):
    nblk = rows * GRID_W // NA_QBLOCK
    kh = min(NA_KH, rows)
    q_rows = NA_QBLOCK // GRID_W
    k_rows = NA_WIN_BLOCKS * q_rows
    plan = []
    for i in (0, 1, 2, nblk - 2, nblk - 1):
        kb = min(max(i - 2, 0), nblk - NA_WIN_BLOCKS)
        per_q = []
        for rr in range(q_rows):
            r = q_rows * i + rr
            r0 = min(max(r - kh // 2, 0), rows - kh)
            per_q.append([(q_rows * kb + kj - r + NA_KH - 1) if r0 <= q_rows * kb + kj < r0 + kh else None
                          for kj in range(k_rows)])
        plan.append(per_q)
    return plan


def _na_bias(rpb, plan):
    c = np.arange(GRID_W)
    c0 = np.clip(c - NA_KW // 2, 0, GRID_W - NA_KW)
    kc = np.arange(GRID_W)
    ok = (kc[None, :] >= c0[:, None]) & (kc[None, :] < c0[:, None] + NA_KW)
    idx = np.where(ok, kc[None, :] - c[:, None] + NA_KW - 1, 0).astype(np.int32)
    cols = jnp.take(rpb.astype(F32), jnp.asarray(idx.reshape(-1)), axis=2)
    cols = cols.reshape(NA_HEADS, 2 * NA_KH - 1, GRID_W, GRID_W)
    cols = jnp.where(jnp.asarray(ok)[None, None], cols, NEG_BIG)
    outside = jnp.full((NA_HEADS, GRID_W, GRID_W), NEG_BIG, F32)
    variants = []
    for per_q in plan:
        q_parts = [jnp.concatenate([outside if dr is None else cols[:, dr] for dr in per_k], axis=-1)
                   for per_k in per_q]
        variants.append(jnp.concatenate(q_parts, axis=1))

    hidden = jnp.full((NA_HEADS, NA_QBLOCK, NA_QBLOCK), NEG_BIG, F32)

    def place(blk, off):
        return jnp.concatenate([hidden] * off + [blk] + [hidden] * (1 - off), axis=-1)

    steps = [(0, 0, 1, 0), (2, 0, 2, 1), (3, 1, 4, 1)]
    return jnp.stack([jnp.concatenate([place(variants[va], oa), place(variants[vb], ob)], axis=1)
                      for va, oa, vb, ob in steps])


def _latent_attn_kernel(q_ref, k_ref, v_ref, o_ref, s_scr, p_scr, al_scr, m_scr, acc_scr, *, tk):
    n_k = k_ref.shape[1] // tk
    n_heads = v_ref.shape[2] // LANES
    n_lane_blocks = tk // LANES
    assert n_k % 2 == 1, "the two-slot hand-over below assumes an odd number of key tiles"

    def key_rows(j):
        return pl.ds(pl.multiple_of(j * tk, tk), tk)

    def scores(j, slot):
        for h in range(n_heads):
            sl = slice(LANES * h, LANES * (h + 1))
            s_scr[slot, h] = _dot_nt(q_ref[0, :, sl], k_ref[0, key_rows(j), sl])

    def softmax(slot):
        for h in range(n_heads):
            part = s_scr[slot, h, :, 0:LANES]
            for c in range(1, n_lane_blocks):
                part = jnp.maximum(part, s_scr[slot, h, :, LANES * c:LANES * (c + 1)])
            m_old = m_scr[h]
            m_new = jnp.maximum(m_old, jnp.max(part, axis=-1, keepdims=True))
            al_scr[slot, h] = jnp.exp2(m_old - m_new)
            m_scr[h] = m_new
            for c in range(n_lane_blocks):
                cs = slice(LANES * c, LANES * (c + 1))
                p_scr[slot, h, :, cs] = jnp.exp2(s_scr[slot, h, :, cs] - m_new).astype(BF16)

    def values(j, slot):
        for h in range(n_heads):
            sl = slice(LANES * h, LANES * (h + 1))
            acc_scr[h] = al_scr[slot, h] * acc_scr[h] + _dot(p_scr[slot, h], v_ref[0, key_rows(j), sl])

    m_scr[...] = jnp.full(m_scr.shape, -jnp.inf, F32)
    acc_scr[...] = jnp.zeros(acc_scr.shape, F32)
    p_scr[1] = jnp.zeros(p_scr.shape[1:], BF16)
    al_scr[1] = jnp.ones(al_scr.shape[1:], F32)
    scores(0, 0)

    def body(i, carry):
        j = 2 * i
        scores(j + 1, 1)
        softmax(0)
        values(jnp.maximum(j - 1, 0), 1)
        scores(j + 2, 0)
        softmax(1)
        values(j, 0)
        return carry

    lax.fori_loop(0, (n_k - 1) // 2, body, 0)
    softmax(0)
    if n_k > 1:
        values(n_k - 2, 1)
    values(n_k - 1, 0)
    outs = []
    for h in range(n_heads):
        a = acc_scr[h]
        outs.append(a[:, 0:HEAD] / a[:, HEAD:HEAD + 1])
    o_ref[0] = jnp.concatenate(outs, axis=1).astype(BF16)


def _latent_attention(q, k, v, *, tq, tk, name):
    b, lq, w = q.shape
    lk = k.shape[1]
    n_heads = w // LANES
    resident = lambda: pl.BlockSpec((1, lk, w), lambda bi, i: (bi, 0, 0), pipeline_mode=pl.Buffered(1))
    return pl.pallas_call(
        functools.partial(_latent_attn_kernel, tk=tk),
        grid=(b, lq // tq),
        in_specs=[pl.BlockSpec((1, tq, w), lambda bi, i: (bi, i, 0)), resident(), resident()],
        out_specs=pl.BlockSpec((1, tq, n_heads * HEAD), lambda bi, i: (bi, i, 0)),
        out_shape=jax.ShapeDtypeStruct((b, lq, n_heads * HEAD), BF16),
        scratch_shapes=[pltpu.VMEM((2, n_heads, tq, tk), F32), pltpu.VMEM((2, n_heads, tq, tk), BF16),
                        pltpu.VMEM((2, n_heads, tq, LANES), F32),
                        pltpu.VMEM((n_heads, tq, LANES), F32), pltpu.VMEM((n_heads, tq, LANES), F32)],
        compiler_params=_cparams(("arbitrary", "arbitrary")),
        name=name,
    )(q, k, v)


def _pair_attn_kernel(q_ref, k_ref, v_ref, o_ref):
    tq = q_ref.shape[1]
    lo = _low_half((tq, LANES))
    for p in range(v_ref.shape[2] // LANES):
        sl = slice(LANES * p, LANES * (p + 1))
        qp, kp, vp = q_ref[0, :, sl], k_ref[0, :, sl], v_ref[0, :, sl]
        halves = []
        for half in range(2):
            qm = jnp.where(lo, qp, 0) if half == 0 else jnp.where(lo, 0, qp)
            s = _dot_nt(qm, kp)
            e = jnp.exp(s - jnp.max(s, axis=-1, keepdims=True))
            halves.append(_dot(e.astype(BF16), vp) / jnp.sum(e, axis=-1, keepdims=True))
        o_ref[0, :, sl] = jnp.where(lo, halves[0], halves[1]).astype(BF16)


def _pair_attention(q, k, v, *, tq, name):
    b, lq, w = q.shape
    lk = k.shape[1]
    return pl.pallas_call(
        _pair_attn_kernel,
        grid=(b, lq // tq),
        in_specs=[pl.BlockSpec((1, tq, w), lambda bi, i: (bi, i, 0)),
                  pl.BlockSpec((1, lk, w), lambda bi, i: (bi, 0, 0)),
                  pl.BlockSpec((1, lk, w), lambda bi, i: (bi, 0, 0))],
        out_specs=pl.BlockSpec((1, tq, w), lambda bi, i: (bi, i, 0)),
        out_shape=jax.ShapeDtypeStruct((b, lq, w), BF16),
        compiler_params=_cparams(("arbitrary", "arbitrary")),
        name=name,
    )(q, k, v)


def _gates(logits, bias):
    scores = _sigmoid(logits)
    sel = scores + bias
    lane = lax.broadcasted_iota(jnp.int32, logits.shape, 1)
    e4 = lane & (EXPERTS_PER_GROUP - 1)
    grp = (lane >> 2) & (N_GROUPS - 1)
    one, zero = jnp.float32(1.0), jnp.float32(0.0)

    rank = jnp.zeros(logits.shape, F32)
    for d in range(1, EXPERTS_PER_GROUP):
        other = _win_roll(sel, d, EXPERTS_PER_GROUP)
        other_first = ((e4 + d) & (EXPERTS_PER_GROUP - 1)) < e4
        ahead = jnp.logical_or(other > sel, jnp.logical_and(other == sel, other_first))
        rank = rank + jnp.where(ahead, one, zero)
    top2 = rank < float(TOP_K)

    def group_sum(a):
        tot = a
        for d in range(1, EXPERTS_PER_GROUP):
            tot = tot + _win_roll(a, d, EXPERTS_PER_GROUP)
        return tot

    gscore = group_sum(jnp.where(top2, sel, zero))
    beaten = jnp.zeros(logits.shape, F32)
    for d in range(1, N_GROUPS):
        other = _win_roll(gscore, d * EXPERTS_PER_GROUP, N_EXPERTS)
        other_first = ((grp + d) & (N_GROUPS - 1)) < grp
        ahead = jnp.logical_or(other > gscore, jnp.logical_and(other == gscore, other_first))
        beaten = beaten + jnp.where(ahead, one, zero)
    chosen = jnp.logical_and(jnp.logical_and(beaten < 0.5, top2), lane < N_EXPERTS)
    w = jnp.where(chosen, scores, zero)
    den = group_sum(w)
    return jnp.where(chosen, w / den, zero)


def _outproj_kernel(ret_ref, na_ref, mla_ref, h_ref, w_ref, g1_ref, sh_ref, sc_ref, n2_ref,
                    rw_ref, hn_ref, u_ref, logit_ref):
    mix = (_dot(ret_ref[0], w_ref[0:256, :]) + _dot(na_ref[0], w_ref[256:768, :])
           + _dot(mla_ref[0], w_ref[768:1024, :]))
    hn = h_ref[0] + g1_ref[0] * mix
    hn_ref[0] = hn
    u = _rms(hn, n2_ref[...]) * (1.0 + sc_ref[0]) + sh_ref[0]
    u_hi = u.astype(BF16)
    u_ref[0] = u_hi
    tm = u.shape[0]
    u_lo = (u - u_hi.astype(F32)).astype(BF16)
    both = _dot(jnp.concatenate([u_hi, u_lo], axis=0), rw_ref[...])
    logits = both[:tm, :LANES] + both[:tm, LANES:] + both[tm:, :LANES]
    logit_ref[0] = logits


def _out_projection(ret_o, na_o, mla_o, h, w_out, g1, sh2, sc2, n2, rw, tm):
    b, l, d = h.shape
    per_batch = g1.shape[0] > 1
    mod_map = (lambda i, j: (i, 0, 0)) if per_batch else (lambda i, j: (0, 0, 0))
    const2 = lambda i, j: (0, 0)
    tok = lambda w: pl.BlockSpec((1, tm, w), lambda i, j: (i, j, 0))
    mod = pl.BlockSpec((1, 1, d), mod_map)
    return pl.pallas_call(
        _outproj_kernel,
        grid=(b, l // tm),
        in_specs=[tok(256), tok(512), tok(256), tok(d),
                  pl.BlockSpec(w_out.shape, const2), mod, mod, mod,
                  pl.BlockSpec((1, d), const2),
                  pl.BlockSpec(rw.shape, const2)],
        out_specs=[tok(d), tok(d), tok(LANES)],
        out_shape=[jax.ShapeDtypeStruct((b, l, d), F32),
                   jax.ShapeDtypeStruct((b, l, d), BF16),
                   jax.ShapeDtypeStruct((b, l, LANES), F32)],
        compiler_params=_cparams(("arbitrary", "arbitrary")),
        name="out_projection",
    )(ret_o, na_o, mla_o, h, w_out, g1, sh2, sc2, n2, rw)


def _moe_kernel(u_ref, logit_ref, logit_next_ref, rb_ref, h_ref, w13_ref, w2_ref, g2_ref, fg_ref,
                o_ref, gate_scr, *, final_norm):
    u = u_ref[...]

    @pl.when(pl.program_id(0) == 0)
    def _():
        gate_scr[...] = _gates(logit_ref[...], rb_ref[...])

    gates = gate_scr[...]
    gates_next = _gates(logit_next_ref[...], rb_ref[...])
    lane = lax.broadcasted_iota(jnp.int32, gates.shape, 1)
    y = None
    for e in range(N_EXPERTS):
        a = _dot(u, w13_ref[e])
        a1, a3 = a[:, :EXPERT_FF], a[:, EXPERT_FF:]
        gcol = jnp.sum(jnp.where(lane == e, gates, 0.0), axis=-1, keepdims=True)
        hdn = ((a1 * _sigmoid(a1)) * a3 * gcol).astype(BF16)
        ye = _dot(hdn, w2_ref[e])
        y = ye if y is None else y + ye
    hn = h_ref[...] + g2_ref[0] * y
    if final_norm:
        hn = _rms(hn, fg_ref[...])
    o_ref[...] = hn
    gate_scr[...] = gates_next


def _moe(u, logits, rb, h, w13, w2, g2, fg, tokens_per_mod, tm, final_norm):
    t, d = h.shape
    per_batch = g2.shape[0] > 1
    tiles_per_mod = tokens_per_mod // tm
    n_tiles = t // tm
    mod_map = (lambda i: (i // tiles_per_mod, 0, 0)) if per_batch else (lambda i: (0, 0, 0))
    tok = lambda w: pl.BlockSpec((tm, w), lambda i: (i, 0))
    resident = lambda a: pl.BlockSpec(a.shape, lambda i: (0, 0, 0), pipeline_mode=pl.Buffered(1))
    return pl.pallas_call(
        functools.partial(_moe_kernel, final_norm=final_norm),
        grid=(n_tiles,),
        in_specs=[tok(d), tok(LANES),
                  pl.BlockSpec((tm, LANES), lambda i: (jnp.minimum(i + 1, n_tiles - 1), 0)),
                  pl.BlockSpec(rb.shape, lambda i: (0, 0)), tok(d),
                  resident(w13), resident(w2),
                  pl.BlockSpec((1, 1, d), mod_map),
                  pl.BlockSpec((1, d), lambda i: (0, 0))],
        out_specs=tok(d),
        out_shape=jax.ShapeDtypeStruct((t, d), F32),
        scratch_shapes=[pltpu.VMEM((tm, LANES), F32)],
        compiler_params=_cparams(("arbitrary",)),
        name="mixture_of_experts",
    )(u, logits, logits, rb, h, w13, w2, g2, fg)


def _rope_tables(seq, positional):
    if not positional:
        ones = jnp.ones((seq, LANES), F32)
        zeros = jnp.zeros((seq, LANES), F32)
        return ones, zeros, ones, zeros
    t = jnp.arange(seq)
    pos_r = (t // GRID_W).astype(F32)[:, None]
    pos_c = (t % GRID_W).astype(F32)[:, None]

    def axial(width):
        h = width // 2
        half = h // 2
        inv = ROPE_BASE ** (-jnp.arange(half, dtype=F32) / half)
        inv2 = jnp.concatenate([inv, inv])[None, :]
        ang = jnp.concatenate([pos_r * inv2, pos_c * inv2], axis=1)
        sign = jnp.tile(jnp.concatenate([-jnp.ones(half), jnp.ones(half)]), 2)[None, :]
        return jnp.cos(ang), jnp.sin(ang) * sign

    rc, rs = axial(RET_DK)
    ra = jnp.concatenate([rc, rc], axis=1)
    rb = jnp.concatenate([rs, rs], axis=1)
    mc, ms = axial(MLA_ROPE)
    ma = jnp.concatenate([jnp.ones((seq, MLA_NOPE), F32), mc, jnp.zeros((seq, 32), F32)], axis=1)
    mb = jnp.concatenate([jnp.zeros((seq, MLA_NOPE), F32), ms, jnp.zeros((seq, 32), F32)], axis=1)
    return ra, rb, ma, mb


def _pad_heads(w, heads, width):
    r = w.shape[0]
    w = w.reshape(r, heads, width)
    return jnp.pad(w, ((0, 0), (0, 0), (0, LANES - width))).reshape(r, heads * LANES)


def _layer_weights(w_in, w_uq, w_ukv):
    d = w_in.shape[0]
    kpe = w_in[:, _C_KPE:_C_KPE + MLA_ROPE]
    blk = jnp.concatenate([jnp.zeros((d, MLA_NOPE), w_in.dtype), kpe,
                           jnp.zeros((d, LANES - MLA_NOPE - MLA_ROPE), w_in.dtype)], axis=1)
    w_ext = jnp.concatenate([w_in[:, :_C_KPE], blk], axis=1).astype(BF16)
    wuq = _pad_heads(w_uq, MLA_HEADS, MLA_NOPE + MLA_ROPE).astype(BF16)
    ukv = w_ukv.reshape(w_ukv.shape[0], MLA_HEADS, MLA_NOPE + MLA_VD)
    wuk = _pad_heads(ukv[:, :, :MLA_NOPE].reshape(w_ukv.shape[0], -1), MLA_HEADS, MLA_NOPE).astype(BF16)
    wuv = _pad_heads(ukv[:, :, MLA_NOPE:].reshape(w_ukv.shape[0], -1), MLA_HEADS, MLA_VD).astype(BF16)
    return w_ext, wuq, wuk, wuv


def _latent_key_tile(n_keys):
    for tk in range(768, 0, -LANES):
        if n_keys % tk == 0 and (n_keys // tk) % 2 == 1:
            return tk
    raise ValueError(f"no odd tiling of {n_keys} keys")


def kernel(x, c, ctx, c_ctx, w_mod, b_mod, norm1_g, norm2_g, w_in, ret_decay_f, ret_decay_b, na_rpb,
           mla_q_norm, mla_kv_norm, w_uq, w_ukv, w_out, router_w, router_b, w1, w3, w2, final_norm_g):
    b, s, d = x.shape
    lc = ctx.shape[1]
    depth = w_mod.shape[0]
    rows = s // GRID_W
    tm_lat = 512
    tm_ctx = min(256, lc)

    n_mod = -(-(b + 1) // 8) * 8
    cv = jnp.concatenate([c, c_ctx[None, :], jnp.zeros((n_mod - b - 1, d), F32)], axis=0)
    mods = _modulation(cv, w_mod, b_mod).reshape(depth, n_mod, 6, d)

    tabs_lat = _rope_tables(s, True)
    tabs_ctx = _rope_tables(lc, False)
    na_plan = _na_bias_plan(rows)
    rw_f32 = jnp.pad(router_w.astype(F32), ((0, 0), (0, LANES - N_EXPERTS)))
    rw_hi = rw_f32.astype(BF16)
    rw = jnp.concatenate([rw_hi, (rw_f32 - rw_hi.astype(F32)).astype(BF16)], axis=1)
    rb = jnp.pad(router_b.astype(F32), (0, LANES - N_EXPERTS)).reshape(1, LANES)
    zero_state = (jnp.zeros((b, 2 * LANES, LANES), F32),) * 2

    h, hc = x, ctx
    for l in range(depth):
        last = l == depth - 1
        m_lat = [mods[l, :b, i][:, None, :] for i in range(6)]
        m_ctx = [mods[l, b:b + 1, i][:, None, :] for i in range(6)]
        w_ext, wuq, wuk, wuv = _layer_weights(w_in[l], w_uq[l], w_ukv[l])
        n1 = norm1_g[l].reshape(1, d)
        n2 = norm2_g[l].reshape(1, d)
        qn = mla_q_norm[l].reshape(1, -1)
        kvn = mla_kv_norm[l].reshape(1, -1)
        wo = w_out[l].astype(BF16)
        w13 = jnp.concatenate([w1[l], w3[l]], axis=-1).astype(BF16)
        w2b = w2[l].astype(BF16)
        fg = final_norm_g.reshape(1, d)

        ret_c, na_c, mq_c, mk_c, mv_c = _in_projection(
            hc, m_ctx[0], m_ctx[1], n1, w_ext, qn, kvn, wuq, wuk, wuv, tabs_ctx, tm_ctx)
        ret_l, na_l, mq_l, mk_l, mv_l = _in_projection(
            h, m_lat[0], m_lat[1], n1, w_ext, qn, kvn, wuq, wuk, wuv, tabs_lat, tm_lat)

        rtabs = _retention_tables(ret_decay_f[l], ret_decay_b[l])
        ro_c, st = _retention(ret_c, rtabs, zero_state)
        ro_l, _ = _retention(ret_l, rtabs, st)

        na_o = _na_attention(na_l, na_c, _na_bias(na_rpb[l], na_plan))
        mla_o = _latent_attention(mq_l, jnp.concatenate([mk_l, mk_c], axis=1),
                                  jnp.concatenate([mv_l, mv_c], axis=1),
                                  tq=256, tk=_latent_key_tile(s + lc), name="latent_attention")

        hn, u2, logits = _out_projection(ro_l, na_o, mla_o, h, wo, m_lat[2], m_lat[3], m_lat[4],
                                         n2, rw, tm_lat)
        h = _moe(u2.reshape(b * s, d), logits.reshape(b * s, LANES), rb, hn.reshape(b * s, d),
                 w13, w2b, m_lat[5], fg, s, 512, last).reshape(b, s, d)

        if not last:
            nac_o = _pair_attention(na_c[:, :, 0:512], na_c[:, :, 512:1024], na_c[:, :, 1024:1536],
                                    tq=tm_ctx, name="context_na_attention")
            mlac_o = _latent_attention(mq_c, mk_c, mv_c, tq=tm_ctx, tk=lc,
                                       name="context_latent_attention")
            hcn, uc2, logits_c = _out_projection(ro_c, nac_o, mlac_o, hc, wo, m_ctx[2], m_ctx[3],
                                                 m_ctx[4], n2, rw, tm_ctx)
            hc = _moe(uc2.reshape(b * lc, d), logits_c.reshape(b * lc, LANES), rb,
                      hcn.reshape(b * lc, d), w13, w2b, m_ctx[5], fg, lc, min(512, lc),
                      False).reshape(b, lc, d)
    return h
```

```python
import functools

import numpy as np
import jax
import jax.numpy as jnp
from jax import lax
from jax.experimental import pallas as pl
from jax.experimental.pallas import tpu as pltpu

F32 = jnp.float32
BF16 = jnp.bfloat16

GRID_W = 64
EPS = 1e-6
ROPE_BASE = 10000.0
RET_HEADS, RET_DK, RET_DV, RET_CHUNK = 4, 64, 64, 128
NA_HEADS, NA_HD, NA_KH, NA_KW, NA_QBLOCK = 8, 64, 8, 16, 128
MLA_HEADS, MLA_NOPE, MLA_ROPE, MLA_VD = 4, 64, 32, 64
MLA_Q_RANK, MLA_KV_RANK = 384, 256
N_EXPERTS, N_GROUPS, TOP_K, EXPERT_FF = 16, 4, 2, 256
EXPERTS_PER_GROUP = N_EXPERTS // N_GROUPS

LANES = 128
HEAD = 64
NA_WIN_BLOCKS = 5
MLA_Q_TILES = 4
MLA_MAX_KEY_TILE = 1024
NA_SUB = 2
NA_STEP_BLOCKS = NA_WIN_BLOCKS + NA_SUB - 1
RET_GROUP = 8
NEG_BIG = -1e30
VMEM_LIMIT = 48 * 1024 * 1024

_C_RET, _C_NA, _C_CQ, _C_CKV, _C_KPE = 0, 1024, 2560, 2944, 3200
_W_EXT = _C_KPE + LANES
MLA_Q_SCALE = (MLA_NOPE + MLA_ROPE) ** -0.5 * float(np.log2(np.e))


def _cparams(sem):
    return pltpu.CompilerParams(dimension_semantics=sem, vmem_limit_bytes=VMEM_LIMIT)


def _dot(a, b):
    return jnp.dot(a, b, preferred_element_type=F32)


def _dot_nt(a, b):
    return lax.dot_general(a, b, (((1,), (1,)), ((), ())), preferred_element_type=F32)


def _sigmoid(x):
    return 1.0 / (1.0 + jnp.exp(-x))


def _win_roll(x, s, w):
    n = x.shape[-1]
    ax = x.ndim - 1
    lane = lax.broadcasted_iota(jnp.int32, x.shape, ax)
    fwd = pltpu.roll(x, n - s, ax)
    bwd = pltpu.roll(x, w - s, ax)
    return jnp.where((lane & (w - 1)) + s < w, fwd, bwd)


def _low_half(shape):
    lane = lax.broadcasted_iota(jnp.int32, shape, len(shape) - 1)
    return (lane & (LANES - 1)) < HEAD


def _mod_kernel(cv_ref, w_ref, b_ref, o_ref):
    cv = cv_ref[...]
    a = cv * _sigmoid(cv)
    o_ref[0] = jnp.dot(a, w_ref[0], preferred_element_type=F32,
                       precision=lax.Precision.HIGHEST) + b_ref[0]


def _modulation(cv, w_mod, b_mod):
    depth, d, n6 = w_mod.shape
    rows = cv.shape[0]
    tn = 1536
    return pl.pallas_call(
        _mod_kernel,
        grid=(depth, n6 // tn),
        in_specs=[pl.BlockSpec((rows, d), lambda l, j: (0, 0)),
                  pl.BlockSpec((1, d, tn), lambda l, j: (l, 0, j)),
                  pl.BlockSpec((1, 1, tn), lambda l, j: (l, 0, j))],
        out_specs=pl.BlockSpec((1, rows, tn), lambda l, j: (l, 0, j)),
        out_shape=jax.ShapeDtypeStruct((depth, rows, n6), F32),
        compiler_params=_cparams(("arbitrary", "arbitrary")),
        name="modulation",
    )(cv, w_mod, b_mod.reshape(depth, 1, n6))


def _rms(x, g):
    return x * lax.rsqrt(jnp.mean(x * x, axis=-1, keepdims=True) + EPS) * g


def _inproj_kernel(h_ref, sh_ref, sc_ref, g_ref, w_ref, qn_ref, kvn_ref, wuq_ref, wuk_ref, wuv_ref,
                   ra_ref, rb_ref, ma_ref, mb_ref,
                   ret_ref, na_ref, q_ref, k_ref, v_ref):
    x = h_ref[0]
    u = _rms(x, g_ref[...]) * (1.0 + sc_ref[0]) + sh_ref[0]
    ub = u.astype(BF16)

    r = _dot(ub, w_ref[:, _C_RET:_C_NA])
    qk = r[:, :512]
    ra = jnp.concatenate([ra_ref[...]] * 4, axis=1)
    rb = jnp.concatenate([rb_ref[...]] * 4, axis=1)
    qk = qk * ra + _win_roll(qk, 16, 32) * rb
    ret_ref[0, :, 0:256] = qk[:, :256] * (RET_DK ** -0.5)
    ret_ref[0, :, 256:512] = qk[:, 256:]
    ret_ref[0, :, 512:1024] = r[:, 512:]

    n = _dot(ub, w_ref[:, _C_NA:_C_CQ])
    na_ref[0, :, 0:512] = (n[:, :512] * (NA_HD ** -0.5)).astype(BF16)
    na_ref[0, :, 512:1536] = n[:, 512:].astype(BF16)

    ma = jnp.concatenate([ma_ref[...]] * 4, axis=1)
    mb = jnp.concatenate([mb_ref[...]] * 4, axis=1)
    cq = _rms(_dot(ub, w_ref[:, _C_CQ:_C_CKV]), qn_ref[...]).astype(BF16)
    q = _dot(cq, wuq_ref[...])
    q = q * ma + _win_roll(q, 8, 16) * mb
    q_ref[0] = (q * MLA_Q_SCALE).astype(BF16)
    ckv = _rms(_dot(ub, w_ref[:, _C_CKV:_C_KPE]), kvn_ref[...]).astype(BF16)
    kp = _dot(ub, w_ref[:, _C_KPE:_W_EXT])
    kp = kp * ma_ref[...] + _win_roll(kp, 8, 16) * mb_ref[...]
    kp = jnp.concatenate([kp] * MLA_HEADS, axis=1)
    k_ref[0] = (_dot(ckv, wuk_ref[...]) + kp).astype(BF16)
    v = _dot(ckv, wuv_ref[...])
    lane = lax.broadcasted_iota(jnp.int32, v.shape, 1)
    v_ref[0] = jnp.where((lane & (LANES - 1)) == HEAD, 1.0, v).astype(BF16)


def _in_projection(h, sh, sc, g, w_ext, qn, kvn, wuq, wuk, wuv, tabs, tm):
    b, l, d = h.shape
    per_batch = sh.shape[0] > 1
    mod_map = (lambda i, j: (i, 0, 0)) if per_batch else (lambda i, j: (0, 0, 0))
    const2 = lambda i, j: (0, 0)
    tab_spec = pl.BlockSpec((tm, LANES), lambda i, j: (j, 0))
    tok = lambda w: pl.BlockSpec((1, tm, w), lambda i, j: (i, j, 0))
    return pl.pallas_call(
        _inproj_kernel,
        grid=(b, l // tm),
        in_specs=[tok(d),
                  pl.BlockSpec((1, 1, d), mod_map), pl.BlockSpec((1, 1, d), mod_map),
                  pl.BlockSpec((1, d), const2),
                  pl.BlockSpec(w_ext.shape, const2),
                  pl.BlockSpec(qn.shape, const2), pl.BlockSpec(kvn.shape, const2),
                  pl.BlockSpec(wuq.shape, const2), pl.BlockSpec(wuk.shape, const2),
                  pl.BlockSpec(wuv.shape, const2),
                  tab_spec, tab_spec, tab_spec, tab_spec],
        out_specs=[tok(1024), tok(1536), tok(512), tok(512), tok(512)],
        out_shape=[jax.ShapeDtypeStruct((b, l, 1024), F32),
                   jax.ShapeDtypeStruct((b, l, 1536), BF16),
                   jax.ShapeDtypeStruct((b, l, 512), BF16),
                   jax.ShapeDtypeStruct((b, l, 512), BF16),
                   jax.ShapeDtypeStruct((b, l, 512), BF16)],
        compiler_params=_cparams(("arbitrary", "arbitrary")),
        name="in_projection",
    )(h, sh, sc, g, w_ext, qn, kvn, wuq, wuk, wuv, *tabs)


def _retention_kernel(*refs, backward, group):
    if backward:
        (q_ref, k_ref, v_ref, g_ref, of_ref, dm_ref, qw_ref, kw_ref, gc_ref, s0_ref,
         o_ref, sfin_ref, state) = refs
    else:
        q_ref, k_ref, v_ref, dm_ref, qw_ref, kw_ref, gc_ref, s0_ref, o_ref, sfin_ref, state = refs
    step = pl.program_id(1)
    c = RET_CHUNK
    n_pairs = RET_HEADS // 2

    @pl.when(step == 0)
    def _():
        state[...] = s0_ref[0]

    lo = _low_half((c, LANES))
    row_lo = lax.broadcasted_iota(jnp.int32, (LANES, LANES), 0) < HEAD
    same_head = row_lo == _low_half((LANES, LANES))
    qw, kw = qw_ref[...], kw_ref[...]

    intra, kv, qd = {}, {}, {}
    for ci in range(group):
        rows = slice(c * ci, c * (ci + 1))
        for p in range(n_pairs):
            sl = slice(LANES * p, LANES * (p + 1))
            qp, kp = q_ref[0, rows, sl], k_ref[0, rows, sl]
            kb, vb = kp.astype(BF16), v_ref[0, rows, sl].astype(BF16)
            qa = jnp.where(lo, qp, 0.0).astype(BF16)
            qb = jnp.where(lo, 0.0, qp).astype(BF16)
            sa = (_dot_nt(qa, kb) * dm_ref[2 * p]).astype(BF16)
            sb = (_dot_nt(qb, kb) * dm_ref[2 * p + 1]).astype(BF16)
            intra[ci, p] = jnp.where(lo, _dot(sa, vb), _dot(sb, vb))
            kv[ci, p] = jnp.where(same_head, _dot((kp * kw[:, sl]).T.astype(BF16), vb), 0.0)
            qd[ci, p] = (qp * qw[:, sl]).astype(BF16)

    order = range(group - 1, -1, -1) if backward else range(group)
    out = {}
    for p in range(n_pairs):
        sl = slice(LANES * p, LANES * (p + 1))
        sp = state[sl, :]
        gc = gc_ref[sl, :]
        for ci in order:
            out[ci, p] = intra[ci, p] + _dot(qd[ci, p], sp.astype(BF16))
            sp = gc * sp + kv[ci, p]
        state[sl, :] = sp

    for ci in range(group):
        rows = slice(c * ci, c * (ci + 1))
        o = jnp.concatenate([out[ci, p] for p in range(n_pairs)], axis=1)
        if not backward:
            o_ref[0, rows, :] = o
            continue
        tot = of_ref[0, rows, :] + o
        g = g_ref[0, rows, :]
        normed = []
        for p in range(n_pairs):
            t = tot[:, LANES * p:LANES * (p + 1)]

            def head_mean(a):
                s_lo = jnp.sum(jnp.where(lo, a, 0.0), axis=-1, keepdims=True)
                s_hi = jnp.sum(jnp.where(lo, 0.0, a), axis=-1, keepdims=True)
                return jnp.where(lo, s_lo, s_hi) * (1.0 / HEAD)

            dlt = t - head_mean(t)
            normed.append(dlt * lax.rsqrt(head_mean(dlt * dlt) + EPS))
        on = jnp.concatenate(normed, axis=1)
        o_ref[0, rows, :] = (on * (g * _sigmoid(g))).astype(BF16)

    @pl.when(step == pl.num_programs(1) - 1)
    def _():
        sfin_ref[0] = state[...]


def _retention_sweep(ret, o_fwd, tabs, s0, *, backward):
    b, l, _ = ret.shape
    c = RET_CHUNK
    group = min(RET_GROUP, l // c)
    n = l // (c * group)
    gidx = (lambda s: n - 1 - s) if backward else (lambda s: s)
    tok = lambda col: pl.BlockSpec((1, c * group, 256), lambda i, s: (i, gidx(s), col))
    const = lambda a: pl.BlockSpec(a.shape, lambda i, s: (0,) * a.ndim)
    state_spec = pl.BlockSpec((1, 2 * LANES, LANES), lambda i, s: (i, 0, 0))
    dm, qw, kw, gc = tabs
    if backward:
        in_specs = [tok(0), tok(1), tok(2), tok(3), tok(0)]
        args = [ret, ret, ret, ret, o_fwd]
    else:
        in_specs = [tok(0), tok(1), tok(2)]
        args = [ret, ret, ret]
    return pl.pallas_call(
        functools.partial(_retention_kernel, backward=backward, group=group),
        grid=(b, n),
        in_specs=in_specs + [const(dm), const(qw), const(kw), const(gc), state_spec],
        out_specs=[tok(0), state_spec],
        out_shape=[jax.ShapeDtypeStruct((b, l, 256), BF16 if backward else F32),
                   jax.ShapeDtypeStruct((b, 2 * LANES, LANES), F32)],
        scratch_shapes=[pltpu.VMEM((2 * LANES, LANES), F32)],
        compiler_params=_cparams(("arbitrary", "arbitrary")),
        name="retention_backward" if backward else "retention_forward",
    )(*args, dm, qw, kw, gc, s0)


def _retention(ret, tabs, s0):
    fwd_tabs = tuple(t[0] for t in tabs)
    bwd_tabs = tuple(t[1] for t in tabs)
    o_f, s_f = _retention_sweep(ret, None, fwd_tabs, s0[0], backward=False)
    o, s_b = _retention_sweep(ret, o_f, bwd_tabs, s0[1], backward=True)
    return o, (s_f, s_b)


def _retention_tables(decay_f, decay_b):
    c = RET_CHUNK
    pos = jnp.arange(c, dtype=F32)
    diff = pos[:, None] - pos[None, :]
    lg = jnp.stack([jax.nn.log_sigmoid(decay_f.astype(F32)), jax.nn.log_sigmoid(decay_b.astype(F32))])
    lgh = lg[:, :, None, None]
    dm_f = jnp.where(diff >= 0, jnp.exp(lgh[0] * jnp.maximum(diff, 0.0)), 0.0)
    dm_b = jnp.where(diff < 0, jnp.exp(lgh[1] * jnp.maximum(-diff, 0.0)), 0.0)
    dm = jnp.stack([dm_f, dm_b])
    lanes = lambda t: jnp.repeat(t, RET_DK, axis=-1)
    lgc = lg[:, None, :]
    qw = lanes(jnp.exp(lgc * jnp.stack([pos + 1.0, c - pos])[:, :, None]))
    kw = lanes(jnp.exp(lgc * jnp.stack([c - 1.0 - pos, pos])[:, :, None]))
    gch = jnp.exp(lg * c)
    gc = jnp.broadcast_to(jnp.repeat(gch, RET_DK, axis=-1)[:, :, None], (2, RET_HEADS * RET_DK, LANES))
    return dm, qw, kw, gc


def _na_kernel(*refs, n_ctx):
    w = NA_STEP_BLOCKS
    q_ref = refs[0]
    k_refs, v_refs = refs[1:1 + w], refs[1 + w:1 + 2 * w]
    kc_ref, vc_ref, bias_ref, o_ref, s_scr, p_scr, den_scr = refs[1 + 2 * w:]
    tq = q_ref.shape[1]
    n_loc = w * NA_QBLOCK
    n_lane_blocks = (n_loc + n_ctx) // LANES
    lo = _low_half((tq, LANES))
    for h in range(NA_HEADS):
        sl = slice(LANES * (h // 2), LANES * (h // 2 + 1))
        qp = q_ref[0, :, sl]
        qm = jnp.where(lo, qp, 0) if h % 2 == 0 else jnp.where(lo, 0, qp)
        kwin = jnp.concatenate([r[0, :, sl] for r in k_refs], axis=0)
        s_scr[h, :, 0:n_loc] = _dot_nt(qm, kwin) + bias_ref[0, h]
        s_scr[h, :, n_loc:] = _dot_nt(qm, kc_ref[0, :, sl])
    for h in range(NA_HEADS):
        part = s_scr[h, :, 0:LANES]
        for c in range(1, n_lane_blocks):
            part = jnp.maximum(part, s_scr[h, :, LANES * c:LANES * (c + 1)])
        m = jnp.max(part, axis=-1, keepdims=True)
        tot = None
        for c in range(n_lane_blocks):
            cs = slice(LANES * c, LANES * (c + 1))
            e = jnp.exp(s_scr[h, :, cs] - m)
            tot = e if tot is None else tot + e
            p_scr[h, :, cs] = e.astype(BF16)
        den_scr[h] = jnp.broadcast_to(jnp.sum(tot, axis=-1, keepdims=True), (tq, LANES))
    for p in range(NA_HEADS // 2):
        sl = slice(LANES * p, LANES * (p + 1))
        vwin = jnp.concatenate([r[0, :, sl] for r in v_refs], axis=0)
        halves = []
        for h in (2 * p, 2 * p + 1):
            o = _dot(p_scr[h, :, 0:n_loc], vwin) + _dot(p_scr[h, :, n_loc:], vc_ref[0, :, sl])
            halves.append(o / den_scr[h])
        o_ref[0, :, sl] = jnp.where(lo, halves[0], halves[1]).astype(BF16)


def _na_step_variant(i, n_steps):
    return jnp.where(i == 0, 0, jnp.where(i == n_steps - 1, 2, 1))


def _na_attention(na, na_ctx, bias):
    b, l, _ = na.shape
    n_ctx = na_ctx.shape[1]
    nblk = l // NA_QBLOCK
    n_steps = nblk // NA_SUB
    tq = NA_SUB * NA_QBLOCK
    n_loc = NA_STEP_BLOCKS * NA_QBLOCK
    n_keys = n_loc + n_ctx

    def win(col, j):
        return pl.BlockSpec((1, NA_QBLOCK, 512), lambda bi, i: (
            bi, jnp.clip(NA_SUB * i - 2, 0, nblk - NA_STEP_BLOCKS) + j, col))

    in_specs = ([pl.BlockSpec((1, tq, 512), lambda bi, i: (bi, i, 0))]
                + [win(1, j) for j in range(NA_STEP_BLOCKS)]
                + [win(2, j) for j in range(NA_STEP_BLOCKS)]
                + [pl.BlockSpec((1, n_ctx, 512), lambda bi, i: (bi, 0, 1)),
                   pl.BlockSpec((1, n_ctx, 512), lambda bi, i: (bi, 0, 2)),
                   pl.BlockSpec((1, NA_HEADS, tq, n_loc),
                                lambda bi, i: (_na_step_variant(i, n_steps), 0, 0, 0))])
    args = [na] * (1 + 2 * NA_STEP_BLOCKS) + [na_ctx, na_ctx, bias]
    return pl.pallas_call(
        functools.partial(_na_kernel, n_ctx=n_ctx),
        grid=(b, n_steps),
        in_specs=in_specs,
        out_specs=pl.BlockSpec((1, tq, 512), lambda bi, i: (bi, i, 0)),
        out_shape=jax.ShapeDtypeStruct((b, l, 512), BF16),
        scratch_shapes=[pltpu.VMEM((NA_HEADS, tq, n_keys), F32),
                        pltpu.VMEM((NA_HEADS, tq, n_keys), BF16),
                        pltpu.VMEM((NA_HEADS, tq, LANES), F32)],
        compiler_params=_cparams(("arbitrary", "arbitrary")),
        name="neighbourhood_attention",
    )(*args)


def _na_bias_plan(rows):
    nblk = rows * GRID_W // NA_QBLOCK
    kh = min(NA_KH, rows)
    q_rows = NA_QBLOCK // GRID_W
    k_rows = NA_WIN_BLOCKS * q_rows
    plan = []
    for i in (0, 1, 2, nblk - 2, nblk - 1):
        kb = min(max(i - 2, 0), nblk - NA_WIN_BLOCKS)
        per_q = []
        for rr in range(q_rows):
            r = q_rows * i + rr
            r0 = min(max(r - kh // 2, 0), rows - kh)
            per_q.append([(q_rows * kb + kj - r + NA_KH - 1) if r0 <= q_rows * kb + kj < r0 + kh else None
                          for kj in range(k_rows)])
        plan.append(per_q)
    return plan


def _na_bias(rpb, plan):
    c = np.arange(GRID_W)
    c0 = np.clip(c - NA_KW // 2, 0, GRID_W - NA_KW)
    kc = np.arange(GRID_W)
    ok = (kc[None, :] >= c0[:, None]) & (kc[None, :] < c0[:, None] + NA_KW)
    reach = GRID_W - NA_KW
    padded = jnp.pad(rpb.astype(F32), ((0, 0), (0, 0), (reach, reach)))
    cols = jnp.stack([padded[:, :, GRID_W - 1 - ci:2 * GRID_W - 1 - ci] for ci in range(GRID_W)], axis=2)
    cols = jnp.where(jnp.asarray(ok)[None, None], cols, NEG_BIG)
    outside = jnp.full((NA_HEADS, GRID_W, GRID_W), NEG_BIG, F32)
    variants = []
    for per_q in plan:
        q_parts = [jnp.concatenate([outside if dr is None else cols[:, dr] for dr in per_k], axis=-1)
                   for per_k in per_q]
        variants.append(jnp.concatenate(q_parts, axis=1))

    hidden = jnp.full((NA_HEADS, NA_QBLOCK, NA_QBLOCK), NEG_BIG, F32)

    def place(blk, off):
        return jnp.concatenate([hidden] * off + [blk] + [hidden] * (1 - off), axis=-1)

    steps = [(0, 0, 1, 0), (2, 0, 2, 1), (3, 1, 4, 1)]
    return jnp.stack([jnp.concatenate([place(variants[va], oa), place(variants[vb], ob)], axis=1)
                      for va, oa, vb, ob in steps])


def _latent_attn_kernel(*refs, tk, q_tiles, n_ctx):
    if n_ctx:
        q_ref, k_ref, v_ref, kc_ref, vc_ref, o_ref, s_scr, p_scr, al_scr, m_scr, acc_scr = refs
    else:
        q_ref, k_ref, v_ref, o_ref, s_scr, p_scr, al_scr, m_scr, acc_scr = refs
    n_k = k_ref.shape[1] // tk + (1 if n_ctx else 0)
    n_heads = v_ref.shape[2] // LANES
    tq = q_ref.shape[1] // q_tiles
    assert n_k % 2 == 1, "static slot parities below assume an odd number of key tiles"

    def is_ctx(j):
        return bool(n_ctx) and isinstance(j, int) and j == n_k - 1

    def key_rows(j):
        return pl.ds(pl.multiple_of(j * tk, tk), tk)

    def slot_of(a, j_parity):
        return (a * n_k + j_parity) % 2

    def scores(a, j, slot):
        for h in range(n_heads):
            sl = slice(LANES * h, LANES * (h + 1))
            qh = q_ref[0, tq * a:tq * (a + 1), sl]
            if is_ctx(j):
                s_scr[slot, h, :, 0:n_ctx] = _dot_nt(qh, kc_ref[0, :, sl])
            else:
                s_scr[slot, h] = _dot_nt(qh, k_ref[0, key_rows(j), sl])

    def softmax(a, slot, ctx_tile=False):
        n_lane_blocks = (n_ctx if ctx_tile else tk) // LANES
        for h in range(n_heads):
            part = s_scr[slot, h, :, 0:LANES]
            for c in range(1, n_lane_blocks):
                part = jnp.maximum(part, s_scr[slot, h, :, LANES * c:LANES * (c + 1)])
            m_old = m_scr[a, h]
            m_new = jnp.maximum(m_old, jnp.max(part, axis=-1, keepdims=True))
            al_scr[slot, h] = jnp.exp2(m_old - m_new)
            m_scr[a, h] = m_new
            for c in range(n_lane_blocks):
                cs = slice(LANES * c, LANES * (c + 1))
                p_scr[slot, h, :, cs] = jnp.exp2(s_scr[slot, h, :, cs] - m_new).astype(BF16)

    def values(a, j, slot):
        for h in range(n_heads):
            sl = slice(LANES * h, LANES * (h + 1))
            if is_ctx(j):
                pv = _dot(p_scr[slot, h, :, 0:n_ctx], vc_ref[0, :, sl])
            else:
                pv = _dot(p_scr[slot, h], v_ref[0, key_rows(j), sl])
            acc_scr[a, h] = al_scr[slot, h] * acc_scr[a, h] + pv

    def time_step(nxt, cur, prv):
        if nxt is not None:
            scores(*nxt)
        softmax(*cur)
        if prv is not None:
            values(*prv)

    m_scr[...] = jnp.full(m_scr.shape, -jnp.inf, F32)
    acc_scr[...] = jnp.zeros(acc_scr.shape, F32)
    scores(0, 0, 0)
    for a in range(q_tiles):
        s_even, s_odd = slot_of(a, 0), slot_of(a, 1)
        first_of_next = (a + 1, 0, slot_of(a + 1, 0)) if a + 1 < q_tiles else None
        last_of_prev = (a - 1, n_k - 1, slot_of(a - 1, 0)) if a > 0 else None
        if n_k == 1:
            time_step(first_of_next, (a, s_even), last_of_prev)
            continue
        time_step((a, 1, s_odd), (a, s_even), last_of_prev)

        def body(i, carry, a=a, s_even=s_even, s_odd=s_odd):
            j = 2 * i + 1
            time_step((a, j + 1, s_even), (a, s_odd), (a, j - 1, s_even))
            time_step((a, j + 2, s_odd), (a, s_even), (a, j, s_odd))
            return carry

        lax.fori_loop(0, (n_k - 3) // 2, body, 0)
        time_step((a, n_k - 1, s_even), (a, s_odd), (a, n_k - 3, s_even))
        time_step(first_of_next, (a, s_even, bool(n_ctx)), (a, n_k - 2, s_odd))
    values(q_tiles - 1, n_k - 1, slot_of(q_tiles - 1, 0))
    for a in range(q_tiles):
        outs = []
        for h in range(n_heads):
            acc = acc_scr[a, h]
            outs.append(acc[:, 0:HEAD] / acc[:, HEAD:HEAD + 1])
        o_ref[0, tq * a:tq * (a + 1), :] = jnp.concatenate(outs, axis=1).astype(BF16)


def _latent_attention(q, k, v, ctx_kv=None, *, tq, q_tiles, tk, name):
    b, lq, w = q.shape
    lk = k.shape[1]
    n_heads = w // LANES
    rows = tq * q_tiles
    n_ctx = 0 if ctx_kv is None else ctx_kv[0].shape[1]
    resident = lambda n: pl.BlockSpec((1, n, w), lambda bi, i: (bi, 0, 0), pipeline_mode=pl.Buffered(1))
    stat = pltpu.VMEM((q_tiles, n_heads, tq, LANES), F32)
    in_specs = [pl.BlockSpec((1, rows, w), lambda bi, i: (bi, i, 0)), resident(lk), resident(lk)]
    args = [q, k, v]
    if n_ctx:
        assert n_ctx <= tk and n_ctx % LANES == 0
        in_specs += [resident(n_ctx), resident(n_ctx)]
        args += list(ctx_kv)
    return pl.pallas_call(
        functools.partial(_latent_attn_kernel, tk=tk, q_tiles=q_tiles, n_ctx=n_ctx),
        grid=(b, lq // rows),
        in_specs=in_specs,
        out_specs=pl.BlockSpec((1, rows, n_heads * HEAD), lambda bi, i: (bi, i, 0)),
        out_shape=jax.ShapeDtypeStruct((b, lq, n_heads * HEAD), BF16),
        scratch_shapes=[pltpu.VMEM((2, n_heads, tq, tk), F32), pltpu.VMEM((2, n_heads, tq, tk), BF16),
                        pltpu.VMEM((2, n_heads, tq, LANES), F32), stat, stat],
        compiler_params=_cparams(("arbitrary", "arbitrary")),
        name=name,
    )(*args)


def _pair_attn_kernel(q_ref, k_ref, v_ref, o_ref):
    tq = q_ref.shape[1]
    lo = _low_half((tq, LANES))
    for p in range(v_ref.shape[2] // LANES):
        sl = slice(LANES * p, LANES * (p + 1))
        qp, kp, vp = q_ref[0, :, sl], k_ref[0, :, sl], v_ref[0, :, sl]
        halves = []
        for half in range(2):
            qm = jnp.where(lo, qp, 0) if half == 0 else jnp.where(lo, 0, qp)
            s = _dot_nt(qm, kp)
            e = jnp.exp(s - jnp.max(s, axis=-1, keepdims=True))
            halves.append(_dot(e.astype(BF16), vp) / jnp.sum(e, axis=-1, keepdims=True))
        o_ref[0, :, sl] = jnp.where(lo, halves[0], halves[1]).astype(BF16)


def _pair_attention(q, k, v, *, tq, name):
    b, lq, w = q.shape
    lk = k.shape[1]
    return pl.pallas_call(
        _pair_attn_kernel,
        grid=(b, lq // tq),
        in_specs=[pl.BlockSpec((1, tq, w), lambda bi, i: (bi, i, 0)),
                  pl.BlockSpec((1, lk, w), lambda bi, i: (bi, 0, 0)),
                  pl.BlockSpec((1, lk, w), lambda bi, i: (bi, 0, 0))],
        out_specs=pl.BlockSpec((1, tq, w), lambda bi, i: (bi, i, 0)),
        out_shape=jax.ShapeDtypeStruct((b, lq, w), BF16),
        compiler_params=_cparams(("arbitrary", "arbitrary")),
        name=name,
    )(q, k, v)


def _gates(logits, bias):
    scores = _sigmoid(logits)
    sel = scores + bias
    lane = lax.broadcasted_iota(jnp.int32, logits.shape, 1)
    e4 = lane & (EXPERTS_PER_GROUP - 1)
    grp = (lane >> 2) & (N_GROUPS - 1)
    one, zero = jnp.float32(1.0), jnp.float32(0.0)

    rank = jnp.zeros(logits.shape, F32)
    for d in range(1, EXPERTS_PER_GROUP):
        other = _win_roll(sel, d, EXPERTS_PER_GROUP)
        other_first = ((e4 + d) & (EXPERTS_PER_GROUP - 1)) < e4
        ahead = jnp.logical_or(other > sel, jnp.logical_and(other == sel, other_first))
        rank = rank + jnp.where(ahead, one, zero)
    top2 = rank < float(TOP_K)

    def group_sum(a):
        tot = a
        for d in range(1, EXPERTS_PER_GROUP):
            tot = tot + _win_roll(a, d, EXPERTS_PER_GROUP)
        return tot

    gscore = group_sum(jnp.where(top2, sel, zero))
    beaten = jnp.zeros(logits.shape, F32)
    for d in range(1, N_GROUPS):
        other = _win_roll(gscore, d * EXPERTS_PER_GROUP, N_EXPERTS)
        other_first = ((grp + d) & (N_GROUPS - 1)) < grp
        ahead = jnp.logical_or(other > gscore, jnp.logical_and(other == gscore, other_first))
        beaten = beaten + jnp.where(ahead, one, zero)
    chosen = jnp.logical_and(jnp.logical_and(beaten < 0.5, top2), lane < N_EXPERTS)
    w = jnp.where(chosen, scores, zero)
    den = group_sum(w)
    return jnp.where(chosen, w / den, zero)


def _outproj_kernel(ret_ref, na_ref, mla_ref, h_ref, w_ref, g1_ref, sh_ref, sc_ref, n2_ref,
                    rw_ref, hn_ref, u_ref, logit_ref):
    mix = (_dot(ret_ref[0], w_ref[0:256, :]) + _dot(na_ref[0], w_ref[256:768, :])
           + _dot(mla_ref[0], w_ref[768:1024, :]))
    hn = h_ref[0] + g1_ref[0] * mix
    hn_ref[0] = hn
    u = _rms(hn, n2_ref[...]) * (1.0 + sc_ref[0]) + sh_ref[0]
    u_hi = u.astype(BF16)
    u_ref[0] = u_hi
    tm = u.shape[0]
    u_lo = (u - u_hi.astype(F32)).astype(BF16)
    both = _dot(jnp.concatenate([u_hi, u_lo], axis=0), rw_ref[...])
    logits = both[:tm, :LANES] + both[:tm, LANES:] + both[tm:, :LANES]
    logit_ref[0] = logits


def _out_projection(ret_o, na_o, mla_o, h, w_out, g1, sh2, sc2, n2, rw, tm):
    b, l, d = h.shape
    per_batch = g1.shape[0] > 1
    mod_map = (lambda i, j: (i, 0, 0)) if per_batch else (lambda i, j: (0, 0, 0))
    const2 = lambda i, j: (0, 0)
    tok = lambda w: pl.BlockSpec((1, tm, w), lambda i, j: (i, j, 0))
    mod = pl.BlockSpec((1, 1, d), mod_map)
    return pl.pallas_call(
        _outproj_kernel,
        grid=(b, l // tm),
        in_specs=[tok(256), tok(512), tok(256), tok(d),
                  pl.BlockSpec(w_out.shape, const2), mod, mod, mod,
                  pl.BlockSpec((1, d), const2),
                  pl.BlockSpec(rw.shape, const2)],
        out_specs=[tok(d), tok(d), tok(LANES)],
        out_shape=[jax.ShapeDtypeStruct((b, l, d), F32),
                   jax.ShapeDtypeStruct((b, l, d), BF16),
                   jax.ShapeDtypeStruct((b, l, LANES), F32)],
        compiler_params=_cparams(("arbitrary", "arbitrary")),
        name="out_projection",
    )(ret_o, na_o, mla_o, h, w_out, g1, sh2, sc2, n2, rw)


def _moe_kernel(u_ref, logit_ref, logit_next_ref, rb_ref, h_ref, w13_ref, w2_ref, g2_ref, fg_ref,
                o_ref, gate_scr, *, final_norm):
    u = u_ref[...]

    @pl.when(pl.program_id(0) == 0)
    def _():
        gate_scr[...] = _gates(logit_ref[...], rb_ref[...])

    gates = gate_scr[...]
    gates_next = _gates(logit_next_ref[...], rb_ref[...])
    lane = lax.broadcasted_iota(jnp.int32, gates.shape, 1)
    y = None
    for e in range(N_EXPERTS):
        a = _dot(u, w13_ref[e])
        a1, a3 = a[:, :EXPERT_FF], a[:, EXPERT_FF:]
        gcol = jnp.sum(jnp.where(lane == e, gates, 0.0), axis=-1, keepdims=True)
        hdn = ((a1 * _sigmoid(a1)) * a3 * gcol).astype(BF16)
        ye = _dot(hdn, w2_ref[e])
        y = ye if y is None else y + ye
    hn = h_ref[...] + g2_ref[0] * y
    if final_norm:
        hn = _rms(hn, fg_ref[...])
    o_ref[...] = hn
    gate_scr[...] = gates_next


def _moe(u, logits, rb, h, w13, w2, g2, fg, tokens_per_mod, tm, final_norm):
    t, d = h.shape
    per_batch = g2.shape[0] > 1
    tiles_per_mod = tokens_per_mod // tm
    n_tiles = t // tm
    mod_map = (lambda i: (i // tiles_per_mod, 0, 0)) if per_batch else (lambda i: (0, 0, 0))
    tok = lambda w: pl.BlockSpec((tm, w), lambda i: (i, 0))
    resident = lambda a: pl.BlockSpec(a.shape, lambda i: (0, 0, 0), pipeline_mode=pl.Buffered(1))
    return pl.pallas_call(
        functools.partial(_moe_kernel, final_norm=final_norm),
        grid=(n_tiles,),
        in_specs=[tok(d), tok(LANES),
                  pl.BlockSpec((tm, LANES), lambda i: (jnp.minimum(i + 1, n_tiles - 1), 0)),
                  pl.BlockSpec(rb.shape, lambda i: (0, 0)), tok(d),
                  resident(w13), resident(w2),
                  pl.BlockSpec((1, 1, d), mod_map),
                  pl.BlockSpec((1, d), lambda i: (0, 0))],
        out_specs=tok(d),
        out_shape=jax.ShapeDtypeStruct((t, d), F32),
        scratch_shapes=[pltpu.VMEM((tm, LANES), F32)],
        compiler_params=_cparams(("arbitrary",)),
        name="mixture_of_experts",
    )(u, logits, logits, rb, h, w13, w2, g2, fg)


def _rope_tables(seq, positional):
    if not positional:
        ones = jnp.ones((seq, LANES), F32)
        zeros = jnp.zeros((seq, LANES), F32)
        return ones, zeros, ones, zeros
    t = jnp.arange(seq)
    pos_r = (t // GRID_W).astype(F32)[:, None]
    pos_c = (t % GRID_W).astype(F32)[:, None]

    def axial(width):
        h = width // 2
        half = h // 2
        inv = ROPE_BASE ** (-jnp.arange(half, dtype=F32) / half)
        inv2 = jnp.concatenate([inv, inv])[None, :]
        ang = jnp.concatenate([pos_r * inv2, pos_c * inv2], axis=1)
        sign = jnp.tile(jnp.concatenate([-jnp.ones(half), jnp.ones(half)]), 2)[None, :]
        return jnp.cos(ang), jnp.sin(ang) * sign

    rc, rs = axial(RET_DK)
    ra = jnp.concatenate([rc, rc], axis=1)
    rb = jnp.concatenate([rs, rs], axis=1)
    mc, ms = axial(MLA_ROPE)
    ma = jnp.concatenate([jnp.ones((seq, MLA_NOPE), F32), mc, jnp.zeros((seq, 32), F32)], axis=1)
    mb = jnp.concatenate([jnp.zeros((seq, MLA_NOPE), F32), ms, jnp.zeros((seq, 32), F32)], axis=1)
    return ra, rb, ma, mb


def _pad_heads(w, heads, width):
    r = w.shape[0]
    w = w.reshape(r, heads, width)
    return jnp.pad(w, ((0, 0), (0, 0), (0, LANES - width))).reshape(r, heads * LANES)


def _layer_weights(w_in, w_uq, w_ukv):
    d = w_in.shape[0]
    kpe = w_in[:, _C_KPE:_C_KPE + MLA_ROPE]
    blk = jnp.concatenate([jnp.zeros((d, MLA_NOPE), w_in.dtype), kpe,
                           jnp.zeros((d, LANES - MLA_NOPE - MLA_ROPE), w_in.dtype)], axis=1)
    w_ext = jnp.concatenate([w_in[:, :_C_KPE], blk], axis=1).astype(BF16)
    wuq = _pad_heads(w_uq, MLA_HEADS, MLA_NOPE + MLA_ROPE).astype(BF16)
    ukv = w_ukv.reshape(w_ukv.shape[0], MLA_HEADS, MLA_NOPE + MLA_VD)
    wuk = _pad_heads(ukv[:, :, :MLA_NOPE].reshape(w_ukv.shape[0], -1), MLA_HEADS, MLA_NOPE).astype(BF16)
    wuv = _pad_heads(ukv[:, :, MLA_NOPE:].reshape(w_ukv.shape[0], -1), MLA_HEADS, MLA_VD).astype(BF16)
    return w_ext, wuq, wuk, wuv


def _latent_key_tile(n_keys, n_ctx):
    for tk in range(MLA_MAX_KEY_TILE, 0, -LANES):
        if tk >= n_ctx and n_keys % tk == 0 and (n_keys // tk) % 2 == 0:
            return tk
    raise ValueError(f"no even tiling of {n_keys} keys")


def kernel(x, c, ctx, c_ctx, w_mod, b_mod, norm1_g, norm2_g, w_in, ret_decay_f, ret_decay_b, na_rpb,
           mla_q_norm, mla_kv_norm, w_uq, w_ukv, w_out, router_w, router_b, w1, w3, w2, final_norm_g):
    b, s, d = x.shape
    lc = ctx.shape[1]
    depth = w_mod.shape[0]
    rows = s // GRID_W
    tm_lat = 512
    tm_ctx = min(256, lc)

    n_mod = -(-(b + 1) // 8) * 8
    cv = jnp.concatenate([c, c_ctx[None, :], jnp.zeros((n_mod - b - 1, d), F32)], axis=0)
    mods = _modulation(cv, w_mod, b_mod).reshape(depth, n_mod, 6, d)

    tabs_lat = _rope_tables(s, True)
    tabs_ctx = _rope_tables(lc, False)
    na_plan = _na_bias_plan(rows)
    rw_f32 = jnp.pad(router_w.astype(F32), ((0, 0), (0, LANES - N_EXPERTS)))
    rw_hi = rw_f32.astype(BF16)
    rw = jnp.concatenate([rw_hi, (rw_f32 - rw_hi.astype(F32)).astype(BF16)], axis=1)
    rb = jnp.pad(router_b.astype(F32), (0, LANES - N_EXPERTS)).reshape(1, LANES)
    zero_state = (jnp.zeros((b, 2 * LANES, LANES), F32),) * 2

    h, hc = x, ctx
    for l in range(depth):
        last = l == depth - 1
        m_lat = [mods[l, :b, i][:, None, :] for i in range(6)]
        m_ctx = [mods[l, b:b + 1, i][:, None, :] for i in range(6)]
        w_ext, wuq, wuk, wuv = _layer_weights(w_in[l], w_uq[l], w_ukv[l])
        n1 = norm1_g[l].reshape(1, d)
        n2 = norm2_g[l].reshape(1, d)
        qn = mla_q_norm[l].reshape(1, -1)
        kvn = mla_kv_norm[l].reshape(1, -1)
        wo = w_out[l].astype(BF16)
        w13 = jnp.concatenate([w1[l], w3[l]], axis=-1).astype(BF16)
        w2b = w2[l].astype(BF16)
        fg = final_norm_g.reshape(1, d)

        ret_c, na_c, mq_c, mk_c, mv_c = _in_projection(
            hc, m_ctx[0], m_ctx[1], n1, w_ext, qn, kvn, wuq, wuk, wuv, tabs_ctx, tm_ctx)
        ret_l, na_l, mq_l, mk_l, mv_l = _in_projection(
            h, m_lat[0], m_lat[1], n1, w_ext, qn, kvn, wuq, wuk, wuv, tabs_lat, tm_lat)

        rtabs = _retention_tables(ret_decay_f[l], ret_decay_b[l])
        ro_c, st = _retention(ret_c, rtabs, zero_state)
        ro_l, _ = _retention(ret_l, rtabs, st)

        na_o = _na_attention(na_l, na_c, _na_bias(na_rpb[l], na_plan))
        mla_o = _latent_attention(mq_l, mk_l, mv_l, (mk_c, mv_c), tq=256, q_tiles=MLA_Q_TILES,
                                  tk=_latent_key_tile(s, lc), name="latent_attention")

        hn, u2, logits = _out_projection(ro_l, na_o, mla_o, h, wo, m_lat[2], m_lat[3], m_lat[4],
                                         n2, rw, tm_lat)
        h = _moe(u2.reshape(b * s, d), logits.reshape(b * s, LANES), rb, hn.reshape(b * s, d),
                 w13, w2b, m_lat[5], fg, s, 512, last).reshape(b, s, d)

        if not last:
            nac_o = _pair_attention(na_c[:, :, 0:512], na_c[:, :, 512:1024], na_c[:, :, 1024:1536],
                                    tq=tm_ctx, name="context_na_attention")
            mlac_o = _latent_attention(mq_c, mk_c, mv_c, tq=tm_ctx, q_tiles=1, tk=lc,
                                       name="context_latent_attention")
            hcn, uc2, logits_c = _out_projection(ro_c, nac_o, mlac_o, hc, wo, m_ctx[2], m_ctx[3],
                                                 m_ctx[4], n2, rw, tm_ctx)
            hc = _moe(uc2.reshape(b * lc, d), logits_c.reshape(b * lc, LANES), rb,
                      hcn.reshape(b * lc, d), w13, w2b, m_ctx[5], fg, lc, min(512, lc),
                      False).reshape(b, lc, d)
    return h
```

```python
import functools

import numpy as np
import jax
import jax.numpy as jnp
from jax import lax
from jax.experimental import pallas as pl
from jax.experimental.pallas import tpu as pltpu

F32 = jnp.float32
BF16 = jnp.bfloat16

GRID_W = 64
EPS = 1e-6
ROPE_BASE = 10000.0
RET_HEADS, RET_DK, RET_DV, RET_CHUNK = 4, 64, 64, 128
NA_HEADS, NA_HD, NA_KH, NA_KW, NA_QBLOCK = 8, 64, 8, 16, 128
MLA_HEADS, MLA_NOPE, MLA_ROPE, MLA_VD = 4, 64, 32, 64
MLA_Q_RANK, MLA_KV_RANK = 384, 256
N_EXPERTS, N_GROUPS, TOP_K, EXPERT_FF = 16, 4, 2, 256
EXPERTS_PER_GROUP = N_EXPERTS // N_GROUPS

LANES = 128
HEAD = 64
TOKEN_TILE = 512
CTX_TILE = 256
NA_WIN_BLOCKS = 5
MLA_Q_TILE = 256
MLA_Q_TILES = 4
MLA_MAX_KEY_TILE = 1024
NA_SUB = 2
NA_STEP_BLOCKS = NA_WIN_BLOCKS + NA_SUB - 1
RET_GROUP = 8
NEG_BIG = -1e30
VMEM_LIMIT = 48 * 1024 * 1024

_C_RET, _C_NA, _C_CQ, _C_CKV, _C_KPE = 0, 1024, 2560, 2944, 3200
_W_EXT = _C_KPE + LANES
MLA_Q_SCALE = (MLA_NOPE + MLA_ROPE) ** -0.5 * float(np.log2(np.e))


def _cparams(sem):
    return pltpu.CompilerParams(dimension_semantics=sem, vmem_limit_bytes=VMEM_LIMIT)


def _dot(a, b):
    return jnp.dot(a, b, preferred_element_type=F32)


def _dot_nt(a, b):
    return lax.dot_general(a, b, (((1,), (1,)), ((), ())), preferred_element_type=F32)


def _sigmoid(x):
    return 1.0 / (1.0 + jnp.exp(-x))


def _win_roll(x, s, w):
    n = x.shape[-1]
    ax = x.ndim - 1
    lane = lax.broadcasted_iota(jnp.int32, x.shape, ax)
    fwd = pltpu.roll(x, n - s, ax)
    bwd = pltpu.roll(x, w - s, ax)
    return jnp.where((lane & (w - 1)) + s < w, fwd, bwd)


def _low_half(shape):
    lane = lax.broadcasted_iota(jnp.int32, shape, len(shape) - 1)
    return (lane & (LANES - 1)) < HEAD


def _mod_kernel(cv_ref, w_ref, b_ref, o_ref):
    cv = cv_ref[...]
    a = cv * _sigmoid(cv)
    o_ref[0] = jnp.dot(a, w_ref[0], preferred_element_type=F32,
                       precision=lax.Precision.HIGHEST) + b_ref[0]


def _modulation(cv, w_mod, b_mod):
    depth, d, n6 = w_mod.shape
    rows = cv.shape[0]
    tn = 1536
    return pl.pallas_call(
        _mod_kernel,
        grid=(depth, n6 // tn),
        in_specs=[pl.BlockSpec((rows, d), lambda l, j: (0, 0)),
                  pl.BlockSpec((1, d, tn), lambda l, j: (l, 0, j)),
                  pl.BlockSpec((1, 1, tn), lambda l, j: (l, 0, j))],
        out_specs=pl.BlockSpec((1, rows, tn), lambda l, j: (l, 0, j)),
        out_shape=jax.ShapeDtypeStruct((depth, rows, n6), F32),
        compiler_params=_cparams(("arbitrary", "arbitrary")),
        name="modulation",
    )(cv, w_mod, b_mod.reshape(depth, 1, n6))


def _rms(x, g):
    return x * lax.rsqrt(jnp.mean(x * x, axis=-1, keepdims=True) + EPS) * g


def _inproj_kernel(h_ref, sh_ref, sc_ref, g_ref, w_ref, qn_ref, kvn_ref, wuq_ref, wuk_ref, wuv_ref,
                   ra_ref, rb_ref, ma_ref, mb_ref,
                   ret_ref, na_ref, q_ref, k_ref, v_ref):
    x = h_ref[0]
    u = _rms(x, g_ref[...]) * (1.0 + sc_ref[0]) + sh_ref[0]
    ub = u.astype(BF16)

    r = _dot(ub, w_ref[:, _C_RET:_C_NA])
    qk = r[:, :512]
    ra = jnp.concatenate([ra_ref[...]] * 4, axis=1)
    rb = jnp.concatenate([rb_ref[...]] * 4, axis=1)
    qk = qk * ra + _win_roll(qk, 16, 32) * rb
    ret_ref[0, :, 0:256] = qk[:, :256] * (RET_DK ** -0.5)
    ret_ref[0, :, 256:512] = qk[:, 256:]
    ret_ref[0, :, 512:1024] = r[:, 512:]

    n = _dot(ub, w_ref[:, _C_NA:_C_CQ])
    na_ref[0, :, 0:512] = (n[:, :512] * (NA_HD ** -0.5)).astype(BF16)
    na_ref[0, :, 512:1536] = n[:, 512:].astype(BF16)

    ma = jnp.concatenate([ma_ref[...]] * 4, axis=1)
    mb = jnp.concatenate([mb_ref[...]] * 4, axis=1)
    cq = _rms(_dot(ub, w_ref[:, _C_CQ:_C_CKV]), qn_ref[...]).astype(BF16)
    q = _dot(cq, wuq_ref[...])
    q = q * ma + _win_roll(q, 8, 16) * mb
    q_ref[0] = (q * MLA_Q_SCALE).astype(BF16)
    ckv = _rms(_dot(ub, w_ref[:, _C_CKV:_C_KPE]), kvn_ref[...]).astype(BF16)
    kp = _dot(ub, w_ref[:, _C_KPE:_W_EXT])
    kp = kp * ma_ref[...] + _win_roll(kp, 8, 16) * mb_ref[...]
    kp = jnp.concatenate([kp] * MLA_HEADS, axis=1)
    k_ref[0] = (_dot(ckv, wuk_ref[...]) + kp).astype(BF16)
    v = _dot(ckv, wuv_ref[...])
    lane = lax.broadcasted_iota(jnp.int32, v.shape, 1)
    v_ref[0] = jnp.where((lane & (LANES - 1)) == HEAD, 1.0, v).astype(BF16)


def _in_projection(h, sh, sc, g, w_ext, qn, kvn, wuq, wuk, wuv, tabs, tm):
    b, l, d = h.shape
    per_batch = sh.shape[0] > 1
    mod_map = (lambda i, j: (i, 0, 0)) if per_batch else (lambda i, j: (0, 0, 0))
    const2 = lambda i, j: (0, 0)
    tab_spec = pl.BlockSpec((tm, LANES), lambda i, j: (j, 0))
    tok = lambda w: pl.BlockSpec((1, tm, w), lambda i, j: (i, j, 0))
    return pl.pallas_call(
        _inproj_kernel,
        grid=(b, l // tm),
        in_specs=[tok(d),
                  pl.BlockSpec((1, 1, d), mod_map), pl.BlockSpec((1, 1, d), mod_map),
                  pl.BlockSpec((1, d), const2),
                  pl.BlockSpec(w_ext.shape, const2),
                  pl.BlockSpec(qn.shape, const2), pl.BlockSpec(kvn.shape, const2),
                  pl.BlockSpec(wuq.shape, const2), pl.BlockSpec(wuk.shape, const2),
                  pl.BlockSpec(wuv.shape, const2),
                  tab_spec, tab_spec, tab_spec, tab_spec],
        out_specs=[tok(1024), tok(1536), tok(512), tok(512), tok(512)],
        out_shape=[jax.ShapeDtypeStruct((b, l, 1024), F32),
                   jax.ShapeDtypeStruct((b, l, 1536), BF16),
                   jax.ShapeDtypeStruct((b, l, 512), BF16),
                   jax.ShapeDtypeStruct((b, l, 512), BF16),
                   jax.ShapeDtypeStruct((b, l, 512), BF16)],
        compiler_params=_cparams(("arbitrary", "arbitrary")),
        name="in_projection",
    )(h, sh, sc, g, w_ext, qn, kvn, wuq, wuk, wuv, *tabs)


def _retention_kernel(*refs, backward, group):
    if backward:
        (q_ref, k_ref, v_ref, g_ref, of_ref, dm_ref, qw_ref, kw_ref, gc_ref, s0_ref,
         o_ref, sfin_ref, state) = refs
    else:
        q_ref, k_ref, v_ref, dm_ref, qw_ref, kw_ref, gc_ref, s0_ref, o_ref, sfin_ref, state = refs
    step = pl.program_id(1)
    c = RET_CHUNK
    n_pairs = RET_HEADS // 2

    @pl.when(step == 0)
    def _():
        state[...] = s0_ref[0]

    lo = _low_half((c, LANES))
    row_lo = lax.broadcasted_iota(jnp.int32, (LANES, LANES), 0) < HEAD
    same_head = row_lo == _low_half((LANES, LANES))
    qw, kw = qw_ref[...], kw_ref[...]

    intra, kv, qd = {}, {}, {}
    for ci in range(group):
        rows = slice(c * ci, c * (ci + 1))
        for p in range(n_pairs):
            sl = slice(LANES * p, LANES * (p + 1))
            qp, kp = q_ref[0, rows, sl], k_ref[0, rows, sl]
            kb, vb = kp.astype(BF16), v_ref[0, rows, sl].astype(BF16)
            qa = jnp.where(lo, qp, 0.0).astype(BF16)
            qb = jnp.where(lo, 0.0, qp).astype(BF16)
            sa = (_dot_nt(qa, kb) * dm_ref[2 * p]).astype(BF16)
            sb = (_dot_nt(qb, kb) * dm_ref[2 * p + 1]).astype(BF16)
            intra[ci, p] = jnp.where(lo, _dot(sa, vb), _dot(sb, vb))
            kv[ci, p] = jnp.where(same_head, _dot((kp * kw[:, sl]).T.astype(BF16), vb), 0.0)
            qd[ci, p] = (qp * qw[:, sl]).astype(BF16)

    order = range(group - 1, -1, -1) if backward else range(group)
    out = {}
    for p in range(n_pairs):
        sl = slice(LANES * p, LANES * (p + 1))
        sp = state[sl, :]
        gc = gc_ref[sl, :]
        for ci in order:
            out[ci, p] = intra[ci, p] + _dot(qd[ci, p], sp.astype(BF16))
            sp = gc * sp + kv[ci, p]
        state[sl, :] = sp

    for ci in range(group):
        rows = slice(c * ci, c * (ci + 1))
        o = jnp.concatenate([out[ci, p] for p in range(n_pairs)], axis=1)
        if not backward:
            o_ref[0, rows, :] = o
            continue
        tot = of_ref[0, rows, :] + o
        g = g_ref[0, rows, :]
        normed = []
        for p in range(n_pairs):
            t = tot[:, LANES * p:LANES * (p + 1)]

            def head_mean(a):
                s_lo = jnp.sum(jnp.where(lo, a, 0.0), axis=-1, keepdims=True)
                s_hi = jnp.sum(jnp.where(lo, 0.0, a), axis=-1, keepdims=True)
                return jnp.where(lo, s_lo, s_hi) * (1.0 / HEAD)

            dlt = t - head_mean(t)
            normed.append(dlt * lax.rsqrt(head_mean(dlt * dlt) + EPS))
        on = jnp.concatenate(normed, axis=1)
        o_ref[0, rows, :] = (on * (g * _sigmoid(g))).astype(BF16)

    @pl.when(step == pl.num_programs(1) - 1)
    def _():
        sfin_ref[0] = state[...]


def _retention_sweep(ret, o_fwd, tabs, s0, *, backward):
    b, l, _ = ret.shape
    c = RET_CHUNK
    group = min(RET_GROUP, l // c)
    n = l // (c * group)
    gidx = (lambda s: n - 1 - s) if backward else (lambda s: s)
    tok = lambda col: pl.BlockSpec((1, c * group, 256), lambda i, s: (i, gidx(s), col))
    const = lambda a: pl.BlockSpec(a.shape, lambda i, s: (0,) * a.ndim)
    state_spec = pl.BlockSpec((1, 2 * LANES, LANES), lambda i, s: (i, 0, 0))
    dm, qw, kw, gc = tabs
    if backward:
        in_specs = [tok(0), tok(1), tok(2), tok(3), tok(0)]
        args = [ret, ret, ret, ret, o_fwd]
    else:
        in_specs = [tok(0), tok(1), tok(2)]
        args = [ret, ret, ret]
    return pl.pallas_call(
        functools.partial(_retention_kernel, backward=backward, group=group),
        grid=(b, n),
        in_specs=in_specs + [const(dm), const(qw), const(kw), const(gc), state_spec],
        out_specs=[tok(0), state_spec],
        out_shape=[jax.ShapeDtypeStruct((b, l, 256), BF16 if backward else F32),
                   jax.ShapeDtypeStruct((b, 2 * LANES, LANES), F32)],
        scratch_shapes=[pltpu.VMEM((2 * LANES, LANES), F32)],
        compiler_params=_cparams(("arbitrary", "arbitrary")),
        name="retention_backward" if backward else "retention_forward",
    )(*args, dm, qw, kw, gc, s0)


def _retention(ret, tabs, s0):
    fwd_tabs = tuple(t[0] for t in tabs)
    bwd_tabs = tuple(t[1] for t in tabs)
    o_f, s_f = _retention_sweep(ret, None, fwd_tabs, s0[0], backward=False)
    o, s_b = _retention_sweep(ret, o_f, bwd_tabs, s0[1], backward=True)
    return o, (s_f, s_b)


def _retention_tables(decay_f, decay_b):
    c = RET_CHUNK
    pos = jnp.arange(c, dtype=F32)
    diff = pos[:, None] - pos[None, :]
    lg = jnp.stack([jax.nn.log_sigmoid(decay_f.astype(F32)), jax.nn.log_sigmoid(decay_b.astype(F32))])
    lgh = lg[:, :, None, None]
    dm_f = jnp.where(diff >= 0, jnp.exp(lgh[0] * jnp.maximum(diff, 0.0)), 0.0)
    dm_b = jnp.where(diff < 0, jnp.exp(lgh[1] * jnp.maximum(-diff, 0.0)), 0.0)
    dm = jnp.stack([dm_f, dm_b])
    lanes = lambda t: jnp.repeat(t, RET_DK, axis=-1)
    lgc = lg[:, None, :]
    qw = lanes(jnp.exp(lgc * jnp.stack([pos + 1.0, c - pos])[:, :, None]))
    kw = lanes(jnp.exp(lgc * jnp.stack([c - 1.0 - pos, pos])[:, :, None]))
    gch = jnp.exp(lg * c)
    gc = jnp.broadcast_to(jnp.repeat(gch, RET_DK, axis=-1)[:, :, None], (2, RET_HEADS * RET_DK, LANES))
    return dm, qw, kw, gc


def _na_kernel(*refs, n_ctx):
    w = NA_STEP_BLOCKS
    q_ref = refs[0]
    k_refs, v_refs = refs[1:1 + w], refs[1 + w:1 + 2 * w]
    kc_ref, vc_ref, bias_ref, o_ref, s_scr, p_scr, den_scr = refs[1 + 2 * w:]
    tq = q_ref.shape[1]
    n_loc = w * NA_QBLOCK
    n_lane_blocks = (n_loc + n_ctx) // LANES
    lo = _low_half((tq, LANES))
    for h in range(NA_HEADS):
        sl = slice(LANES * (h // 2), LANES * (h // 2 + 1))
        qp = q_ref[0, :, sl]
        qm = jnp.where(lo, qp, 0) if h % 2 == 0 else jnp.where(lo, 0, qp)
        kwin = jnp.concatenate([r[0, :, sl] for r in k_refs], axis=0)
        s_scr[h, :, 0:n_loc] = _dot_nt(qm, kwin) + bias_ref[0, h]
        s_scr[h, :, n_loc:] = _dot_nt(qm, kc_ref[0, :, sl])
    for h in range(NA_HEADS):
        part = s_scr[h, :, 0:LANES]
        for c in range(1, n_lane_blocks):
            part = jnp.maximum(part, s_scr[h, :, LANES * c:LANES * (c + 1)])
        m = jnp.max(part, axis=-1, keepdims=True)
        tot = None
        for c in range(n_lane_blocks):
            cs = slice(LANES * c, LANES * (c + 1))
            e = jnp.exp(s_scr[h, :, cs] - m)
            tot = e if tot is None else tot + e
            p_scr[h, :, cs] = e.astype(BF16)
        den_scr[h] = jnp.broadcast_to(jnp.sum(tot, axis=-1, keepdims=True), (tq, LANES))
    for p in range(NA_HEADS // 2):
        sl = slice(LANES * p, LANES * (p + 1))
        vwin = jnp.concatenate([r[0, :, sl] for r in v_refs], axis=0)
        halves = []
        for h in (2 * p, 2 * p + 1):
            o = _dot(p_scr[h, :, 0:n_loc], vwin) + _dot(p_scr[h, :, n_loc:], vc_ref[0, :, sl])
            halves.append(o / den_scr[h])
        o_ref[0, :, sl] = jnp.where(lo, halves[0], halves[1]).astype(BF16)


def _na_step_variant(i, n_steps):
    return jnp.where(i == 0, 0, jnp.where(i == n_steps - 1, 2, 1))


def _na_attention(na, na_ctx, bias):
    b, l, _ = na.shape
    n_ctx = na_ctx.shape[1]
    nblk = l // NA_QBLOCK
    n_steps = nblk // NA_SUB
    tq = NA_SUB * NA_QBLOCK
    n_loc = NA_STEP_BLOCKS * NA_QBLOCK
    n_keys = n_loc + n_ctx

    def win(col, j):
        return pl.BlockSpec((1, NA_QBLOCK, 512), lambda bi, i: (
            bi, jnp.clip(NA_SUB * i - 2, 0, nblk - NA_STEP_BLOCKS) + j, col))

    in_specs = ([pl.BlockSpec((1, tq, 512), lambda bi, i: (bi, i, 0))]
                + [win(1, j) for j in range(NA_STEP_BLOCKS)]
                + [win(2, j) for j in range(NA_STEP_BLOCKS)]
                + [pl.BlockSpec((1, n_ctx, 512), lambda bi, i: (bi, 0, 1)),
                   pl.BlockSpec((1, n_ctx, 512), lambda bi, i: (bi, 0, 2)),
                   pl.BlockSpec((1, NA_HEADS, tq, n_loc),
                                lambda bi, i: (_na_step_variant(i, n_steps), 0, 0, 0))])
    args = [na] * (1 + 2 * NA_STEP_BLOCKS) + [na_ctx, na_ctx, bias]
    return pl.pallas_call(
        functools.partial(_na_kernel, n_ctx=n_ctx),
        grid=(b, n_steps),
        in_specs=in_specs,
        out_specs=pl.BlockSpec((1, tq, 512), lambda bi, i: (bi, i, 0)),
        out_shape=jax.ShapeDtypeStruct((b, l, 512), BF16),
        scratch_shapes=[pltpu.VMEM((NA_HEADS, tq, n_keys), F32),
                        pltpu.VMEM((NA_HEADS, tq, n_keys), BF16),
                        pltpu.VMEM((NA_HEADS, tq, LANES), F32)],
        compiler_params=_cparams(("arbitrary", "arbitrary")),
        name="neighbourhood_attention",
    )(*args)


def _na_bias_plan(rows):
    nblk = rows * GRID_W // NA_QBLOCK
    kh = min(NA_KH, rows)
    q_rows = NA_QBLOCK // GRID_W
    k_rows = NA_WIN_BLOCKS * q_rows
    plan = []
    for i in (0, 1, 2, nblk - 2, nblk - 1):
        kb = min(max(i - 2, 0), nblk - NA_WIN_BLOCKS)
        per_q = []
        for rr in range(q_rows):
            r = q_rows * i + rr
            r0 = min(max(r - kh // 2, 0), rows - kh)
            per_q.append([(q_rows * kb + kj - r + NA_KH - 1) if r0 <= q_rows * kb + kj < r0 + kh else None
                          for kj in range(k_rows)])
        plan.append(per_q)
    return plan


def _na_bias(rpb, plan):
    c = np.arange(GRID_W)
    c0 = np.clip(c - NA_KW // 2, 0, GRID_W - NA_KW)
    kc = np.arange(GRID_W)
    ok = (kc[None, :] >= c0[:, None]) & (kc[None, :] < c0[:, None] + NA_KW)
    reach = GRID_W - NA_KW
    padded = jnp.pad(rpb.astype(F32), ((0, 0), (0, 0), (reach, reach)))
    cols = jnp.stack([padded[:, :, GRID_W - 1 - ci:2 * GRID_W - 1 - ci] for ci in range(GRID_W)], axis=2)
    cols = jnp.where(jnp.asarray(ok)[None, None], cols, NEG_BIG)
    outside = jnp.full((NA_HEADS, GRID_W, GRID_W), NEG_BIG, F32)
    variants = []
    for per_q in plan:
        q_parts = [jnp.concatenate([outside if dr is None else cols[:, dr] for dr in per_k], axis=-1)
                   for per_k in per_q]
        variants.append(jnp.concatenate(q_parts, axis=1))

    hidden = jnp.full((NA_HEADS, NA_QBLOCK, NA_QBLOCK), NEG_BIG, F32)

    def place(blk, off):
        return jnp.concatenate([hidden] * off + [blk] + [hidden] * (1 - off), axis=-1)

    steps = [(0, 0, 1, 0), (2, 0, 2, 1), (3, 1, 4, 1)]
    return jnp.stack([jnp.concatenate([place(variants[va], oa), place(variants[vb], ob)], axis=1)
                      for va, oa, vb, ob in steps])


def _latent_attn_kernel(*refs, tk, q_tiles, n_ctx):
    if n_ctx:
        q_ref, k_ref, v_ref, kc_ref, vc_ref, o_ref, s_scr, p_scr, al_scr, m_scr, acc_scr = refs
    else:
        q_ref, k_ref, v_ref, o_ref, s_scr, p_scr, al_scr, m_scr, acc_scr = refs
    n_k = k_ref.shape[1] // tk + (1 if n_ctx else 0)
    n_heads = v_ref.shape[2] // LANES
    tq = q_ref.shape[1] // q_tiles
    assert n_k % 2 == 1, "static slot parities below assume an odd number of key tiles"

    def is_ctx(j):
        return bool(n_ctx) and isinstance(j, int) and j == n_k - 1

    def key_rows(j):
        return pl.ds(pl.multiple_of(j * tk, tk), tk)

    def slot_of(a, j_parity):
        return (a * n_k + j_parity) % 2

    def scores(a, j, slot):
        for h in range(n_heads):
            sl = slice(LANES * h, LANES * (h + 1))
            qh = q_ref[0, tq * a:tq * (a + 1), sl]
            if is_ctx(j):
                s_scr[slot, h, :, 0:n_ctx] = _dot_nt(qh, kc_ref[0, :, sl])
            else:
                s_scr[slot, h] = _dot_nt(qh, k_ref[0, key_rows(j), sl])

    def softmax(a, slot, ctx_tile=False):
        n_lane_blocks = (n_ctx if ctx_tile else tk) // LANES
        for h in range(n_heads):
            part = s_scr[slot, h, :, 0:LANES]
            for c in range(1, n_lane_blocks):
                part = jnp.maximum(part, s_scr[slot, h, :, LANES * c:LANES * (c + 1)])
            m_old = m_scr[a, h]
            m_new = jnp.maximum(m_old, jnp.max(part, axis=-1, keepdims=True))
            al_scr[slot, h] = jnp.exp2(m_old - m_new)
            m_scr[a, h] = m_new
            for c in range(n_lane_blocks):
                cs = slice(LANES * c, LANES * (c + 1))
                p_scr[slot, h, :, cs] = jnp.exp2(s_scr[slot, h, :, cs] - m_new).astype(BF16)

    def values(a, j, slot):
        for h in range(n_heads):
            sl = slice(LANES * h, LANES * (h + 1))
            if is_ctx(j):
                pv = _dot(p_scr[slot, h, :, 0:n_ctx], vc_ref[0, :, sl])
            else:
                pv = _dot(p_scr[slot, h], v_ref[0, key_rows(j), sl])
            acc_scr[a, h] = al_scr[slot, h] * acc_scr[a, h] + pv

    def time_step(nxt, cur, prv):
        if nxt is not None:
            scores(*nxt)
        softmax(*cur)
        if prv is not None:
            values(*prv)

    m_scr[...] = jnp.full(m_scr.shape, -jnp.inf, F32)
    acc_scr[...] = jnp.zeros(acc_scr.shape, F32)
    scores(0, 0, 0)
    for a in range(q_tiles):
        s_even, s_odd = slot_of(a, 0), slot_of(a, 1)
        first_of_next = (a + 1, 0, slot_of(a + 1, 0)) if a + 1 < q_tiles else None
        last_of_prev = (a - 1, n_k - 1, slot_of(a - 1, 0)) if a > 0 else None
        if n_k == 1:
            time_step(first_of_next, (a, s_even), last_of_prev)
            continue
        time_step((a, 1, s_odd), (a, s_even), last_of_prev)

        def body(i, carry, a=a, s_even=s_even, s_odd=s_odd):
            j = 2 * i + 1
            time_step((a, j + 1, s_even), (a, s_odd), (a, j - 1, s_even))
            time_step((a, j + 2, s_odd), (a, s_even), (a, j, s_odd))
            return carry

        lax.fori_loop(0, (n_k - 3) // 2, body, 0)
        time_step((a, n_k - 1, s_even), (a, s_odd), (a, n_k - 3, s_even))
        time_step(first_of_next, (a, s_even, bool(n_ctx)), (a, n_k - 2, s_odd))
    values(q_tiles - 1, n_k - 1, slot_of(q_tiles - 1, 0))
    for a in range(q_tiles):
        outs = []
        for h in range(n_heads):
            acc = acc_scr[a, h]
            outs.append(acc[:, 0:HEAD] / acc[:, HEAD:HEAD + 1])
        o_ref[0, tq * a:tq * (a + 1), :] = jnp.concatenate(outs, axis=1).astype(BF16)


def _latent_attention(q, k, v, ctx_kv=None, *, tq, q_tiles, tk, name):
    b, lq, w = q.shape
    lk = k.shape[1]
    n_heads = w // LANES
    rows = tq * q_tiles
    n_ctx = 0 if ctx_kv is None else ctx_kv[0].shape[1]
    resident = lambda n: pl.BlockSpec((1, n, w), lambda bi, i: (bi, 0, 0), pipeline_mode=pl.Buffered(1))
    stat = pltpu.VMEM((q_tiles, n_heads, tq, LANES), F32)
    in_specs = [pl.BlockSpec((1, rows, w), lambda bi, i: (bi, i, 0)), resident(lk), resident(lk)]
    args = [q, k, v]
    if n_ctx:
        assert n_ctx <= tk and n_ctx % LANES == 0
        in_specs += [resident(n_ctx), resident(n_ctx)]
        args += list(ctx_kv)
    return pl.pallas_call(
        functools.partial(_latent_attn_kernel, tk=tk, q_tiles=q_tiles, n_ctx=n_ctx),
        grid=(b, lq // rows),
        in_specs=in_specs,
        out_specs=pl.BlockSpec((1, rows, n_heads * HEAD), lambda bi, i: (bi, i, 0)),
        out_shape=jax.ShapeDtypeStruct((b, lq, n_heads * HEAD), BF16),
        scratch_shapes=[pltpu.VMEM((2, n_heads, tq, tk), F32), pltpu.VMEM((2, n_heads, tq, tk), BF16),
                        pltpu.VMEM((2, n_heads, tq, LANES), F32), stat, stat],
        compiler_params=_cparams(("arbitrary", "arbitrary")),
        name=name,
    )(*args)


def _pair_attn_kernel(q_ref, k_ref, v_ref, o_ref):
    tq = q_ref.shape[1]
    lo = _low_half((tq, LANES))
    for p in range(v_ref.shape[2] // LANES):
        sl = slice(LANES * p, LANES * (p + 1))
        qp, kp, vp = q_ref[0, :, sl], k_ref[0, :, sl], v_ref[0, :, sl]
        halves = []
        for half in range(2):
            qm = jnp.where(lo, qp, 0) if half == 0 else jnp.where(lo, 0, qp)
            s = _dot_nt(qm, kp)
            e = jnp.exp(s - jnp.max(s, axis=-1, keepdims=True))
            halves.append(_dot(e.astype(BF16), vp) / jnp.sum(e, axis=-1, keepdims=True))
        o_ref[0, :, sl] = jnp.where(lo, halves[0], halves[1]).astype(BF16)


def _pair_attention(q, k, v, *, tq, name):
    b, lq, w = q.shape
    lk = k.shape[1]
    return pl.pallas_call(
        _pair_attn_kernel,
        grid=(b, lq // tq),
        in_specs=[pl.BlockSpec((1, tq, w), lambda bi, i: (bi, i, 0)),
                  pl.BlockSpec((1, lk, w), lambda bi, i: (bi, 0, 0)),
                  pl.BlockSpec((1, lk, w), lambda bi, i: (bi, 0, 0))],
        out_specs=pl.BlockSpec((1, tq, w), lambda bi, i: (bi, i, 0)),
        out_shape=jax.ShapeDtypeStruct((b, lq, w), BF16),
        compiler_params=_cparams(("arbitrary", "arbitrary")),
        name=name,
    )(q, k, v)


def _gates(logits, bias):
    scores = _sigmoid(logits)
    sel = scores + bias
    lane = lax.broadcasted_iota(jnp.int32, logits.shape, 1)
    e4 = lane & (EXPERTS_PER_GROUP - 1)
    grp = (lane >> 2) & (N_GROUPS - 1)
    one, zero = jnp.float32(1.0), jnp.float32(0.0)

    rank = jnp.zeros(logits.shape, F32)
    for d in range(1, EXPERTS_PER_GROUP):
        other = _win_roll(sel, d, EXPERTS_PER_GROUP)
        other_first = ((e4 + d) & (EXPERTS_PER_GROUP - 1)) < e4
        ahead = jnp.logical_or(other > sel, jnp.logical_and(other == sel, other_first))
        rank = rank + jnp.where(ahead, one, zero)
    top2 = rank < float(TOP_K)

    def group_sum(a):
        tot = a
        for d in range(1, EXPERTS_PER_GROUP):
            tot = tot + _win_roll(a, d, EXPERTS_PER_GROUP)
        return tot

    gscore = group_sum(jnp.where(top2, sel, zero))
    beaten = jnp.zeros(logits.shape, F32)
    for d in range(1, N_GROUPS):
        other = _win_roll(gscore, d * EXPERTS_PER_GROUP, N_EXPERTS)
        other_first = ((grp + d) & (N_GROUPS - 1)) < grp
        ahead = jnp.logical_or(other > gscore, jnp.logical_and(other == gscore, other_first))
        beaten = beaten + jnp.where(ahead, one, zero)
    chosen = jnp.logical_and(jnp.logical_and(beaten < 0.5, top2), lane < N_EXPERTS)
    w = jnp.where(chosen, scores, zero)
    den = group_sum(w)
    return jnp.where(chosen, w / den, zero)


def _outproj_kernel(ret_ref, na_ref, mla_ref, h_ref, w_ref, g1_ref, sh_ref, sc_ref, n2_ref,
                    rw_ref, hn_ref, u_ref, logit_ref):
    mix = (_dot(ret_ref[0], w_ref[0:256, :]) + _dot(na_ref[0], w_ref[256:768, :])
           + _dot(mla_ref[0], w_ref[768:1024, :]))
    hn = h_ref[0] + g1_ref[0] * mix
    hn_ref[0] = hn
    u = _rms(hn, n2_ref[...]) * (1.0 + sc_ref[0]) + sh_ref[0]
    u_hi = u.astype(BF16)
    u_ref[0] = u_hi
    tm = u.shape[0]
    u_lo = (u - u_hi.astype(F32)).astype(BF16)
    both = _dot(jnp.concatenate([u_hi, u_lo], axis=0), rw_ref[...])
    logits = both[:tm, :LANES] + both[:tm, LANES:] + both[tm:, :LANES]
    logit_ref[0] = logits


def _out_projection(ret_o, na_o, mla_o, h, w_out, g1, sh2, sc2, n2, rw, tm):
    b, l, d = h.shape
    per_batch = g1.shape[0] > 1
    mod_map = (lambda i, j: (i, 0, 0)) if per_batch else (lambda i, j: (0, 0, 0))
    const2 = lambda i, j: (0, 0)
    tok = lambda w: pl.BlockSpec((1, tm, w), lambda i, j: (i, j, 0))
    mod = pl.BlockSpec((1, 1, d), mod_map)
    return pl.pallas_call(
        _outproj_kernel,
        grid=(b, l // tm),
        in_specs=[tok(256), tok(512), tok(256), tok(d),
                  pl.BlockSpec(w_out.shape, const2), mod, mod, mod,
                  pl.BlockSpec((1, d), const2),
                  pl.BlockSpec(rw.shape, const2)],
        out_specs=[tok(d), tok(d), tok(LANES)],
        out_shape=[jax.ShapeDtypeStruct((b, l, d), F32),
                   jax.ShapeDtypeStruct((b, l, d), BF16),
                   jax.ShapeDtypeStruct((b, l, LANES), F32)],
        compiler_params=_cparams(("arbitrary", "arbitrary")),
        name="out_projection",
    )(ret_o, na_o, mla_o, h, w_out, g1, sh2, sc2, n2, rw)


def _moe_kernel(u_ref, logit_ref, logit_next_ref, rb_ref, h_ref, w13_ref, w2_ref, g2_ref, fg_ref,
                o_ref, gate_scr, *, final_norm):
    u = u_ref[...]

    @pl.when(pl.program_id(0) == 0)
    def _():
        gate_scr[...] = _gates(logit_ref[...], rb_ref[...])

    gates = gate_scr[...]
    gates_next = _gates(logit_next_ref[...], rb_ref[...])
    lane = lax.broadcasted_iota(jnp.int32, gates.shape, 1)
    y = None
    for e in range(N_EXPERTS):
        a = _dot(u, w13_ref[e])
        a1, a3 = a[:, :EXPERT_FF], a[:, EXPERT_FF:]
        gcol = jnp.sum(jnp.where(lane == e, gates, 0.0), axis=-1, keepdims=True)
        hdn = ((a1 * _sigmoid(a1)) * a3 * gcol).astype(BF16)
        ye = _dot(hdn, w2_ref[e])
        y = ye if y is None else y + ye
    hn = h_ref[...] + g2_ref[0] * y
    if final_norm:
        hn = _rms(hn, fg_ref[...])
    o_ref[...] = hn
    gate_scr[...] = gates_next


def _moe(u, logits, rb, h, w13, w2, g2, fg, tokens_per_mod, tm, final_norm):
    t, d = h.shape
    per_batch = g2.shape[0] > 1
    tiles_per_mod = tokens_per_mod // tm
    n_tiles = t // tm
    mod_map = (lambda i: (i // tiles_per_mod, 0, 0)) if per_batch else (lambda i: (0, 0, 0))
    tok = lambda w: pl.BlockSpec((tm, w), lambda i: (i, 0))
    resident = lambda a: pl.BlockSpec(a.shape, lambda i: (0, 0, 0), pipeline_mode=pl.Buffered(1))
    return pl.pallas_call(
        functools.partial(_moe_kernel, final_norm=final_norm),
        grid=(n_tiles,),
        in_specs=[tok(d), tok(LANES),
                  pl.BlockSpec((tm, LANES), lambda i: (jnp.minimum(i + 1, n_tiles - 1), 0)),
                  pl.BlockSpec(rb.shape, lambda i: (0, 0)), tok(d),
                  resident(w13), resident(w2),
                  pl.BlockSpec((1, 1, d), mod_map),
                  pl.BlockSpec((1, d), lambda i: (0, 0))],
        out_specs=tok(d),
        out_shape=jax.ShapeDtypeStruct((t, d), F32),
        scratch_shapes=[pltpu.VMEM((tm, LANES), F32)],
        compiler_params=_cparams(("arbitrary",)),
        name="mixture_of_experts",
    )(u, logits, logits, rb, h, w13, w2, g2, fg)


def _rope_tables(seq, positional):
    if not positional:
        ones = jnp.ones((seq, LANES), F32)
        zeros = jnp.zeros((seq, LANES), F32)
        return ones, zeros, ones, zeros
    t = jnp.arange(seq)
    pos_r = (t // GRID_W).astype(F32)[:, None]
    pos_c = (t % GRID_W).astype(F32)[:, None]

    def axial(width):
        h = width // 2
        half = h // 2
        inv = ROPE_BASE ** (-jnp.arange(half, dtype=F32) / half)
        inv2 = jnp.concatenate([inv, inv])[None, :]
        ang = jnp.concatenate([pos_r * inv2, pos_c * inv2], axis=1)
        sign = jnp.tile(jnp.concatenate([-jnp.ones(half), jnp.ones(half)]), 2)[None, :]
        return jnp.cos(ang), jnp.sin(ang) * sign

    rc, rs = axial(RET_DK)
    ra = jnp.concatenate([rc, rc], axis=1)
    rb = jnp.concatenate([rs, rs], axis=1)
    mc, ms = axial(MLA_ROPE)
    ma = jnp.concatenate([jnp.ones((seq, MLA_NOPE), F32), mc, jnp.zeros((seq, 32), F32)], axis=1)
    mb = jnp.concatenate([jnp.zeros((seq, MLA_NOPE), F32), ms, jnp.zeros((seq, 32), F32)], axis=1)
    return ra, rb, ma, mb


def _pad_heads(w, heads, width):
    r = w.shape[0]
    w = w.reshape(r, heads, width)
    return jnp.pad(w, ((0, 0), (0, 0), (0, LANES - width))).reshape(r, heads * LANES)


def _layer_weights(w_in, w_uq, w_ukv):
    d = w_in.shape[0]
    kpe = w_in[:, _C_KPE:_C_KPE + MLA_ROPE]
    blk = jnp.concatenate([jnp.zeros((d, MLA_NOPE), w_in.dtype), kpe,
                           jnp.zeros((d, LANES - MLA_NOPE - MLA_ROPE), w_in.dtype)], axis=1)
    w_ext = jnp.concatenate([w_in[:, :_C_KPE], blk], axis=1).astype(BF16)
    wuq = _pad_heads(w_uq, MLA_HEADS, MLA_NOPE + MLA_ROPE).astype(BF16)
    ukv = w_ukv.reshape(w_ukv.shape[0], MLA_HEADS, MLA_NOPE + MLA_VD)
    wuk = _pad_heads(ukv[:, :, :MLA_NOPE].reshape(w_ukv.shape[0], -1), MLA_HEADS, MLA_NOPE).astype(BF16)
    wuv = _pad_heads(ukv[:, :, MLA_NOPE:].reshape(w_ukv.shape[0], -1), MLA_HEADS, MLA_VD).astype(BF16)
    return w_ext, wuq, wuk, wuv


def _latent_key_tile(n_keys, n_ctx):
    for tk in range(MLA_MAX_KEY_TILE, 0, -LANES):
        if tk >= n_ctx and n_keys % tk == 0 and (n_keys // tk) % 2 == 0:
            return tk
    raise ValueError(f"no even tiling of {n_keys} keys")


def kernel(x, c, ctx, c_ctx, w_mod, b_mod, norm1_g, norm2_g, w_in, ret_decay_f, ret_decay_b, na_rpb,
           mla_q_norm, mla_kv_norm, w_uq, w_ukv, w_out, router_w, router_b, w1, w3, w2, final_norm_g):
    b, s, d = x.shape
    lc = ctx.shape[1]
    depth = w_mod.shape[0]
    rows = s // GRID_W
    tm_lat = TOKEN_TILE
    tm_ctx = min(CTX_TILE, lc)
    q_tiles = MLA_Q_TILES if s % (MLA_Q_TILE * MLA_Q_TILES) == 0 else 1

    n_mod = -(-(b + 1) // 8) * 8
    cv = jnp.concatenate([c, c_ctx[None, :], jnp.zeros((n_mod - b - 1, d), F32)], axis=0)
    mods = _modulation(cv, w_mod, b_mod).reshape(depth, n_mod, 6, d)

    tabs_lat = _rope_tables(s, True)
    tabs_ctx = _rope_tables(lc, False)
    na_plan = _na_bias_plan(rows)
    rw_f32 = jnp.pad(router_w.astype(F32), ((0, 0), (0, LANES - N_EXPERTS)))
    rw_hi = rw_f32.astype(BF16)
    rw = jnp.concatenate([rw_hi, (rw_f32 - rw_hi.astype(F32)).astype(BF16)], axis=1)
    rb = jnp.pad(router_b.astype(F32), (0, LANES - N_EXPERTS)).reshape(1, LANES)
    zero_state = (jnp.zeros((b, 2 * LANES, LANES), F32),) * 2

    h, hc = x, ctx
    for l in range(depth):
        last = l == depth - 1
        m_lat = [mods[l, :b, i][:, None, :] for i in range(6)]
        m_ctx = [mods[l, b:b + 1, i][:, None, :] for i in range(6)]
        w_ext, wuq, wuk, wuv = _layer_weights(w_in[l], w_uq[l], w_ukv[l])
        n1 = norm1_g[l].reshape(1, d)
        n2 = norm2_g[l].reshape(1, d)
        qn = mla_q_norm[l].reshape(1, -1)
        kvn = mla_kv_norm[l].reshape(1, -1)
        wo = w_out[l].astype(BF16)
        w13 = jnp.concatenate([w1[l], w3[l]], axis=-1).astype(BF16)
        w2b = w2[l].astype(BF16)
        fg = final_norm_g.reshape(1, d)

        ret_c, na_c, mq_c, mk_c, mv_c = _in_projection(
            hc, m_ctx[0], m_ctx[1], n1, w_ext, qn, kvn, wuq, wuk, wuv, tabs_ctx, tm_ctx)
        ret_l, na_l, mq_l, mk_l, mv_l = _in_projection(
            h, m_lat[0], m_lat[1], n1, w_ext, qn, kvn, wuq, wuk, wuv, tabs_lat, tm_lat)

        rtabs = _retention_tables(ret_decay_f[l], ret_decay_b[l])
        ro_c, st = _retention(ret_c, rtabs, zero_state)
        ro_l, _ = _retention(ret_l, rtabs, st)

        na_o = _na_attention(na_l, na_c, _na_bias(na_rpb[l], na_plan))
        mla_o = _latent_attention(mq_l, mk_l, mv_l, (mk_c, mv_c), tq=MLA_Q_TILE, q_tiles=q_tiles,
                                  tk=_latent_key_tile(s, lc), name="latent_attention")

        hn, u2, logits = _out_projection(ro_l, na_o, mla_o, h, wo, m_lat[2], m_lat[3], m_lat[4],
                                         n2, rw, tm_lat)
        h = _moe(u2.reshape(b * s, d), logits.reshape(b * s, LANES), rb, hn.reshape(b * s, d),
                 w13, w2b, m_lat[5], fg, s, TOKEN_TILE, last).reshape(b, s, d)

        if not last:
            nac_o = _pair_attention(na_c[:, :, 0:512], na_c[:, :, 512:1024], na_c[:, :, 1024:1536],
                                    tq=tm_ctx, name="context_na_attention")
            mlac_o = _latent_attention(mq_c, mk_c, mv_c, tq=tm_ctx, q_tiles=1, tk=lc,
                                       name="context_latent_attention")
            hcn, uc2, logits_c = _out_projection(ro_c, nac_o, mlac_o, hc, wo, m_ctx[2], m_ctx[3],
                                                 m_ctx[4], n2, rw, tm_ctx)
            hc = _moe(uc2.reshape(b * lc, d), logits_c.reshape(b * lc, LANES), rb,
                      hcn.reshape(b * lc, d), w13, w2b, m_ctx[5], fg, lc, min(TOKEN_TILE, lc),
                      False).reshape(b, lc, d)
    return h
```

```python
import functools

import numpy as np
import jax
import jax.numpy as jnp
from jax import lax
from jax.experimental import pallas as pl
from jax.experimental.pallas import tpu as pltpu

F32 = jnp.float32
BF16 = jnp.bfloat16

GRID_W = 64
EPS = 1e-6
ROPE_BASE = 10000.0
RET_HEADS, RET_DK, RET_DV, RET_CHUNK = 4, 64, 64, 128
NA_HEADS, NA_HD, NA_KH, NA_KW, NA_QBLOCK = 8, 64, 8, 16, 128
MLA_HEADS, MLA_NOPE, MLA_ROPE, MLA_VD = 4, 64, 32, 64
MLA_Q_RANK, MLA_KV_RANK = 384, 256
N_EXPERTS, N_GROUPS, TOP_K, EXPERT_FF = 16, 4, 2, 256
EXPERTS_PER_GROUP = N_EXPERTS // N_GROUPS

LANES = 128
HEAD = 64
TOKEN_TILE = 512
CTX_TILE = 256
NA_WIN_BLOCKS = 5
MLA_Q_TILE = 256
MLA_Q_TILES = 4
MLA_MAX_KEY_TILE = 1024
NA_SUB = 2
NA_STEP_BLOCKS = NA_WIN_BLOCKS + NA_SUB - 1
RET_GROUP = 8
NEG_BIG = -1e30
VMEM_LIMIT = 48 * 1024 * 1024

_C_RET, _C_NA, _C_CQ, _C_CKV, _C_KPE = 0, 1024, 2560, 2944, 3200
_W_EXT = _C_KPE + LANES
MLA_Q_SCALE = (MLA_NOPE + MLA_ROPE) ** -0.5 * float(np.log2(np.e))


def _cparams(sem):
    return pltpu.CompilerParams(dimension_semantics=sem, vmem_limit_bytes=VMEM_LIMIT)


def _dot(a, b):
    return jnp.dot(a, b, preferred_element_type=F32)


def _dot_nt(a, b):
    return lax.dot_general(a, b, (((1,), (1,)), ((), ())), preferred_element_type=F32)


def _sigmoid(x):
    return 1.0 / (1.0 + jnp.exp(-x))


def _win_roll(x, s, w):
    n = x.shape[-1]
    ax = x.ndim - 1
    lane = lax.broadcasted_iota(jnp.int32, x.shape, ax)
    fwd = pltpu.roll(x, n - s, ax)
    bwd = pltpu.roll(x, w - s, ax)
    return jnp.where((lane & (w - 1)) + s < w, fwd, bwd)


def _low_half(shape):
    lane = lax.broadcasted_iota(jnp.int32, shape, len(shape) - 1)
    return (lane & (LANES - 1)) < HEAD


def _mod_kernel(cv_ref, w_ref, b_ref, o_ref):
    cv = cv_ref[...]
    a = cv * _sigmoid(cv)
    o_ref[0] = jnp.dot(a, w_ref[0], preferred_element_type=F32,
                       precision=lax.Precision.HIGHEST) + b_ref[0]


def _modulation(cv, w_mod, b_mod):
    depth, d, n6 = w_mod.shape
    rows = cv.shape[0]
    tn = 1536
    return pl.pallas_call(
        _mod_kernel,
        grid=(depth, n6 // tn),
        in_specs=[pl.BlockSpec((rows, d), lambda l, j: (0, 0)),
                  pl.BlockSpec((1, d, tn), lambda l, j: (l, 0, j)),
                  pl.BlockSpec((1, 1, tn), lambda l, j: (l, 0, j))],
        out_specs=pl.BlockSpec((1, rows, tn), lambda l, j: (l, 0, j)),
        out_shape=jax.ShapeDtypeStruct((depth, rows, n6), F32),
        compiler_params=_cparams(("arbitrary", "arbitrary")),
        name="modulation",
    )(cv, w_mod, b_mod.reshape(depth, 1, n6))


def _rms(x, g):
    return x * lax.rsqrt(jnp.mean(x * x, axis=-1, keepdims=True) + EPS) * g


def _inproj_kernel(h_ref, sh_ref, sc_ref, g_ref, w_ref, qn_ref, kvn_ref, wuq_ref, wuk_ref, wuv_ref,
                   ra_ref, rb_ref, ma_ref, mb_ref,
                   ret_ref, na_ref, q_ref, k_ref, v_ref):
    x = h_ref[0]
    u = _rms(x, g_ref[...]) * (1.0 + sc_ref[0]) + sh_ref[0]
    ub = u.astype(BF16)

    r = _dot(ub, w_ref[:, _C_RET:_C_NA])
    qk = r[:, :512]
    ra = jnp.concatenate([ra_ref[...]] * 4, axis=1)
    rb = jnp.concatenate([rb_ref[...]] * 4, axis=1)
    qk = qk * ra + _win_roll(qk, 16, 32) * rb
    ret_ref[0, :, 0:256] = qk[:, :256] * (RET_DK ** -0.5)
    ret_ref[0, :, 256:512] = qk[:, 256:]
    ret_ref[0, :, 512:1024] = r[:, 512:]

    n = _dot(ub, w_ref[:, _C_NA:_C_CQ])
    na_ref[0, :, 0:512] = (n[:, :512] * (NA_HD ** -0.5)).astype(BF16)
    na_ref[0, :, 512:1536] = n[:, 512:].astype(BF16)

    ma = jnp.concatenate([ma_ref[...]] * 4, axis=1)
    mb = jnp.concatenate([mb_ref[...]] * 4, axis=1)
    cq = _rms(_dot(ub, w_ref[:, _C_CQ:_C_CKV]), qn_ref[...]).astype(BF16)
    q = _dot(cq, wuq_ref[...])
    q = q * ma + _win_roll(q, 8, 16) * mb
    q_ref[0] = (q * MLA_Q_SCALE).astype(BF16)
    ckv = _rms(_dot(ub, w_ref[:, _C_CKV:_C_KPE]), kvn_ref[...]).astype(BF16)
    kp = _dot(ub, w_ref[:, _C_KPE:_W_EXT])
    kp = kp * ma_ref[...] + _win_roll(kp, 8, 16) * mb_ref[...]
    kp = jnp.concatenate([kp] * MLA_HEADS, axis=1)
    k_ref[0] = (_dot(ckv, wuk_ref[...]) + kp).astype(BF16)
    v = _dot(ckv, wuv_ref[...])
    lane = lax.broadcasted_iota(jnp.int32, v.shape, 1)
    v_ref[0] = jnp.where((lane & (LANES - 1)) == HEAD, 1.0, v).astype(BF16)


def _in_projection(h, sh, sc, g, w_ext, qn, kvn, wuq, wuk, wuv, tabs, tm):
    b, l, d = h.shape
    per_batch = sh.shape[0] > 1
    mod_map = (lambda i, j: (i, 0, 0)) if per_batch else (lambda i, j: (0, 0, 0))
    const2 = lambda i, j: (0, 0)
    tab_spec = pl.BlockSpec((tm, LANES), lambda i, j: (j, 0))
    tok = lambda w: pl.BlockSpec((1, tm, w), lambda i, j: (i, j, 0))
    return pl.pallas_call(
        _inproj_kernel,
        grid=(b, l // tm),
        in_specs=[tok(d),
                  pl.BlockSpec((1, 1, d), mod_map), pl.BlockSpec((1, 1, d), mod_map),
                  pl.BlockSpec((1, d), const2),
                  pl.BlockSpec(w_ext.shape, const2),
                  pl.BlockSpec(qn.shape, const2), pl.BlockSpec(kvn.shape, const2),
                  pl.BlockSpec(wuq.shape, const2), pl.BlockSpec(wuk.shape, const2),
                  pl.BlockSpec(wuv.shape, const2),
                  tab_spec, tab_spec, tab_spec, tab_spec],
        out_specs=[tok(1024), tok(1536), tok(512), tok(512), tok(512)],
        out_shape=[jax.ShapeDtypeStruct((b, l, 1024), F32),
                   jax.ShapeDtypeStruct((b, l, 1536), BF16),
                   jax.ShapeDtypeStruct((b, l, 512), BF16),
                   jax.ShapeDtypeStruct((b, l, 512), BF16),
                   jax.ShapeDtypeStruct((b, l, 512), BF16)],
        compiler_params=_cparams(("arbitrary", "arbitrary")),
        name="in_projection",
    )(h, sh, sc, g, w_ext, qn, kvn, wuq, wuk, wuv, *tabs)


def _retention_kernel(*refs, backward, group):
    if backward:
        (q_ref, k_ref, v_ref, g_ref, of_ref, dm_ref, qw_ref, kw_ref, gc_ref, s0_ref,
         o_ref, sfin_ref, state) = refs
    else:
        q_ref, k_ref, v_ref, dm_ref, qw_ref, kw_ref, gc_ref, s0_ref, o_ref, sfin_ref, state = refs
    step = pl.program_id(1)
    c = RET_CHUNK
    n_pairs = RET_HEADS // 2

    @pl.when(step == 0)
    def _():
        state[...] = s0_ref[0]

    lo = _low_half((c, LANES))
    row_lo = lax.broadcasted_iota(jnp.int32, (LANES, LANES), 0) < HEAD
    same_head = row_lo == _low_half((LANES, LANES))
    qw, kw = qw_ref[...], kw_ref[...]

    intra, kv, qd = {}, {}, {}
    for ci in range(group):
        rows = slice(c * ci, c * (ci + 1))
        for p in range(n_pairs):
            sl = slice(LANES * p, LANES * (p + 1))
            qp, kp = q_ref[0, rows, sl], k_ref[0, rows, sl]
            kb, vb = kp.astype(BF16), v_ref[0, rows, sl].astype(BF16)
            qab = jnp.concatenate([jnp.where(lo, qp, 0.0), jnp.where(lo, 0.0, qp)], axis=0).astype(BF16)
            sab = (_dot_nt(qab, kb) * dm_ref[p]).astype(BF16)
            oab = _dot(sab, vb)
            intra[ci, p] = jnp.where(lo, oab[:c], oab[c:])
            kv[ci, p] = jnp.where(same_head, _dot((kp * kw[:, sl]).T.astype(BF16), vb), 0.0)
            qd[ci, p] = (qp * qw[:, sl]).astype(BF16)

    order = range(group - 1, -1, -1) if backward else range(group)
    out = {}
    for p in range(n_pairs):
        sl = slice(LANES * p, LANES * (p + 1))
        sp = state[sl, :]
        gc = gc_ref[sl, :]
        for ci in order:
            out[ci, p] = intra[ci, p] + _dot(qd[ci, p], sp.astype(BF16))
            sp = gc * sp + kv[ci, p]
        state[sl, :] = sp

    for ci in range(group):
        rows = slice(c * ci, c * (ci + 1))
        o = jnp.concatenate([out[ci, p] for p in range(n_pairs)], axis=1)
        if not backward:
            o_ref[0, rows, :] = o
            continue
        tot = of_ref[0, rows, :] + o
        g = g_ref[0, rows, :]
        normed = []
        for p in range(n_pairs):
            t = tot[:, LANES * p:LANES * (p + 1)]

            def head_mean(a):
                s_lo = jnp.sum(jnp.where(lo, a, 0.0), axis=-1, keepdims=True)
                s_hi = jnp.sum(jnp.where(lo, 0.0, a), axis=-1, keepdims=True)
                return jnp.where(lo, s_lo, s_hi) * (1.0 / HEAD)

            dlt = t - head_mean(t)
            normed.append(dlt * lax.rsqrt(head_mean(dlt * dlt) + EPS))
        on = jnp.concatenate(normed, axis=1)
        o_ref[0, rows, :] = (on * (g * _sigmoid(g))).astype(BF16)

    @pl.when(step == pl.num_programs(1) - 1)
    def _():
        sfin_ref[0] = state[...]


def _retention_sweep(ret, o_fwd, tabs, s0, *, backward):
    b, l, _ = ret.shape
    c = RET_CHUNK
    group = min(RET_GROUP, l // c)
    n = l // (c * group)
    gidx = (lambda s: n - 1 - s) if backward else (lambda s: s)
    tok = lambda col: pl.BlockSpec((1, c * group, 256), lambda i, s: (i, gidx(s), col))
    const = lambda a: pl.BlockSpec(a.shape, lambda i, s: (0,) * a.ndim)
    state_spec = pl.BlockSpec((1, 2 * LANES, LANES), lambda i, s: (i, 0, 0))
    dm, qw, kw, gc = tabs
    if backward:
        in_specs = [tok(0), tok(1), tok(2), tok(3), tok(0)]
        args = [ret, ret, ret, ret, o_fwd]
    else:
        in_specs = [tok(0), tok(1), tok(2)]
        args = [ret, ret, ret]
    return pl.pallas_call(
        functools.partial(_retention_kernel, backward=backward, group=group),
        grid=(b, n),
        in_specs=in_specs + [const(dm), const(qw), const(kw), const(gc), state_spec],
        out_specs=[tok(0), state_spec],
        out_shape=[jax.ShapeDtypeStruct((b, l, 256), BF16 if backward else F32),
                   jax.ShapeDtypeStruct((b, 2 * LANES, LANES), F32)],
        scratch_shapes=[pltpu.VMEM((2 * LANES, LANES), F32)],
        compiler_params=_cparams(("arbitrary", "arbitrary")),
        name="retention_backward" if backward else "retention_forward",
    )(*args, dm, qw, kw, gc, s0)


def _retention(ret, tabs, s0):
    fwd_tabs = tuple(t[0] for t in tabs)
    bwd_tabs = tuple(t[1] for t in tabs)
    o_f, s_f = _retention_sweep(ret, None, fwd_tabs, s0[0], backward=False)
    o, s_b = _retention_sweep(ret, o_f, bwd_tabs, s0[1], backward=True)
    return o, (s_f, s_b)


def _retention_tables(decay_f, decay_b):
    c = RET_CHUNK
    pos = jnp.arange(c, dtype=F32)
    diff = pos[:, None] - pos[None, :]
    lg = jnp.stack([jax.nn.log_sigmoid(decay_f.astype(F32)), jax.nn.log_sigmoid(decay_b.astype(F32))])
    lgh = lg[:, :, None, None]
    dm_f = jnp.where(diff >= 0, jnp.exp(lgh[0] * jnp.maximum(diff, 0.0)), 0.0)
    dm_b = jnp.where(diff < 0, jnp.exp(lgh[1] * jnp.maximum(-diff, 0.0)), 0.0)
    dm = jnp.stack([dm_f, dm_b]).reshape(2, RET_HEADS // 2, 2 * c, c)
    lanes = lambda t: jnp.repeat(t, RET_DK, axis=-1)
    lgc = lg[:, None, :]
    qw = lanes(jnp.exp(lgc * jnp.stack([pos + 1.0, c - pos])[:, :, None]))
    kw = lanes(jnp.exp(lgc * jnp.stack([c - 1.0 - pos, pos])[:, :, None]))
    gch = jnp.exp(lg * c)
    gc = jnp.broadcast_to(jnp.repeat(gch, RET_DK, axis=-1)[:, :, None], (2, RET_HEADS * RET_DK, LANES))
    return dm, qw, kw, gc


def _na_kernel(*refs, n_ctx):
    w = NA_STEP_BLOCKS
    q_ref = refs[0]
    k_refs, v_refs = refs[1:1 + w], refs[1 + w:1 + 2 * w]
    kc_ref, vc_ref, bias_ref, o_ref, s_scr, p_scr, den_scr = refs[1 + 2 * w:]
    tq = q_ref.shape[1]
    n_loc = w * NA_QBLOCK
    n_lane_blocks = (n_loc + n_ctx) // LANES
    lo = _low_half((tq, LANES))
    for h in range(NA_HEADS):
        sl = slice(LANES * (h // 2), LANES * (h // 2 + 1))
        qp = q_ref[0, :, sl]
        qm = jnp.where(lo, qp, 0) if h % 2 == 0 else jnp.where(lo, 0, qp)
        kwin = jnp.concatenate([r[0, :, sl] for r in k_refs], axis=0)
        s_scr[h, :, 0:n_loc] = _dot_nt(qm, kwin) + bias_ref[0, h]
        s_scr[h, :, n_loc:] = _dot_nt(qm, kc_ref[0, :, sl])
    for h in range(NA_HEADS):
        part = s_scr[h, :, 0:LANES]
        for c in range(1, n_lane_blocks):
            part = jnp.maximum(part, s_scr[h, :, LANES * c:LANES * (c + 1)])
        m = jnp.max(part, axis=-1, keepdims=True)
        tot = None
        for c in range(n_lane_blocks):
            cs = slice(LANES * c, LANES * (c + 1))
            e = jnp.exp(s_scr[h, :, cs] - m)
            tot = e if tot is None else tot + e
            p_scr[h, :, cs] = e.astype(BF16)
        den_scr[h] = jnp.broadcast_to(jnp.sum(tot, axis=-1, keepdims=True), (tq, LANES))
    for p in range(NA_HEADS // 2):
        sl = slice(LANES * p, LANES * (p + 1))
        vwin = jnp.concatenate([r[0, :, sl] for r in v_refs], axis=0)
        halves = []
        for h in (2 * p, 2 * p + 1):
            o = _dot(p_scr[h, :, 0:n_loc], vwin) + _dot(p_scr[h, :, n_loc:], vc_ref[0, :, sl])
            halves.append(o / den_scr[h])
        o_ref[0, :, sl] = jnp.where(lo, halves[0], halves[1]).astype(BF16)


def _na_step_variant(i, n_steps):
    return jnp.where(i == 0, 0, jnp.where(i == n_steps - 1, 2, 1))


def _na_attention(na, na_ctx, bias):
    b, l, _ = na.shape
    n_ctx = na_ctx.shape[1]
    nblk = l // NA_QBLOCK
    n_steps = nblk // NA_SUB
    tq = NA_SUB * NA_QBLOCK
    n_loc = NA_STEP_BLOCKS * NA_QBLOCK
    n_keys = n_loc + n_ctx

    def win(col, j):
        return pl.BlockSpec((1, NA_QBLOCK, 512), lambda bi, i: (
            bi, jnp.clip(NA_SUB * i - 2, 0, nblk - NA_STEP_BLOCKS) + j, col))

    in_specs = ([pl.BlockSpec((1, tq, 512), lambda bi, i: (bi, i, 0))]
                + [win(1, j) for j in range(NA_STEP_BLOCKS)]
                + [win(2, j) for j in range(NA_STEP_BLOCKS)]
                + [pl.BlockSpec((1, n_ctx, 512), lambda bi, i: (bi, 0, 1)),
                   pl.BlockSpec((1, n_ctx, 512), lambda bi, i: (bi, 0, 2)),
                   pl.BlockSpec((1, NA_HEADS, tq, n_loc),
                                lambda bi, i: (_na_step_variant(i, n_steps), 0, 0, 0))])
    args = [na] * (1 + 2 * NA_STEP_BLOCKS) + [na_ctx, na_ctx, bias]
    return pl.pallas_call(
        functools.partial(_na_kernel, n_ctx=n_ctx),
        grid=(b, n_steps),
        in_specs=in_specs,
        out_specs=pl.BlockSpec((1, tq, 512), lambda bi, i: (bi, i, 0)),
        out_shape=jax.ShapeDtypeStruct((b, l, 512), BF16),
        scratch_shapes=[pltpu.VMEM((NA_HEADS, tq, n_keys), F32),
                        pltpu.VMEM((NA_HEADS, tq, n_keys), BF16),
                        pltpu.VMEM((NA_HEADS, tq, LANES), F32)],
        compiler_params=_cparams(("arbitrary", "arbitrary")),
        name="neighbourhood_attention",
    )(*args)


def _na_bias_plan(rows):
    nblk = rows * GRID_W // NA_QBLOCK
    kh = min(NA_KH, rows)
    q_rows = NA_QBLOCK // GRID_W
    k_rows = NA_WIN_BLOCKS * q_rows
    plan = []
    for i in (0, 1, 2, nblk - 2, nblk - 1):
        kb = min(max(i - 2, 0), nblk - NA_WIN_BLOCKS)
        per_q = []
        for rr in range(q_rows):
            r = q_rows * i + rr
            r0 = min(max(r - kh // 2, 0), rows - kh)
            per_q.append([(q_rows * kb + kj - r + NA_KH - 1) if r0 <= q_rows * kb + kj < r0 + kh else None
                          for kj in range(k_rows)])
        plan.append(per_q)
    return plan


def _na_bias(rpb, plan):
    c = np.arange(GRID_W)
    c0 = np.clip(c - NA_KW // 2, 0, GRID_W - NA_KW)
    kc = np.arange(GRID_W)
    ok = (kc[None, :] >= c0[:, None]) & (kc[None, :] < c0[:, None] + NA_KW)
    reach = GRID_W - NA_KW
    padded = jnp.pad(rpb.astype(F32), ((0, 0), (0, 0), (reach, reach)))
    cols = jnp.stack([padded[:, :, GRID_W - 1 - ci:2 * GRID_W - 1 - ci] for ci in range(GRID_W)], axis=2)
    cols = jnp.where(jnp.asarray(ok)[None, None], cols, NEG_BIG)
    outside = jnp.full((NA_HEADS, GRID_W, GRID_W), NEG_BIG, F32)
    variants = []
    for per_q in plan:
        q_parts = [jnp.concatenate([outside if dr is None else cols[:, dr] for dr in per_k], axis=-1)
                   for per_k in per_q]
        variants.append(jnp.concatenate(q_parts, axis=1))

    hidden = jnp.full((NA_HEADS, NA_QBLOCK, NA_QBLOCK), NEG_BIG, F32)

    def place(blk, off):
        return jnp.concatenate([hidden] * off + [blk] + [hidden] * (1 - off), axis=-1)

    steps = [(0, 0, 1, 0), (2, 0, 2, 1), (3, 1, 4, 1)]
    return jnp.stack([jnp.concatenate([place(variants[va], oa), place(variants[vb], ob)], axis=1)
                      for va, oa, vb, ob in steps])


def _latent_attn_kernel(*refs, tk, q_tiles, n_ctx):
    if n_ctx:
        q_ref, k_ref, v_ref, kc_ref, vc_ref, o_ref, s_scr, p_scr, al_scr, m_scr, acc_scr = refs
    else:
        q_ref, k_ref, v_ref, o_ref, s_scr, p_scr, al_scr, m_scr, acc_scr = refs
    n_k = k_ref.shape[1] // tk + (1 if n_ctx else 0)
    n_heads = v_ref.shape[2] // LANES
    tq = q_ref.shape[1] // q_tiles
    assert n_k % 2 == 1, "static slot parities below assume an odd number of key tiles"

    def is_ctx(j):
        return bool(n_ctx) and isinstance(j, int) and j == n_k - 1

    def key_rows(j):
        return pl.ds(pl.multiple_of(j * tk, tk), tk)

    def slot_of(a, j_parity):
        return (a * n_k + j_parity) % 2

    def scores(a, j, slot):
        for h in range(n_heads):
            sl = slice(LANES * h, LANES * (h + 1))
            qh = q_ref[0, tq * a:tq * (a + 1), sl]
            if is_ctx(j):
                s_scr[slot, h, :, 0:n_ctx] = _dot_nt(qh, kc_ref[0, :, sl])
            else:
                s_scr[slot, h] = _dot_nt(qh, k_ref[0, key_rows(j), sl])

    def softmax(a, slot, ctx_tile=False):
        n_lane_blocks = (n_ctx if ctx_tile else tk) // LANES
        for h in range(n_heads):
            part = s_scr[slot, h, :, 0:LANES]
            for c in range(1, n_lane_blocks):
                part = jnp.maximum(part, s_scr[slot, h, :, LANES * c:LANES * (c + 1)])
            m_old = m_scr[a, h]
            m_new = jnp.maximum(m_old, jnp.max(part, axis=-1, keepdims=True))
            al_scr[slot, h] = jnp.exp2(m_old - m_new)
            m_scr[a, h] = m_new
            for c in range(n_lane_blocks):
                cs = slice(LANES * c, LANES * (c + 1))
                p_scr[slot, h, :, cs] = jnp.exp2(s_scr[slot, h, :, cs] - m_new).astype(BF16)

    def values(a, j, slot):
        for h in range(n_heads):
            sl = slice(LANES * h, LANES * (h + 1))
            if is_ctx(j):
                pv = _dot(p_scr[slot, h, :, 0:n_ctx], vc_ref[0, :, sl])
            else:
                pv = _dot(p_scr[slot, h], v_ref[0, key_rows(j), sl])
            acc_scr[a, h] = al_scr[slot, h] * acc_scr[a, h] + pv

    def time_step(nxt, cur, prv):
        if nxt is not None:
            scores(*nxt)
        softmax(*cur)
        if prv is not None:
            values(*prv)

    m_scr[...] = jnp.full(m_scr.shape, -jnp.inf, F32)
    acc_scr[...] = jnp.zeros(acc_scr.shape, F32)
    scores(0, 0, 0)
    for a in range(q_tiles):
        s_even, s_odd = slot_of(a, 0), slot_of(a, 1)
        first_of_next = (a + 1, 0, slot_of(a + 1, 0)) if a + 1 < q_tiles else None
        last_of_prev = (a - 1, n_k - 1, slot_of(a - 1, 0)) if a > 0 else None
        if n_k == 1:
            time_step(first_of_next, (a, s_even), last_of_prev)
            continue
        time_step((a, 1, s_odd), (a, s_even), last_of_prev)

        def body(i, carry, a=a, s_even=s_even, s_odd=s_odd):
            j = 2 * i + 1
            time_step((a, j + 1, s_even), (a, s_odd), (a, j - 1, s_even))
            time_step((a, j + 2, s_odd), (a, s_even), (a, j, s_odd))
            return carry

        lax.fori_loop(0, (n_k - 3) // 2, body, 0)
        time_step((a, n_k - 1, s_even), (a, s_odd), (a, n_k - 3, s_even))
        time_step(first_of_next, (a, s_even, bool(n_ctx)), (a, n_k - 2, s_odd))
    values(q_tiles - 1, n_k - 1, slot_of(q_tiles - 1, 0))
    for a in range(q_tiles):
        outs = []
        for h in range(n_heads):
            acc = acc_scr[a, h]
            outs.append(acc[:, 0:HEAD] / acc[:, HEAD:HEAD + 1])
        o_ref[0, tq * a:tq * (a + 1), :] = jnp.concatenate(outs, axis=1).astype(BF16)


def _latent_attention(q, k, v, ctx_kv=None, *, tq, q_tiles, tk, name):
    b, lq, w = q.shape
    lk = k.shape[1]
    n_heads = w // LANES
    rows = tq * q_tiles
    n_ctx = 0 if ctx_kv is None else ctx_kv[0].shape[1]
    resident = lambda n: pl.BlockSpec((1, n, w), lambda bi, i: (bi, 0, 0), pipeline_mode=pl.Buffered(1))
    stat = pltpu.VMEM((q_tiles, n_heads, tq, LANES), F32)
    in_specs = [pl.BlockSpec((1, rows, w), lambda bi, i: (bi, i, 0)), resident(lk), resident(lk)]
    args = [q, k, v]
    if n_ctx:
        assert n_ctx <= tk and n_ctx % LANES == 0
        in_specs += [resident(n_ctx), resident(n_ctx)]
        args += list(ctx_kv)
    return pl.pallas_call(
        functools.partial(_latent_attn_kernel, tk=tk, q_tiles=q_tiles, n_ctx=n_ctx),
        grid=(b, lq // rows),
        in_specs=in_specs,
        out_specs=pl.BlockSpec((1, rows, n_heads * HEAD), lambda bi, i: (bi, i, 0)),
        out_shape=jax.ShapeDtypeStruct((b, lq, n_heads * HEAD), BF16),
        scratch_shapes=[pltpu.VMEM((2, n_heads, tq, tk), F32), pltpu.VMEM((2, n_heads, tq, tk), BF16),
                        pltpu.VMEM((2, n_heads, tq, LANES), F32), stat, stat],
        compiler_params=_cparams(("arbitrary", "arbitrary")),
        name=name,
    )(*args)


def _pair_attn_kernel(q_ref, k_ref, v_ref, o_ref):
    tq = q_ref.shape[1]
    lo = _low_half((tq, LANES))
    for p in range(v_ref.shape[2] // LANES):
        sl = slice(LANES * p, LANES * (p + 1))
        qp, kp, vp = q_ref[0, :, sl], k_ref[0, :, sl], v_ref[0, :, sl]
        halves = []
        for half in range(2):
            qm = jnp.where(lo, qp, 0) if half == 0 else jnp.where(lo, 0, qp)
            s = _dot_nt(qm, kp)
            e = jnp.exp(s - jnp.max(s, axis=-1, keepdims=True))
            halves.append(_dot(e.astype(BF16), vp) / jnp.sum(e, axis=-1, keepdims=True))
        o_ref[0, :, sl] = jnp.where(lo, halves[0], halves[1]).astype(BF16)


def _pair_attention(q, k, v, *, tq, name):
    b, lq, w = q.shape
    lk = k.shape[1]
    return pl.pallas_call(
        _pair_attn_kernel,
        grid=(b, lq // tq),
        in_specs=[pl.BlockSpec((1, tq, w), lambda bi, i: (bi, i, 0)),
                  pl.BlockSpec((1, lk, w), lambda bi, i: (bi, 0, 0)),
                  pl.BlockSpec((1, lk, w), lambda bi, i: (bi, 0, 0))],
        out_specs=pl.BlockSpec((1, tq, w), lambda bi, i: (bi, i, 0)),
        out_shape=jax.ShapeDtypeStruct((b, lq, w), BF16),
        compiler_params=_cparams(("arbitrary", "arbitrary")),
        name=name,
    )(q, k, v)


def _gates(logits, bias):
    scores = _sigmoid(logits)
    sel = scores + bias
    lane = lax.broadcasted_iota(jnp.int32, logits.shape, 1)
    e4 = lane & (EXPERTS_PER_GROUP - 1)
    grp = (lane >> 2) & (N_GROUPS - 1)
    one, zero = jnp.float32(1.0), jnp.float32(0.0)

    rank = jnp.zeros(logits.shape, F32)
    for d in range(1, EXPERTS_PER_GROUP):
        other = _win_roll(sel, d, EXPERTS_PER_GROUP)
        other_first = ((e4 + d) & (EXPERTS_PER_GROUP - 1)) < e4
        ahead = jnp.logical_or(other > sel, jnp.logical_and(other == sel, other_first))
        rank = rank + jnp.where(ahead, one, zero)
    top2 = rank < float(TOP_K)

    def group_sum(a):
        tot = a
        for d in range(1, EXPERTS_PER_GROUP):
            tot = tot + _win_roll(a, d, EXPERTS_PER_GROUP)
        return tot

    gscore = group_sum(jnp.where(top2, sel, zero))
    beaten = jnp.zeros(logits.shape, F32)
    for d in range(1, N_GROUPS):
        other = _win_roll(gscore, d * EXPERTS_PER_GROUP, N_EXPERTS)
        other_first = ((grp + d) & (N_GROUPS - 1)) < grp
        ahead = jnp.logical_or(other > gscore, jnp.logical_and(other == gscore, other_first))
        beaten = beaten + jnp.where(ahead, one, zero)
    chosen = jnp.logical_and(jnp.logical_and(beaten < 0.5, top2), lane < N_EXPERTS)
    w = jnp.where(chosen, scores, zero)
    den = group_sum(w)
    return jnp.where(chosen, w / den, zero)


def _outproj_kernel(ret_ref, na_ref, mla_ref, h_ref, w_ref, g1_ref, sh_ref, sc_ref, n2_ref,
                    rw_ref, hn_ref, u_ref, logit_ref):
    mix = (_dot(ret_ref[0], w_ref[0:256, :]) + _dot(na_ref[0], w_ref[256:768, :])
           + _dot(mla_ref[0], w_ref[768:1024, :]))
    hn = h_ref[0] + g1_ref[0] * mix
    hn_ref[0] = hn
    u = _rms(hn, n2_ref[...]) * (1.0 + sc_ref[0]) + sh_ref[0]
    u_hi = u.astype(BF16)
    u_ref[0] = u_hi
    tm = u.shape[0]
    u_lo = (u - u_hi.astype(F32)).astype(BF16)
    both = _dot(jnp.concatenate([u_hi, u_lo], axis=0), rw_ref[...])
    logits = both[:tm, :LANES] + both[:tm, LANES:] + both[tm:, :LANES]
    logit_ref[0] = logits


def _out_projection(ret_o, na_o, mla_o, h, w_out, g1, sh2, sc2, n2, rw, tm):
    b, l, d = h.shape
    per_batch = g1.shape[0] > 1
    mod_map = (lambda i, j: (i, 0, 0)) if per_batch else (lambda i, j: (0, 0, 0))
    const2 = lambda i, j: (0, 0)
    tok = lambda w: pl.BlockSpec((1, tm, w), lambda i, j: (i, j, 0))
    mod = pl.BlockSpec((1, 1, d), mod_map)
    return pl.pallas_call(
        _outproj_kernel,
        grid=(b, l // tm),
        in_specs=[tok(256), tok(512), tok(256), tok(d),
                  pl.BlockSpec(w_out.shape, const2), mod, mod, mod,
                  pl.BlockSpec((1, d), const2),
                  pl.BlockSpec(rw.shape, const2)],
        out_specs=[tok(d), tok(d), tok(LANES)],
        out_shape=[jax.ShapeDtypeStruct((b, l, d), F32),
                   jax.ShapeDtypeStruct((b, l, d), BF16),
                   jax.ShapeDtypeStruct((b, l, LANES), F32)],
        compiler_params=_cparams(("arbitrary", "arbitrary")),
        name="out_projection",
    )(ret_o, na_o, mla_o, h, w_out, g1, sh2, sc2, n2, rw)


def _moe_kernel(u_ref, logit_ref, logit_next_ref, rb_ref, h_ref, w13_ref, w2_ref, g2_ref, fg_ref,
                o_ref, gate_scr, *, final_norm):
    u = u_ref[...]

    @pl.when(pl.program_id(0) == 0)
    def _():
        gate_scr[...] = _gates(logit_ref[...], rb_ref[...])

    gates = gate_scr[...]
    gates_next = _gates(logit_next_ref[...], rb_ref[...])
    lane = lax.broadcasted_iota(jnp.int32, gates.shape, 1)
    y = None
    for e in range(N_EXPERTS):
        a = _dot(u, w13_ref[e])
        a1, a3 = a[:, :EXPERT_FF], a[:, EXPERT_FF:]
        gcol = jnp.sum(jnp.where(lane == e, gates, 0.0), axis=-1, keepdims=True)
        hdn = ((a1 * _sigmoid(a1)) * a3 * gcol).astype(BF16)
        ye = _dot(hdn, w2_ref[e])
        y = ye if y is None else y + ye
    hn = h_ref[...] + g2_ref[0] * y
    if final_norm:
        hn = _rms(hn, fg_ref[...])
    o_ref[...] = hn
    gate_scr[...] = gates_next


def _moe(u, logits, rb, h, w13, w2, g2, fg, tokens_per_mod, tm, final_norm):
    t, d = h.shape
    per_batch = g2.shape[0] > 1
    tiles_per_mod = tokens_per_mod // tm
    n_tiles = t // tm
    mod_map = (lambda i: (i // tiles_per_mod, 0, 0)) if per_batch else (lambda i: (0, 0, 0))
    tok = lambda w: pl.BlockSpec((tm, w), lambda i: (i, 0))
    resident = lambda a: pl.BlockSpec(a.shape, lambda i: (0, 0, 0), pipeline_mode=pl.Buffered(1))
    return pl.pallas_call(
        functools.partial(_moe_kernel, final_norm=final_norm),
        grid=(n_tiles,),
        in_specs=[tok(d), tok(LANES),
                  pl.BlockSpec((tm, LANES), lambda i: (jnp.minimum(i + 1, n_tiles - 1), 0)),
                  pl.BlockSpec(rb.shape, lambda i: (0, 0)), tok(d),
                  resident(w13), resident(w2),
                  pl.BlockSpec((1, 1, d), mod_map),
                  pl.BlockSpec((1, d), lambda i: (0, 0))],
        out_specs=tok(d),
        out_shape=jax.ShapeDtypeStruct((t, d), F32),
        scratch_shapes=[pltpu.VMEM((tm, LANES), F32)],
        compiler_params=_cparams(("arbitrary",)),
        name="mixture_of_experts",
    )(u, logits, logits, rb, h, w13, w2, g2, fg)


def _rope_tables(seq, positional):
    if not positional:
        ones = jnp.ones((seq, LANES), F32)
        zeros = jnp.zeros((seq, LANES), F32)
        return ones, zeros, ones, zeros
    t = jnp.arange(seq)
    pos_r = (t // GRID_W).astype(F32)[:, None]
    pos_c = (t % GRID_W).astype(F32)[:, None]

    def axial(width):
        h = width // 2
        half = h // 2
        inv = ROPE_BASE ** (-jnp.arange(half, dtype=F32) / half)
        inv2 = jnp.concatenate([inv, inv])[None, :]
        ang = jnp.concatenate([pos_r * inv2, pos_c * inv2], axis=1)
        sign = jnp.tile(jnp.concatenate([-jnp.ones(half), jnp.ones(half)]), 2)[None, :]
        return jnp.cos(ang), jnp.sin(ang) * sign

    rc, rs = axial(RET_DK)
    ra = jnp.concatenate([rc, rc], axis=1)
    rb = jnp.concatenate([rs, rs], axis=1)
    mc, ms = axial(MLA_ROPE)
    ma = jnp.concatenate([jnp.ones((seq, MLA_NOPE), F32), mc, jnp.zeros((seq, 32), F32)], axis=1)
    mb = jnp.concatenate([jnp.zeros((seq, MLA_NOPE), F32), ms, jnp.zeros((seq, 32), F32)], axis=1)
    return ra, rb, ma, mb


def _pad_heads(w, heads, width):
    r = w.shape[0]
    w = w.reshape(r, heads, width)
    return jnp.pad(w, ((0, 0), (0, 0), (0, LANES - width))).reshape(r, heads * LANES)


def _layer_weights(w_in, w_uq, w_ukv):
    d = w_in.shape[0]
    kpe = w_in[:, _C_KPE:_C_KPE + MLA_ROPE]
    blk = jnp.concatenate([jnp.zeros((d, MLA_NOPE), w_in.dtype), kpe,
                           jnp.zeros((d, LANES - MLA_NOPE - MLA_ROPE), w_in.dtype)], axis=1)
    w_ext = jnp.concatenate([w_in[:, :_C_KPE], blk], axis=1).astype(BF16)
    wuq = _pad_heads(w_uq, MLA_HEADS, MLA_NOPE + MLA_ROPE).astype(BF16)
    ukv = w_ukv.reshape(w_ukv.shape[0], MLA_HEADS, MLA_NOPE + MLA_VD)
    wuk = _pad_heads(ukv[:, :, :MLA_NOPE].reshape(w_ukv.shape[0], -1), MLA_HEADS, MLA_NOPE).astype(BF16)
    wuv = _pad_heads(ukv[:, :, MLA_NOPE:].reshape(w_ukv.shape[0], -1), MLA_HEADS, MLA_VD).astype(BF16)
    return w_ext, wuq, wuk, wuv


def _latent_key_tile(n_keys, n_ctx):
    for tk in range(MLA_MAX_KEY_TILE, 0, -LANES):
        if tk >= n_ctx and n_keys % tk == 0 and (n_keys // tk) % 2 == 0:
            return tk
    raise ValueError(f"no even tiling of {n_keys} keys")


def kernel(x, c, ctx, c_ctx, w_mod, b_mod, norm1_g, norm2_g, w_in, ret_decay_f, ret_decay_b, na_rpb,
           mla_q_norm, mla_kv_norm, w_uq, w_ukv, w_out, router_w, router_b, w1, w3, w2, final_norm_g):
    b, s, d = x.shape
    lc = ctx.shape[1]
    depth = w_mod.shape[0]
    rows = s // GRID_W
    tm_lat = TOKEN_TILE
    tm_ctx = min(CTX_TILE, lc)
    q_tiles = MLA_Q_TILES if s % (MLA_Q_TILE * MLA_Q_TILES) == 0 else 1

    n_mod = -(-(b + 1) // 8) * 8
    cv = jnp.concatenate([c, c_ctx[None, :], jnp.zeros((n_mod - b - 1, d), F32)], axis=0)
    mods = _modulation(cv, w_mod, b_mod).reshape(depth, n_mod, 6, d)

    tabs_lat = _rope_tables(s, True)
    tabs_ctx = _rope_tables(lc, False)
    na_plan = _na_bias_plan(rows)
    rw_f32 = jnp.pad(router_w.astype(F32), ((0, 0), (0, LANES - N_EXPERTS)))
    rw_hi = rw_f32.astype(BF16)
    rw = jnp.concatenate([rw_hi, (rw_f32 - rw_hi.astype(F32)).astype(BF16)], axis=1)
    rb = jnp.pad(router_b.astype(F32), (0, LANES - N_EXPERTS)).reshape(1, LANES)
    zero_state = (jnp.zeros((b, 2 * LANES, LANES), F32),) * 2

    h, hc = x, ctx
    for l in range(depth):
        last = l == depth - 1
        m_lat = [mods[l, :b, i][:, None, :] for i in range(6)]
        m_ctx = [mods[l, b:b + 1, i][:, None, :] for i in range(6)]
        w_ext, wuq, wuk, wuv = _layer_weights(w_in[l], w_uq[l], w_ukv[l])
        n1 = norm1_g[l].reshape(1, d)
        n2 = norm2_g[l].reshape(1, d)
        qn = mla_q_norm[l].reshape(1, -1)
        kvn = mla_kv_norm[l].reshape(1, -1)
        wo = w_out[l].astype(BF16)
        w13 = jnp.concatenate([w1[l], w3[l]], axis=-1).astype(BF16)
        w2b = w2[l].astype(BF16)
        fg = final_norm_g.reshape(1, d)

        ret_c, na_c, mq_c, mk_c, mv_c = _in_projection(
            hc, m_ctx[0], m_ctx[1], n1, w_ext, qn, kvn, wuq, wuk, wuv, tabs_ctx, tm_ctx)
        ret_l, na_l, mq_l, mk_l, mv_l = _in_projection(
            h, m_lat[0], m_lat[1], n1, w_ext, qn, kvn, wuq, wuk, wuv, tabs_lat, tm_lat)

        rtabs = _retention_tables(ret_decay_f[l], ret_decay_b[l])
        ro_c, st = _retention(ret_c, rtabs, zero_state)
        ro_l, _ = _retention(ret_l, rtabs, st)

        na_o = _na_attention(na_l, na_c, _na_bias(na_rpb[l], na_plan))
        mla_o = _latent_attention(mq_l, mk_l, mv_l, (mk_c, mv_c), tq=MLA_Q_TILE, q_tiles=q_tiles,
                                  tk=_latent_key_tile(s, lc), name="latent_attention")

        hn, u2, logits = _out_projection(ro_l, na_o, mla_o, h, wo, m_lat[2], m_lat[3], m_lat[4],
                                         n2, rw, tm_lat)
        h = _moe(u2.reshape(b * s, d), logits.reshape(b * s, LANES), rb, hn.reshape(b * s, d),
                 w13, w2b, m_lat[5], fg, s, TOKEN_TILE, last).reshape(b, s, d)

        if not last:
            nac_o = _pair_attention(na_c[:, :, 0:512], na_c[:, :, 512:1024], na_c[:, :, 1024:1536],
                                    tq=tm_ctx, name="context_na_attention")
            mlac_o = _latent_attention(mq_c, mk_c, mv_c, tq=tm_ctx, q_tiles=1, tk=lc,
                                       name="context_latent_attention")
            hcn, uc2, logits_c = _out_projection(ro_c, nac_o, mlac_o, hc, wo, m_ctx[2], m_ctx[3],
                                                 m_ctx[4], n2, rw, tm_ctx)
            hc = _moe(uc2.reshape(b * lc, d), logits_c.reshape(b * lc, LANES), rb,
                      hcn.reshape(b * lc, d), w13, w2b, m_ctx[5], fg, lc, min(TOKEN_TILE, lc),
                      False).reshape(b, lc, d)
    return h
```

```python
import functools

import numpy as np
import jax
import jax.numpy as jnp
from jax import lax
from jax.experimental import pallas as pl
from jax.experimental.pallas import tpu as pltpu

F32 = jnp.float32
BF16 = jnp.bfloat16

GRID_W = 64
EPS = 1e-6
ROPE_BASE = 10000.0
RET_HEADS, RET_DK, RET_DV = 4, 64, 64
RET_CHUNK = 256
NA_HEADS, NA_HD, NA_KH, NA_KW, NA_QBLOCK = 8, 64, 8, 16, 128
MLA_HEADS, MLA_NOPE, MLA_ROPE, MLA_VD = 4, 64, 32, 64
MLA_Q_RANK, MLA_KV_RANK = 384, 256
N_EXPERTS, N_GROUPS, TOP_K, EXPERT_FF = 16, 4, 2, 256
EXPERTS_PER_GROUP = N_EXPERTS // N_GROUPS

LANES = 128
HEAD = 64
TOKEN_TILE = 512
CTX_TILE = 256
NA_WIN_BLOCKS = 5
MLA_Q_TILE = 256
MLA_Q_TILES = 4
MLA_MAX_KEY_TILE = 1024
NA_SUB = 2
NA_STEP_BLOCKS = NA_WIN_BLOCKS + NA_SUB - 1
RET_GROUP = 8
NEG_BIG = -1e30
VMEM_LIMIT = 48 * 1024 * 1024

_C_RET, _C_NA, _C_CQ, _C_CKV, _C_KPE = 0, 1024, 2560, 2944, 3200
_W_EXT = _C_KPE + LANES
MLA_Q_SCALE = (MLA_NOPE + MLA_ROPE) ** -0.5 * float(np.log2(np.e))


def _cparams(sem):
    return pltpu.CompilerParams(dimension_semantics=sem, vmem_limit_bytes=VMEM_LIMIT)


def _dot(a, b):
    return jnp.dot(a, b, preferred_element_type=F32)


def _dot_nt(a, b):
    return lax.dot_general(a, b, (((1,), (1,)), ((), ())), preferred_element_type=F32)


def _sigmoid(x):
    return 1.0 / (1.0 + jnp.exp(-x))


def _win_roll(x, s, w):
    n = x.shape[-1]
    ax = x.ndim - 1
    lane = lax.broadcasted_iota(jnp.int32, x.shape, ax)
    fwd = pltpu.roll(x, n - s, ax)
    bwd = pltpu.roll(x, w - s, ax)
    return jnp.where((lane & (w - 1)) + s < w, fwd, bwd)


def _low_half(shape):
    lane = lax.broadcasted_iota(jnp.int32, shape, len(shape) - 1)
    return (lane & (LANES - 1)) < HEAD


def _mod_kernel(cv_ref, w_ref, b_ref, o_ref):
    cv = cv_ref[...]
    a = cv * _sigmoid(cv)
    o_ref[0] = jnp.dot(a, w_ref[0], preferred_element_type=F32,
                       precision=lax.Precision.HIGHEST) + b_ref[0]


def _modulation(cv, w_mod, b_mod):
    depth, d, n6 = w_mod.shape
    rows = cv.shape[0]
    tn = 1536
    return pl.pallas_call(
        _mod_kernel,
        grid=(depth, n6 // tn),
        in_specs=[pl.BlockSpec((rows, d), lambda l, j: (0, 0)),
                  pl.BlockSpec((1, d, tn), lambda l, j: (l, 0, j)),
                  pl.BlockSpec((1, 1, tn), lambda l, j: (l, 0, j))],
        out_specs=pl.BlockSpec((1, rows, tn), lambda l, j: (l, 0, j)),
        out_shape=jax.ShapeDtypeStruct((depth, rows, n6), F32),
        compiler_params=_cparams(("arbitrary", "arbitrary")),
        name="modulation",
    )(cv, w_mod, b_mod.reshape(depth, 1, n6))


def _rms(x, g):
    return x * lax.rsqrt(jnp.mean(x * x, axis=-1, keepdims=True) + EPS) * g


def _inproj_kernel(h_ref, sh_ref, sc_ref, g_ref, w_ref, qn_ref, kvn_ref, wuq_ref, wuk_ref, wuv_ref,
                   ra_ref, rb_ref, ma_ref, mb_ref,
                   ret_ref, na_ref, q_ref, k_ref, v_ref):
    x = h_ref[0]
    u = _rms(x, g_ref[...]) * (1.0 + sc_ref[0]) + sh_ref[0]
    ub = u.astype(BF16)

    r = _dot(ub, w_ref[:, _C_RET:_C_NA])
    qk = r[:, :512]
    ra = jnp.concatenate([ra_ref[...]] * 4, axis=1)
    rb = jnp.concatenate([rb_ref[...]] * 4, axis=1)
    qk = qk * ra + _win_roll(qk, 16, 32) * rb
    ret_ref[0, :, 0:256] = qk[:, :256] * (RET_DK ** -0.5)
    ret_ref[0, :, 256:512] = qk[:, 256:]
    ret_ref[0, :, 512:1024] = r[:, 512:]

    n = _dot(ub, w_ref[:, _C_NA:_C_CQ])
    na_ref[0, :, 0:512] = (n[:, :512] * (NA_HD ** -0.5)).astype(BF16)
    na_ref[0, :, 512:1536] = n[:, 512:].astype(BF16)

    ma = jnp.concatenate([ma_ref[...]] * 4, axis=1)
    mb = jnp.concatenate([mb_ref[...]] * 4, axis=1)
    cq = _rms(_dot(ub, w_ref[:, _C_CQ:_C_CKV]), qn_ref[...]).astype(BF16)
    q = _dot(cq, wuq_ref[...])
    q = q * ma + _win_roll(q, 8, 16) * mb
    q_ref[0] = (q * MLA_Q_SCALE).astype(BF16)
    ckv = _rms(_dot(ub, w_ref[:, _C_CKV:_C_KPE]), kvn_ref[...]).astype(BF16)
    kp = _dot(ub, w_ref[:, _C_KPE:_W_EXT])
    kp = kp * ma_ref[...] + _win_roll(kp, 8, 16) * mb_ref[...]
    kp = jnp.concatenate([kp] * MLA_HEADS, axis=1)
    k_ref[0] = (_dot(ckv, wuk_ref[...]) + kp).astype(BF16)
    v = _dot(ckv, wuv_ref[...])
    lane = lax.broadcasted_iota(jnp.int32, v.shape, 1)
    v_ref[0] = jnp.where((lane & (LANES - 1)) == HEAD, 1.0, v).astype(BF16)


def _in_projection(h, sh, sc, g, w_ext, qn, kvn, wuq, wuk, wuv, tabs, tm):
    b, l, d = h.shape
    per_batch = sh.shape[0] > 1
    mod_map = (lambda i, j: (i, 0, 0)) if per_batch else (lambda i, j: (0, 0, 0))
    const2 = lambda i, j: (0, 0)
    tab_spec = pl.BlockSpec((tm, LANES), lambda i, j: (j, 0))
    tok = lambda w: pl.BlockSpec((1, tm, w), lambda i, j: (i, j, 0))
    return pl.pallas_call(
        _inproj_kernel,
        grid=(b, l // tm),
        in_specs=[tok(d),
                  pl.BlockSpec((1, 1, d), mod_map), pl.BlockSpec((1, 1, d), mod_map),
                  pl.BlockSpec((1, d), const2),
                  pl.BlockSpec(w_ext.shape, const2),
                  pl.BlockSpec(qn.shape, const2), pl.BlockSpec(kvn.shape, const2),
                  pl.BlockSpec(wuq.shape, const2), pl.BlockSpec(wuk.shape, const2),
                  pl.BlockSpec(wuv.shape, const2),
                  tab_spec, tab_spec, tab_spec, tab_spec],
        out_specs=[tok(1024), tok(1536), tok(512), tok(512), tok(512)],
        out_shape=[jax.ShapeDtypeStruct((b, l, 1024), F32),
                   jax.ShapeDtypeStruct((b, l, 1536), BF16),
                   jax.ShapeDtypeStruct((b, l, 512), BF16),
                   jax.ShapeDtypeStruct((b, l, 512), BF16),
                   jax.ShapeDtypeStruct((b, l, 512), BF16)],
        compiler_params=_cparams(("arbitrary", "arbitrary")),
        name="in_projection",
    )(h, sh, sc, g, w_ext, qn, kvn, wuq, wuk, wuv, *tabs)


def _retention_kernel(*refs, backward, group):
    if backward:
        (q_ref, k_ref, v_ref, g_ref, of_ref, dm_ref, qw_ref, kw_ref, gc_ref, s0_ref,
         o_ref, sfin_ref, state) = refs
    else:
        q_ref, k_ref, v_ref, dm_ref, qw_ref, kw_ref, gc_ref, s0_ref, o_ref, sfin_ref, state = refs
    step = pl.program_id(1)
    c = RET_CHUNK
    n_pairs = RET_HEADS // 2

    @pl.when(step == 0)
    def _():
        state[...] = s0_ref[0]

    lo = _low_half((c, LANES))
    row_lo = lax.broadcasted_iota(jnp.int32, (LANES, LANES), 0) < HEAD
    same_head = row_lo == _low_half((LANES, LANES))
    qw, kw = qw_ref[...], kw_ref[...]

    intra, kv, qd = {}, {}, {}
    for ci in range(group):
        rows = slice(c * ci, c * (ci + 1))
        for p in range(n_pairs):
            sl = slice(LANES * p, LANES * (p + 1))
            qp, kp = q_ref[0, rows, sl], k_ref[0, rows, sl]
            kb, vb = kp.astype(BF16), v_ref[0, rows, sl].astype(BF16)
            qab = jnp.concatenate([jnp.where(lo, qp, 0.0), jnp.where(lo, 0.0, qp)], axis=0).astype(BF16)
            sab = (_dot_nt(qab, kb) * dm_ref[p]).astype(BF16)
            oab = _dot(sab, vb)
            intra[ci, p] = jnp.where(lo, oab[:c], oab[c:])
            kv[ci, p] = jnp.where(same_head, _dot((kp * kw[:, sl]).T.astype(BF16), vb), 0.0)
            qd[ci, p] = (qp * qw[:, sl]).astype(BF16)

    order = range(group - 1, -1, -1) if backward else range(group)
    out = {}
    for p in range(n_pairs):
        sl = slice(LANES * p, LANES * (p + 1))
        sp = state[sl, :]
        gc = gc_ref[sl, :]
        for ci in order:
            out[ci, p] = intra[ci, p] + _dot(qd[ci, p], sp.astype(BF16))
            sp = gc * sp + kv[ci, p]
        state[sl, :] = sp

    for ci in range(group):
        rows = slice(c * ci, c * (ci + 1))
        o = jnp.concatenate([out[ci, p] for p in range(n_pairs)], axis=1)
        if not backward:
            o_ref[0, rows, :] = o
            continue
        tot = of_ref[0, rows, :] + o
        g = g_ref[0, rows, :]
        normed = []
        for p in range(n_pairs):
            t = tot[:, LANES * p:LANES * (p + 1)]

            def head_mean(a):
                s_lo = jnp.sum(jnp.where(lo, a, 0.0), axis=-1, keepdims=True)
                s_hi = jnp.sum(jnp.where(lo, 0.0, a), axis=-1, keepdims=True)
                return jnp.where(lo, s_lo, s_hi) * (1.0 / HEAD)

            dlt = t - head_mean(t)
            normed.append(dlt * lax.rsqrt(head_mean(dlt * dlt) + EPS))
        on = jnp.concatenate(normed, axis=1)
        o_ref[0, rows, :] = (on * (g * _sigmoid(g))).astype(BF16)

    @pl.when(step == pl.num_programs(1) - 1)
    def _():
        sfin_ref[0] = state[...]


def _retention_sweep(ret, o_fwd, tabs, s0, *, backward):
    b, l, _ = ret.shape
    c = RET_CHUNK
    group = min(RET_GROUP, l // c)
    n = l // (c * group)
    gidx = (lambda s: n - 1 - s) if backward else (lambda s: s)
    tok = lambda col: pl.BlockSpec((1, c * group, 256), lambda i, s: (i, gidx(s), col))
    const = lambda a: pl.BlockSpec(a.shape, lambda i, s: (0,) * a.ndim)
    state_spec = pl.BlockSpec((1, 2 * LANES, LANES), lambda i, s: (i, 0, 0))
    dm, qw, kw, gc = tabs
    if backward:
        in_specs = [tok(0), tok(1), tok(2), tok(3), tok(0)]
        args = [ret, ret, ret, ret, o_fwd]
    else:
        in_specs = [tok(0), tok(1), tok(2)]
        args = [ret, ret, ret]
    return pl.pallas_call(
        functools.partial(_retention_kernel, backward=backward, group=group),
        grid=(b, n),
        in_specs=in_specs + [const(dm), const(qw), const(kw), const(gc), state_spec],
        out_specs=[tok(0), state_spec],
        out_shape=[jax.ShapeDtypeStruct((b, l, 256), BF16 if backward else F32),
                   jax.ShapeDtypeStruct((b, 2 * LANES, LANES), F32)],
        scratch_shapes=[pltpu.VMEM((2 * LANES, LANES), F32)],
        compiler_params=_cparams(("arbitrary", "arbitrary")),
        name="retention_backward" if backward else "retention_forward",
    )(*args, dm, qw, kw, gc, s0)


def _retention(ret, tabs, s0):
    fwd_tabs = tuple(t[0] for t in tabs)
    bwd_tabs = tuple(t[1] for t in tabs)
    o_f, s_f = _retention_sweep(ret, None, fwd_tabs, s0[0], backward=False)
    o, s_b = _retention_sweep(ret, o_f, bwd_tabs, s0[1], backward=True)
    return o, (s_f, s_b)


def _retention_tables(decay_f, decay_b):
    c = RET_CHUNK
    pos = jnp.arange(c, dtype=F32)
    diff = pos[:, None] - pos[None, :]
    lg = jnp.stack([jax.nn.log_sigmoid(decay_f.astype(F32)), jax.nn.log_sigmoid(decay_b.astype(F32))])
    lgh = lg[:, :, None, None]
    dm_f = jnp.where(diff >= 0, jnp.exp(lgh[0] * jnp.maximum(diff, 0.0)), 0.0)
    dm_b = jnp.where(diff < 0, jnp.exp(lgh[1] * jnp.maximum(-diff, 0.0)), 0.0)
    dm = jnp.stack([dm_f, dm_b]).reshape(2, RET_HEADS // 2, 2 * c, c)
    lanes = lambda t: jnp.repeat(t, RET_DK, axis=-1)
    lgc = lg[:, None, :]
    qw = lanes(jnp.exp(lgc * jnp.stack([pos + 1.0, c - pos])[:, :, None]))
    kw = lanes(jnp.exp(lgc * jnp.stack([c - 1.0 - pos, pos])[:, :, None]))
    gch = jnp.exp(lg * c)
    gc = jnp.broadcast_to(jnp.repeat(gch, RET_DK, axis=-1)[:, :, None], (2, RET_HEADS * RET_DK, LANES))
    return dm, qw, kw, gc


def _na_kernel(*refs, n_ctx):
    w = NA_STEP_BLOCKS
    q_ref = refs[0]
    k_refs, v_refs = refs[1:1 + w], refs[1 + w:1 + 2 * w]
    kc_ref, vc_ref, bias_ref, o_ref, s_scr, p_scr, den_scr = refs[1 + 2 * w:]
    tq = q_ref.shape[1]
    n_loc = w * NA_QBLOCK
    n_lane_blocks = (n_loc + n_ctx) // LANES
    lo = _low_half((tq, LANES))
    for h in range(NA_HEADS):
        sl = slice(LANES * (h // 2), LANES * (h // 2 + 1))
        qp = q_ref[0, :, sl]
        qm = jnp.where(lo, qp, 0) if h % 2 == 0 else jnp.where(lo, 0, qp)
        kwin = jnp.concatenate([r[0, :, sl] for r in k_refs], axis=0)
        s_scr[h, :, 0:n_loc] = _dot_nt(qm, kwin) + bias_ref[0, h]
        s_scr[h, :, n_loc:] = _dot_nt(qm, kc_ref[0, :, sl])
    for h in range(NA_HEADS):
        part = s_scr[h, :, 0:LANES]
        for c in range(1, n_lane_blocks):
            part = jnp.maximum(part, s_scr[h, :, LANES * c:LANES * (c + 1)])
        m = jnp.max(part, axis=-1, keepdims=True)
        tot = None
        for c in range(n_lane_blocks):
            cs = slice(LANES * c, LANES * (c + 1))
            e = jnp.exp(s_scr[h, :, cs] - m)
            tot = e if tot is None else tot + e
            p_scr[h, :, cs] = e.astype(BF16)
        den_scr[h] = jnp.broadcast_to(jnp.sum(tot, axis=-1, keepdims=True), (tq, LANES))
    for p in range(NA_HEADS // 2):
        sl = slice(LANES * p, LANES * (p + 1))
        vwin = jnp.concatenate([r[0, :, sl] for r in v_refs], axis=0)
        halves = []
        for h in (2 * p, 2 * p + 1):
            o = _dot(p_scr[h, :, 0:n_loc], vwin) + _dot(p_scr[h, :, n_loc:], vc_ref[0, :, sl])
            halves.append(o / den_scr[h])
        o_ref[0, :, sl] = jnp.where(lo, halves[0], halves[1]).astype(BF16)


def _na_step_variant(i, n_steps):
    return jnp.where(i == 0, 0, jnp.where(i == n_steps - 1, 2, 1))


def _na_attention(na, na_ctx, bias):
    b, l, _ = na.shape
    n_ctx = na_ctx.shape[1]
    nblk = l // NA_QBLOCK
    n_steps = nblk // NA_SUB
    tq = NA_SUB * NA_QBLOCK
    n_loc = NA_STEP_BLOCKS * NA_QBLOCK
    n_keys = n_loc + n_ctx

    def win(col, j):
        return pl.BlockSpec((1, NA_QBLOCK, 512), lambda bi, i: (
            bi, jnp.clip(NA_SUB * i - 2, 0, nblk - NA_STEP_BLOCKS) + j, col))

    in_specs = ([pl.BlockSpec((1, tq, 512), lambda bi, i: (bi, i, 0))]
                + [win(1, j) for j in range(NA_STEP_BLOCKS)]
                + [win(2, j) for j in range(NA_STEP_BLOCKS)]
                + [pl.BlockSpec((1, n_ctx, 512), lambda bi, i: (bi, 0, 1)),
                   pl.BlockSpec((1, n_ctx, 512), lambda bi, i: (bi, 0, 2)),
                   pl.BlockSpec((1, NA_HEADS, tq, n_loc),
                                lambda bi, i: (_na_step_variant(i, n_steps), 0, 0, 0))])
    args = [na] * (1 + 2 * NA_STEP_BLOCKS) + [na_ctx, na_ctx, bias]
    return pl.pallas_call(
        functools.partial(_na_kernel, n_ctx=n_ctx),
        grid=(b, n_steps),
        in_specs=in_specs,
        out_specs=pl.BlockSpec((1, tq, 512), lambda bi, i: (bi, i, 0)),
        out_shape=jax.ShapeDtypeStruct((b, l, 512), BF16),
        scratch_shapes=[pltpu.VMEM((NA_HEADS, tq, n_keys), F32),
                        pltpu.VMEM((NA_HEADS, tq, n_keys), BF16),
                        pltpu.VMEM((NA_HEADS, tq, LANES), F32)],
        compiler_params=_cparams(("arbitrary", "arbitrary")),
        name="neighbourhood_attention",
    )(*args)


def _na_bias_plan(rows):
    nblk = rows * GRID_W // NA_QBLOCK
    kh = min(NA_KH, rows)
    q_rows = NA_QBLOCK // GRID_W
    k_rows = NA_WIN_BLOCKS * q_rows
    plan = []
    for i in (0, 1, 2, nblk - 2, nblk - 1):
        kb = min(max(i - 2, 0), nblk - NA_WIN_BLOCKS)
        per_q = []
        for rr in range(q_rows):
            r = q_rows * i + rr
            r0 = min(max(r - kh // 2, 0), rows - kh)
            per_q.append([(q_rows * kb + kj - r + NA_KH - 1) if r0 <= q_rows * kb + kj < r0 + kh else None
                          for kj in range(k_rows)])
        plan.append(per_q)
    return plan


def _na_bias(rpb, plan):
    c = np.arange(GRID_W)
    c0 = np.clip(c - NA_KW // 2, 0, GRID_W - NA_KW)
    kc = np.arange(GRID_W)
    ok = (kc[None, :] >= c0[:, None]) & (kc[None, :] < c0[:, None] + NA_KW)
    reach = GRID_W - NA_KW
    padded = jnp.pad(rpb.astype(F32), ((0, 0), (0, 0), (reach, reach)))
    cols = jnp.stack([padded[:, :, GRID_W - 1 - ci:2 * GRID_W - 1 - ci] for ci in range(GRID_W)], axis=2)
    cols = jnp.where(jnp.asarray(ok)[None, None], cols, NEG_BIG)
    outside = jnp.full((NA_HEADS, GRID_W, GRID_W), NEG_BIG, F32)
    variants = []
    for per_q in plan:
        q_parts = [jnp.concatenate([outside if dr is None else cols[:, dr] for dr in per_k], axis=-1)
                   for per_k in per_q]
        variants.append(jnp.concatenate(q_parts, axis=1))

    hidden = jnp.full((NA_HEADS, NA_QBLOCK, NA_QBLOCK), NEG_BIG, F32)

    def place(blk, off):
        return jnp.concatenate([hidden] * off + [blk] + [hidden] * (1 - off), axis=-1)

    steps = [(0, 0, 1, 0), (2, 0, 2, 1), (3, 1, 4, 1)]
    return jnp.stack([jnp.concatenate([place(variants[va], oa), place(variants[vb], ob)], axis=1)
                      for va, oa, vb, ob in steps])


def _latent_attn_kernel(*refs, tk, q_tiles, n_ctx):
    if n_ctx:
        q_ref, k_ref, v_ref, kc_ref, vc_ref, o_ref, s_scr, p_scr, al_scr, m_scr, acc_scr = refs
    else:
        q_ref, k_ref, v_ref, o_ref, s_scr, p_scr, al_scr, m_scr, acc_scr = refs
    n_k = k_ref.shape[1] // tk + (1 if n_ctx else 0)
    n_heads = v_ref.shape[2] // LANES
    tq = q_ref.shape[1] // q_tiles
    assert n_k % 2 == 1, "static slot parities below assume an odd number of key tiles"

    def is_ctx(j):
        return bool(n_ctx) and isinstance(j, int) and j == n_k - 1

    def key_rows(j):
        return pl.ds(pl.multiple_of(j * tk, tk), tk)

    def slot_of(a, j_parity):
        return (a * n_k + j_parity) % 2

    def scores(a, j, slot):
        for h in range(n_heads):
            sl = slice(LANES * h, LANES * (h + 1))
            qh = q_ref[0, tq * a:tq * (a + 1), sl]
            if is_ctx(j):
                s_scr[slot, h, :, 0:n_ctx] = _dot_nt(qh, kc_ref[0, :, sl])
            else:
                s_scr[slot, h] = _dot_nt(qh, k_ref[0, key_rows(j), sl])

    def softmax(a, slot, ctx_tile=False):
        n_lane_blocks = (n_ctx if ctx_tile else tk) // LANES
        for h in range(n_heads):
            part = s_scr[slot, h, :, 0:LANES]
            for c in range(1, n_lane_blocks):
                part = jnp.maximum(part, s_scr[slot, h, :, LANES * c:LANES * (c + 1)])
            m_old = m_scr[a, h]
            m_new = jnp.maximum(m_old, jnp.max(part, axis=-1, keepdims=True))
            al_scr[slot, h] = jnp.exp2(m_old - m_new)
            m_scr[a, h] = m_new
            for c in range(n_lane_blocks):
                cs = slice(LANES * c, LANES * (c + 1))
                p_scr[slot, h, :, cs] = jnp.exp2(s_scr[slot, h, :, cs] - m_new).astype(BF16)

    def values(a, j, slot):
        for h in range(n_heads):
            sl = slice(LANES * h, LANES * (h + 1))
            if is_ctx(j):
                pv = _dot(p_scr[slot, h, :, 0:n_ctx], vc_ref[0, :, sl])
            else:
                pv = _dot(p_scr[slot, h], v_ref[0, key_rows(j), sl])
            acc_scr[a, h] = al_scr[slot, h] * acc_scr[a, h] + pv

    def time_step(nxt, cur, prv):
        if nxt is not None:
            scores(*nxt)
        softmax(*cur)
        if prv is not None:
            values(*prv)

    m_scr[...] = jnp.full(m_scr.shape, -jnp.inf, F32)
    acc_scr[...] = jnp.zeros(acc_scr.shape, F32)
    scores(0, 0, 0)
    for a in range(q_tiles):
        s_even, s_odd = slot_of(a, 0), slot_of(a, 1)
        first_of_next = (a + 1, 0, slot_of(a + 1, 0)) if a + 1 < q_tiles else None
        last_of_prev = (a - 1, n_k - 1, slot_of(a - 1, 0)) if a > 0 else None
        if n_k == 1:
            time_step(first_of_next, (a, s_even), last_of_prev)
            continue
        time_step((a, 1, s_odd), (a, s_even), last_of_prev)

        def body(i, carry, a=a, s_even=s_even, s_odd=s_odd):
            j = 2 * i + 1
            time_step((a, j + 1, s_even), (a, s_odd), (a, j - 1, s_even))
            time_step((a, j + 2, s_odd), (a, s_even), (a, j, s_odd))
            return carry

        lax.fori_loop(0, (n_k - 3) // 2, body, 0)
        time_step((a, n_k - 1, s_even), (a, s_odd), (a, n_k - 3, s_even))
        time_step(first_of_next, (a, s_even, bool(n_ctx)), (a, n_k - 2, s_odd))
    values(q_tiles - 1, n_k - 1, slot_of(q_tiles - 1, 0))
    for a in range(q_tiles):
        outs = []
        for h in range(n_heads):
            acc = acc_scr[a, h]
            outs.append(acc[:, 0:HEAD] / acc[:, HEAD:HEAD + 1])
        o_ref[0, tq * a:tq * (a + 1), :] = jnp.concatenate(outs, axis=1).astype(BF16)


def _latent_attention(q, k, v, ctx_kv=None, *, tq, q_tiles, tk, name):
    b, lq, w = q.shape
    lk = k.shape[1]
    n_heads = w // LANES
    rows = tq * q_tiles
    n_ctx = 0 if ctx_kv is None else ctx_kv[0].shape[1]
    resident = lambda n: pl.BlockSpec((1, n, w), lambda bi, i: (bi, 0, 0), pipeline_mode=pl.Buffered(1))
    stat = pltpu.VMEM((q_tiles, n_heads, tq, LANES), F32)
    in_specs = [pl.BlockSpec((1, rows, w), lambda bi, i: (bi, i, 0)), resident(lk), resident(lk)]
    args = [q, k, v]
    if n_ctx:
        assert n_ctx <= tk and n_ctx % LANES == 0
        in_specs += [resident(n_ctx), resident(n_ctx)]
        args += list(ctx_kv)
    return pl.pallas_call(
        functools.partial(_latent_attn_kernel, tk=tk, q_tiles=q_tiles, n_ctx=n_ctx),
        grid=(b, lq // rows),
        in_specs=in_specs,
        out_specs=pl.BlockSpec((1, rows, n_heads * HEAD), lambda bi, i: (bi, i, 0)),
        out_shape=jax.ShapeDtypeStruct((b, lq, n_heads * HEAD), BF16),
        scratch_shapes=[pltpu.VMEM((2, n_heads, tq, tk), F32), pltpu.VMEM((2, n_heads, tq, tk), BF16),
                        pltpu.VMEM((2, n_heads, tq, LANES), F32), stat, stat],
        compiler_params=_cparams(("arbitrary", "arbitrary")),
        name=name,
    )(*args)


def _pair_attn_kernel(q_ref, k_ref, v_ref, o_ref):
    tq = q_ref.shape[1]
    lo = _low_half((tq, LANES))
    for p in range(v_ref.shape[2] // LANES):
        sl = slice(LANES * p, LANES * (p + 1))
        qp, kp, vp = q_ref[0, :, sl], k_ref[0, :, sl], v_ref[0, :, sl]
        halves = []
        for half in range(2):
            qm = jnp.where(lo, qp, 0) if half == 0 else jnp.where(lo, 0, qp)
            s = _dot_nt(qm, kp)
            e = jnp.exp(s - jnp.max(s, axis=-1, keepdims=True))
            halves.append(_dot(e.astype(BF16), vp) / jnp.sum(e, axis=-1, keepdims=True))
        o_ref[0, :, sl] = jnp.where(lo, halves[0], halves[1]).astype(BF16)


def _pair_attention(q, k, v, *, tq, name):
    b, lq, w = q.shape
    lk = k.shape[1]
    return pl.pallas_call(
        _pair_attn_kernel,
        grid=(b, lq // tq),
        in_specs=[pl.BlockSpec((1, tq, w), lambda bi, i: (bi, i, 0)),
                  pl.BlockSpec((1, lk, w), lambda bi, i: (bi, 0, 0)),
                  pl.BlockSpec((1, lk, w), lambda bi, i: (bi, 0, 0))],
        out_specs=pl.BlockSpec((1, tq, w), lambda bi, i: (bi, i, 0)),
        out_shape=jax.ShapeDtypeStruct((b, lq, w), BF16),
        compiler_params=_cparams(("arbitrary", "arbitrary")),
        name=name,
    )(q, k, v)


def _gates(logits, bias):
    scores = _sigmoid(logits)
    sel = scores + bias
    lane = lax.broadcasted_iota(jnp.int32, logits.shape, 1)
    e4 = lane & (EXPERTS_PER_GROUP - 1)
    grp = (lane >> 2) & (N_GROUPS - 1)
    one, zero = jnp.float32(1.0), jnp.float32(0.0)

    rank = jnp.zeros(logits.shape, F32)
    for d in range(1, EXPERTS_PER_GROUP):
        other = _win_roll(sel, d, EXPERTS_PER_GROUP)
        other_first = ((e4 + d) & (EXPERTS_PER_GROUP - 1)) < e4
        ahead = jnp.logical_or(other > sel, jnp.logical_and(other == sel, other_first))
        rank = rank + jnp.where(ahead, one, zero)
    top2 = rank < float(TOP_K)

    def group_sum(a):
        tot = a
        for d in range(1, EXPERTS_PER_GROUP):
            tot = tot + _win_roll(a, d, EXPERTS_PER_GROUP)
        return tot

    gscore = group_sum(jnp.where(top2, sel, zero))
    beaten = jnp.zeros(logits.shape, F32)
    for d in range(1, N_GROUPS):
        other = _win_roll(gscore, d * EXPERTS_PER_GROUP, N_EXPERTS)
        other_first = ((grp + d) & (N_GROUPS - 1)) < grp
        ahead = jnp.logical_or(other > gscore, jnp.logical_and(other == gscore, other_first))
        beaten = beaten + jnp.where(ahead, one, zero)
    chosen = jnp.logical_and(jnp.logical_and(beaten < 0.5, top2), lane < N_EXPERTS)
    w = jnp.where(chosen, scores, zero)
    den = group_sum(w)
    return jnp.where(chosen, w / den, zero)


def _outproj_kernel(ret_ref, na_ref, mla_ref, h_ref, w_ref, g1_ref, sh_ref, sc_ref, n2_ref,
                    rw_ref, hn_ref, u_ref, logit_ref):
    mix = (_dot(ret_ref[0], w_ref[0:256, :]) + _dot(na_ref[0], w_ref[256:768, :])
           + _dot(mla_ref[0], w_ref[768:1024, :]))
    hn = h_ref[0] + g1_ref[0] * mix
    hn_ref[0] = hn
    u = _rms(hn, n2_ref[...]) * (1.0 + sc_ref[0]) + sh_ref[0]
    u_hi = u.astype(BF16)
    u_ref[0] = u_hi
    tm = u.shape[0]
    u_lo = (u - u_hi.astype(F32)).astype(BF16)
    both = _dot(jnp.concatenate([u_hi, u_lo], axis=0), rw_ref[...])
    logits = both[:tm, :LANES] + both[:tm, LANES:] + both[tm:, :LANES]
    logit_ref[0] = logits


def _out_projection(ret_o, na_o, mla_o, h, w_out, g1, sh2, sc2, n2, rw, tm):
    b, l, d = h.shape
    per_batch = g1.shape[0] > 1
    mod_map = (lambda i, j: (i, 0, 0)) if per_batch else (lambda i, j: (0, 0, 0))
    const2 = lambda i, j: (0, 0)
    tok = lambda w: pl.BlockSpec((1, tm, w), lambda i, j: (i, j, 0))
    mod = pl.BlockSpec((1, 1, d), mod_map)
    return pl.pallas_call(
        _outproj_kernel,
        grid=(b, l // tm),
        in_specs=[tok(256), tok(512), tok(256), tok(d),
                  pl.BlockSpec(w_out.shape, const2), mod, mod, mod,
                  pl.BlockSpec((1, d), const2),
                  pl.BlockSpec(rw.shape, const2)],
        out_specs=[tok(d), tok(d), tok(LANES)],
        out_shape=[jax.ShapeDtypeStruct((b, l, d), F32),
                   jax.ShapeDtypeStruct((b, l, d), BF16),
                   jax.ShapeDtypeStruct((b, l, LANES), F32)],
        compiler_params=_cparams(("arbitrary", "arbitrary")),
        name="out_projection",
    )(ret_o, na_o, mla_o, h, w_out, g1, sh2, sc2, n2, rw)


def _moe_kernel(u_ref, logit_ref, logit_next_ref, rb_ref, h_ref, w13_ref, w2_ref, g2_ref, fg_ref,
                o_ref, gate_scr, *, final_norm):
    u = u_ref[...]

    @pl.when(pl.program_id(0) == 0)
    def _():
        gate_scr[...] = _gates(logit_ref[...], rb_ref[...])

    gates = gate_scr[...]
    gates_next = _gates(logit_next_ref[...], rb_ref[...])
    lane = lax.broadcasted_iota(jnp.int32, gates.shape, 1)
    y = None
    for e in range(N_EXPERTS):
        a = _dot(u, w13_ref[e])
        a1, a3 = a[:, :EXPERT_FF], a[:, EXPERT_FF:]
        gcol = jnp.sum(jnp.where(lane == e, gates, 0.0), axis=-1, keepdims=True)
        hdn = ((a1 * _sigmoid(a1)) * a3 * gcol).astype(BF16)
        ye = _dot(hdn, w2_ref[e])
        y = ye if y is None else y + ye
    hn = h_ref[...] + g2_ref[0] * y
    if final_norm:
        hn = _rms(hn, fg_ref[...])
    o_ref[...] = hn
    gate_scr[...] = gates_next


def _moe(u, logits, rb, h, w13, w2, g2, fg, tokens_per_mod, tm, final_norm):
    t, d = h.shape
    per_batch = g2.shape[0] > 1
    tiles_per_mod = tokens_per_mod // tm
    n_tiles = t // tm
    mod_map = (lambda i: (i // tiles_per_mod, 0, 0)) if per_batch else (lambda i: (0, 0, 0))
    tok = lambda w: pl.BlockSpec((tm, w), lambda i: (i, 0))
    resident = lambda a: pl.BlockSpec(a.shape, lambda i: (0, 0, 0), pipeline_mode=pl.Buffered(1))
    return pl.pallas_call(
        functools.partial(_moe_kernel, final_norm=final_norm),
        grid=(n_tiles,),
        in_specs=[tok(d), tok(LANES),
                  pl.BlockSpec((tm, LANES), lambda i: (jnp.minimum(i + 1, n_tiles - 1), 0)),
                  pl.BlockSpec(rb.shape, lambda i: (0, 0)), tok(d),
                  resident(w13), resident(w2),
                  pl.BlockSpec((1, 1, d), mod_map),
                  pl.BlockSpec((1, d), lambda i: (0, 0))],
        out_specs=tok(d),
        out_shape=jax.ShapeDtypeStruct((t, d), F32),
        scratch_shapes=[pltpu.VMEM((tm, LANES), F32)],
        compiler_params=_cparams(("arbitrary",)),
        name="mixture_of_experts",
    )(u, logits, logits, rb, h, w13, w2, g2, fg)


def _rope_tables(seq, positional):
    if not positional:
        ones = jnp.ones((seq, LANES), F32)
        zeros = jnp.zeros((seq, LANES), F32)
        return ones, zeros, ones, zeros
    t = jnp.arange(seq)
    pos_r = (t // GRID_W).astype(F32)[:, None]
    pos_c = (t % GRID_W).astype(F32)[:, None]

    def axial(width):
        h = width // 2
        half = h // 2
        inv = ROPE_BASE ** (-jnp.arange(half, dtype=F32) / half)
        inv2 = jnp.concatenate([inv, inv])[None, :]
        ang = jnp.concatenate([pos_r * inv2, pos_c * inv2], axis=1)
        sign = jnp.tile(jnp.concatenate([-jnp.ones(half), jnp.ones(half)]), 2)[None, :]
        return jnp.cos(ang), jnp.sin(ang) * sign

    rc, rs = axial(RET_DK)
    ra = jnp.concatenate([rc, rc], axis=1)
    rb = jnp.concatenate([rs, rs], axis=1)
    mc, ms = axial(MLA_ROPE)
    ma = jnp.concatenate([jnp.ones((seq, MLA_NOPE), F32), mc, jnp.zeros((seq, 32), F32)], axis=1)
    mb = jnp.concatenate([jnp.zeros((seq, MLA_NOPE), F32), ms, jnp.zeros((seq, 32), F32)], axis=1)
    return ra, rb, ma, mb


def _pad_heads(w, heads, width):
    r = w.shape[0]
    w = w.reshape(r, heads, width)
    return jnp.pad(w, ((0, 0), (0, 0), (0, LANES - width))).reshape(r, heads * LANES)


def _layer_weights(w_in, w_uq, w_ukv):
    d = w_in.shape[0]
    kpe = w_in[:, _C_KPE:_C_KPE + MLA_ROPE]
    blk = jnp.concatenate([jnp.zeros((d, MLA_NOPE), w_in.dtype), kpe,
                           jnp.zeros((d, LANES - MLA_NOPE - MLA_ROPE), w_in.dtype)], axis=1)
    w_ext = jnp.concatenate([w_in[:, :_C_KPE], blk], axis=1).astype(BF16)
    wuq = _pad_heads(w_uq, MLA_HEADS, MLA_NOPE + MLA_ROPE).astype(BF16)
    ukv = w_ukv.reshape(w_ukv.shape[0], MLA_HEADS, MLA_NOPE + MLA_VD)
    wuk = _pad_heads(ukv[:, :, :MLA_NOPE].reshape(w_ukv.shape[0], -1), MLA_HEADS, MLA_NOPE).astype(BF16)
    wuv = _pad_heads(ukv[:, :, MLA_NOPE:].reshape(w_ukv.shape[0], -1), MLA_HEADS, MLA_VD).astype(BF16)
    return w_ext, wuq, wuk, wuv


def _latent_key_tile(n_keys, n_ctx):
    for tk in range(MLA_MAX_KEY_TILE, 0, -LANES):
        if tk >= n_ctx and n_keys % tk == 0 and (n_keys // tk) % 2 == 0:
            return tk
    raise ValueError(f"no even tiling of {n_keys} keys")


def kernel(x, c, ctx, c_ctx, w_mod, b_mod, norm1_g, norm2_g, w_in, ret_decay_f, ret_decay_b, na_rpb,
           mla_q_norm, mla_kv_norm, w_uq, w_ukv, w_out, router_w, router_b, w1, w3, w2, final_norm_g):
    b, s, d = x.shape
    lc = ctx.shape[1]
    depth = w_mod.shape[0]
    rows = s // GRID_W
    tm_lat = TOKEN_TILE
    tm_ctx = min(CTX_TILE, lc)
    q_tiles = MLA_Q_TILES if s % (MLA_Q_TILE * MLA_Q_TILES) == 0 else 1

    n_mod = -(-(b + 1) // 8) * 8
    cv = jnp.concatenate([c, c_ctx[None, :], jnp.zeros((n_mod - b - 1, d), F32)], axis=0)
    mods = _modulation(cv, w_mod, b_mod).reshape(depth, n_mod, 6, d)

    tabs_lat = _rope_tables(s, True)
    tabs_ctx = _rope_tables(lc, False)
    na_plan = _na_bias_plan(rows)
    rw_f32 = jnp.pad(router_w.astype(F32), ((0, 0), (0, LANES - N_EXPERTS)))
    rw_hi = rw_f32.astype(BF16)
    rw = jnp.concatenate([rw_hi, (rw_f32 - rw_hi.astype(F32)).astype(BF16)], axis=1)
    rb = jnp.pad(router_b.astype(F32), (0, LANES - N_EXPERTS)).reshape(1, LANES)
    zero_state = (jnp.zeros((b, 2 * LANES, LANES), F32),) * 2

    h, hc = x, ctx
    for l in range(depth):
        last = l == depth - 1
        m_lat = [mods[l, :b, i][:, None, :] for i in range(6)]
        m_ctx = [mods[l, b:b + 1, i][:, None, :] for i in range(6)]
        w_ext, wuq, wuk, wuv = _layer_weights(w_in[l], w_uq[l], w_ukv[l])
        n1 = norm1_g[l].reshape(1, d)
        n2 = norm2_g[l].reshape(1, d)
        qn = mla_q_norm[l].reshape(1, -1)
        kvn = mla_kv_norm[l].reshape(1, -1)
        wo = w_out[l].astype(BF16)
        w13 = jnp.concatenate([w1[l], w3[l]], axis=-1).astype(BF16)
        w2b = w2[l].astype(BF16)
        fg = final_norm_g.reshape(1, d)

        ret_c, na_c, mq_c, mk_c, mv_c = _in_projection(
            hc, m_ctx[0], m_ctx[1], n1, w_ext, qn, kvn, wuq, wuk, wuv, tabs_ctx, tm_ctx)
        ret_l, na_l, mq_l, mk_l, mv_l = _in_projection(
            h, m_lat[0], m_lat[1], n1, w_ext, qn, kvn, wuq, wuk, wuv, tabs_lat, tm_lat)

        rtabs = _retention_tables(ret_decay_f[l], ret_decay_b[l])
        ro_c, st = _retention(ret_c, rtabs, zero_state)
        ro_l, _ = _retention(ret_l, rtabs, st)

        na_o = _na_attention(na_l, na_c, _na_bias(na_rpb[l], na_plan))
        mla_o = _latent_attention(mq_l, mk_l, mv_l, (mk_c, mv_c), tq=MLA_Q_TILE, q_tiles=q_tiles,
                                  tk=_latent_key_tile(s, lc), name="latent_attention")

        hn, u2, logits = _out_projection(ro_l, na_o, mla_o, h, wo, m_lat[2], m_lat[3], m_lat[4],
                                         n2, rw, tm_lat)
        h = _moe(u2.reshape(b * s, d), logits.reshape(b * s, LANES), rb, hn.reshape(b * s, d),
                 w13, w2b, m_lat[5], fg, s, TOKEN_TILE, last).reshape(b, s, d)

        if not last:
            nac_o = _pair_attention(na_c[:, :, 0:512], na_c[:, :, 512:1024], na_c[:, :, 1024:1536],
                                    tq=tm_ctx, name="context_na_attention")
            mlac_o = _latent_attention(mq_c, mk_c, mv_c, tq=tm_ctx, q_tiles=1, tk=lc,
                                       name="context_latent_attention")
            hcn, uc2, logits_c = _out_projection(ro_c, nac_o, mlac_o, hc, wo, m_ctx[2], m_ctx[3],
                                                 m_ctx[4], n2, rw, tm_ctx)
            hc = _moe(uc2.reshape(b * lc, d), logits_c.reshape(b * lc, LANES), rb,
                      hcn.reshape(b * lc, d), w13, w2b, m_ctx[5], fg, lc, min(TOKEN_TILE, lc),
                      False).reshape(b, lc, d)
    return h
```

```python
import functools

import numpy as np
import jax
import jax.numpy as jnp
from jax import lax
from jax.experimental import pallas as pl
from jax.experimental.pallas import tpu as pltpu

F32 = jnp.float32
BF16 = jnp.bfloat16

GRID_W = 64
EPS = 1e-6
ROPE_BASE = 10000.0
RET_HEADS, RET_DK, RET_DV = 4, 64, 64
RET_CHUNK = 256
NA_HEADS, NA_HD, NA_KH, NA_KW, NA_QBLOCK = 8, 64, 8, 16, 128
MLA_HEADS, MLA_NOPE, MLA_ROPE, MLA_VD = 4, 64, 32, 64
MLA_Q_RANK, MLA_KV_RANK = 384, 256
N_EXPERTS, N_GROUPS, TOP_K, EXPERT_FF = 16, 4, 2, 256
EXPERTS_PER_GROUP = N_EXPERTS // N_GROUPS

LANES = 128
HEAD = 64
TOKEN_TILE = 512
CTX_TILE = 256
NA_WIN_BLOCKS = 5
MLA_Q_TILE = 256
MLA_Q_TILES = 4
MLA_MAX_KEY_TILE = 1024
NA_SUB = 2
NA_STEP_BLOCKS = NA_WIN_BLOCKS + NA_SUB - 1
RET_GROUP = 8
NEG_BIG = -1e30
VMEM_LIMIT = 48 * 1024 * 1024

_C_RET, _C_NA, _C_CQ, _C_CKV, _C_KPE = 0, 1024, 2560, 2944, 3200
_W_EXT = _C_KPE + LANES
MLA_Q_SCALE = (MLA_NOPE + MLA_ROPE) ** -0.5 * float(np.log2(np.e))


def _cparams(sem):
    return pltpu.CompilerParams(dimension_semantics=sem, vmem_limit_bytes=VMEM_LIMIT)


def _dot(a, b):
    return jnp.dot(a, b, preferred_element_type=F32)


def _dot_nt(a, b):
    return lax.dot_general(a, b, (((1,), (1,)), ((), ())), preferred_element_type=F32)


def _sigmoid(x):
    return 1.0 / (1.0 + jnp.exp(-x))


def _win_roll(x, s, w):
    n = x.shape[-1]
    ax = x.ndim - 1
    lane = lax.broadcasted_iota(jnp.int32, x.shape, ax)
    fwd = pltpu.roll(x, n - s, ax)
    bwd = pltpu.roll(x, w - s, ax)
    return jnp.where((lane & (w - 1)) + s < w, fwd, bwd)


def _low_half(shape):
    lane = lax.broadcasted_iota(jnp.int32, shape, len(shape) - 1)
    return (lane & (LANES - 1)) < HEAD


def _mod_kernel(cv_ref, w_ref, b_ref, o_ref):
    cv = cv_ref[...]
    a = cv * _sigmoid(cv)
    o_ref[0] = jnp.dot(a, w_ref[0], preferred_element_type=F32,
                       precision=lax.Precision.HIGHEST) + b_ref[0]


def _modulation(cv, w_mod, b_mod):
    depth, d, n6 = w_mod.shape
    rows = cv.shape[0]
    tn = 1536
    return pl.pallas_call(
        _mod_kernel,
        grid=(depth, n6 // tn),
        in_specs=[pl.BlockSpec((rows, d), lambda l, j: (0, 0)),
                  pl.BlockSpec((1, d, tn), lambda l, j: (l, 0, j)),
                  pl.BlockSpec((1, 1, tn), lambda l, j: (l, 0, j))],
        out_specs=pl.BlockSpec((1, rows, tn), lambda l, j: (l, 0, j)),
        out_shape=jax.ShapeDtypeStruct((depth, rows, n6), F32),
        compiler_params=_cparams(("arbitrary", "arbitrary")),
        name="modulation",
    )(cv, w_mod, b_mod.reshape(depth, 1, n6))


def _rms(x, g):
    return x * lax.rsqrt(jnp.mean(x * x, axis=-1, keepdims=True) + EPS) * g


def _inproj_kernel(h_ref, sh_ref, sc_ref, g_ref, w_ref, qn_ref, kvn_ref, wuq_ref, wuk_ref, wuv_ref,
                   ra_ref, rb_ref, ma_ref, mb_ref,
                   ret_ref, na_ref, q_ref, k_ref, v_ref):
    x = h_ref[0]
    u = _rms(x, g_ref[...]) * (1.0 + sc_ref[0]) + sh_ref[0]
    ub = u.astype(BF16)

    r = _dot(ub, w_ref[:, _C_RET:_C_NA])
    qk = r[:, :512]
    ra = jnp.concatenate([ra_ref[...]] * 4, axis=1)
    rb = jnp.concatenate([rb_ref[...]] * 4, axis=1)
    qk = qk * ra + _win_roll(qk, 16, 32) * rb
    ret_ref[0, :, 0:256] = qk[:, :256] * (RET_DK ** -0.5)
    ret_ref[0, :, 256:512] = qk[:, 256:]
    ret_ref[0, :, 512:1024] = r[:, 512:]

    n = _dot(ub, w_ref[:, _C_NA:_C_CQ])
    na_ref[0, :, 0:512] = (n[:, :512] * (NA_HD ** -0.5)).astype(BF16)
    na_ref[0, :, 512:1536] = n[:, 512:].astype(BF16)

    ma = jnp.concatenate([ma_ref[...]] * 4, axis=1)
    mb = jnp.concatenate([mb_ref[...]] * 4, axis=1)
    cq = _rms(_dot(ub, w_ref[:, _C_CQ:_C_CKV]), qn_ref[...]).astype(BF16)
    q = _dot(cq, wuq_ref[...])
    q = q * ma + _win_roll(q, 8, 16) * mb
    q_ref[0] = (q * MLA_Q_SCALE).astype(BF16)
    ckv = _rms(_dot(ub, w_ref[:, _C_CKV:_C_KPE]), kvn_ref[...]).astype(BF16)
    kp = _dot(ub, w_ref[:, _C_KPE:_W_EXT])
    kp = kp * ma_ref[...] + _win_roll(kp, 8, 16) * mb_ref[...]
    kp = jnp.concatenate([kp] * MLA_HEADS, axis=1)
    k_ref[0] = (_dot(ckv, wuk_ref[...]) + kp).astype(BF16)
    v = _dot(ckv, wuv_ref[...])
    lane = lax.broadcasted_iota(jnp.int32, v.shape, 1)
    v_ref[0] = jnp.where((lane & (LANES - 1)) == HEAD, 1.0, v).astype(BF16)


def _in_projection(h, sh, sc, g, w_ext, qn, kvn, wuq, wuk, wuv, tabs, tm):
    b, l, d = h.shape
    per_batch = sh.shape[0] > 1
    mod_map = (lambda i, j: (i, 0, 0)) if per_batch else (lambda i, j: (0, 0, 0))
    const2 = lambda i, j: (0, 0)
    tab_spec = pl.BlockSpec((tm, LANES), lambda i, j: (j, 0))
    tok = lambda w: pl.BlockSpec((1, tm, w), lambda i, j: (i, j, 0))
    return pl.pallas_call(
        _inproj_kernel,
        grid=(b, l // tm),
        in_specs=[tok(d),
                  pl.BlockSpec((1, 1, d), mod_map), pl.BlockSpec((1, 1, d), mod_map),
                  pl.BlockSpec((1, d), const2),
                  pl.BlockSpec(w_ext.shape, const2),
                  pl.BlockSpec(qn.shape, const2), pl.BlockSpec(kvn.shape, const2),
                  pl.BlockSpec(wuq.shape, const2), pl.BlockSpec(wuk.shape, const2),
                  pl.BlockSpec(wuv.shape, const2),
                  tab_spec, tab_spec, tab_spec, tab_spec],
        out_specs=[tok(1024), tok(1536), tok(512), tok(512), tok(512)],
        out_shape=[jax.ShapeDtypeStruct((b, l, 1024), F32),
                   jax.ShapeDtypeStruct((b, l, 1536), BF16),
                   jax.ShapeDtypeStruct((b, l, 512), BF16),
                   jax.ShapeDtypeStruct((b, l, 512), BF16),
                   jax.ShapeDtypeStruct((b, l, 512), BF16)],
        compiler_params=_cparams(("arbitrary", "arbitrary")),
        name="in_projection",
    )(h, sh, sc, g, w_ext, qn, kvn, wuq, wuk, wuv, *tabs)


def _retention_kernel(*refs, backward, group):
    if backward:
        (q_ref, k_ref, v_ref, g_ref, of_ref, dm_ref, qw_ref, kw_ref, gc_ref, s0_ref,
         o_ref, sfin_ref, state) = refs
    else:
        q_ref, k_ref, v_ref, dm_ref, qw_ref, kw_ref, gc_ref, s0_ref, o_ref, sfin_ref, state = refs
    step = pl.program_id(1)
    c = RET_CHUNK
    n_pairs = RET_HEADS // 2

    @pl.when(step == 0)
    def _():
        state[...] = s0_ref[0]

    lo = _low_half((c, LANES))
    row_lo = lax.broadcasted_iota(jnp.int32, (LANES, LANES), 0) < HEAD
    same_head = row_lo == _low_half((LANES, LANES))
    qw, kw = qw_ref[...], kw_ref[...]

    intra, kv, qd = {}, {}, {}
    for ci in range(group):
        rows = slice(c * ci, c * (ci + 1))
        for p in range(n_pairs):
            sl = slice(LANES * p, LANES * (p + 1))
            qp, kp = q_ref[0, rows, sl], k_ref[0, rows, sl]
            kb, vb = kp.astype(BF16), v_ref[0, rows, sl].astype(BF16)
            qab = jnp.concatenate([jnp.where(lo, qp, 0.0), jnp.where(lo, 0.0, qp)], axis=0).astype(BF16)
            sab = (_dot_nt(qab, kb) * dm_ref[p]).astype(BF16)
            oab = _dot(sab, vb)
            intra[ci, p] = jnp.where(lo, oab[:c], oab[c:])
            kv[ci, p] = jnp.where(same_head, _dot((kp * kw[:, sl]).T.astype(BF16), vb), 0.0)
            qd[ci, p] = (qp * qw[:, sl]).astype(BF16)

    order = range(group - 1, -1, -1) if backward else range(group)
    out = {}
    for p in range(n_pairs):
        sl = slice(LANES * p, LANES * (p + 1))
        sp = state[sl, :]
        gc = gc_ref[sl, :]
        for ci in order:
            out[ci, p] = intra[ci, p] + _dot(qd[ci, p], sp.astype(BF16))
            sp = gc * sp + kv[ci, p]
        state[sl, :] = sp

    for ci in range(group):
        rows = slice(c * ci, c * (ci + 1))
        o = jnp.concatenate([out[ci, p] for p in range(n_pairs)], axis=1)
        if not backward:
            o_ref[0, rows, :] = o
            continue
        tot = of_ref[0, rows, :] + o
        g = g_ref[0, rows, :]
        normed = []
        for p in range(n_pairs):
            t = tot[:, LANES * p:LANES * (p + 1)]

            def head_mean(a):
                s_lo = jnp.sum(jnp.where(lo, a, 0.0), axis=-1, keepdims=True)
                s_hi = jnp.sum(jnp.where(lo, 0.0, a), axis=-1, keepdims=True)
                return jnp.where(lo, s_lo, s_hi) * (1.0 / HEAD)

            dlt = t - head_mean(t)
            normed.append(dlt * lax.rsqrt(head_mean(dlt * dlt) + EPS))
        on = jnp.concatenate(normed, axis=1)
        o_ref[0, rows, :] = (on * (g * _sigmoid(g))).astype(BF16)

    @pl.when(step == pl.num_programs(1) - 1)
    def _():
        sfin_ref[0] = state[...]


def _retention_sweep(ret, o_fwd, tabs, s0, *, backward):
    b, l, _ = ret.shape
    c = RET_CHUNK
    group = min(RET_GROUP, l // c)
    n = l // (c * group)
    gidx = (lambda s: n - 1 - s) if backward else (lambda s: s)
    tok = lambda col: pl.BlockSpec((1, c * group, 256), lambda i, s: (i, gidx(s), col))
    const = lambda a: pl.BlockSpec(a.shape, lambda i, s: (0,) * a.ndim)
    state_spec = pl.BlockSpec((1, 2 * LANES, LANES), lambda i, s: (i, 0, 0))
    dm, qw, kw, gc = tabs
    if backward:
        in_specs = [tok(0), tok(1), tok(2), tok(3), tok(0)]
        args = [ret, ret, ret, ret, o_fwd]
    else:
        in_specs = [tok(0), tok(1), tok(2)]
        args = [ret, ret, ret]
    return pl.pallas_call(
        functools.partial(_retention_kernel, backward=backward, group=group),
        grid=(b, n),
        in_specs=in_specs + [const(dm), const(qw), const(kw), const(gc), state_spec],
        out_specs=[tok(0), state_spec],
        out_shape=[jax.ShapeDtypeStruct((b, l, 256), BF16 if backward else F32),
                   jax.ShapeDtypeStruct((b, 2 * LANES, LANES), F32)],
        scratch_shapes=[pltpu.VMEM((2 * LANES, LANES), F32)],
        compiler_params=_cparams(("arbitrary", "arbitrary")),
        name="retention_backward" if backward else "retention_forward",
    )(*args, dm, qw, kw, gc, s0)


def _retention(ret, tabs, s0):
    fwd_tabs = tuple(t[0] for t in tabs)
    bwd_tabs = tuple(t[1] for t in tabs)
    o_f, s_f = _retention_sweep(ret, None, fwd_tabs, s0[0], backward=False)
    o, s_b = _retention_sweep(ret, o_f, bwd_tabs, s0[1], backward=True)
    return o, (s_f, s_b)


def _retention_tables(decay_f, decay_b):
    c = RET_CHUNK
    pos = jnp.arange(c, dtype=F32)
    diff = pos[:, None] - pos[None, :]
    lg = jnp.stack([jax.nn.log_sigmoid(decay_f.astype(F32)), jax.nn.log_sigmoid(decay_b.astype(F32))])
    lgh = lg[:, :, None, None]
    dm_f = jnp.where(diff >= 0, jnp.exp(lgh[0] * jnp.maximum(diff, 0.0)), 0.0)
    dm_b = jnp.where(diff < 0, jnp.exp(lgh[1] * jnp.maximum(-diff, 0.0)), 0.0)
    dm = jnp.stack([dm_f, dm_b]).reshape(2, RET_HEADS // 2, 2 * c, c)
    lanes = lambda t: jnp.repeat(t, RET_DK, axis=-1)
    lgc = lg[:, None, :]
    qw = lanes(jnp.exp(lgc * jnp.stack([pos + 1.0, c - pos])[:, :, None]))
    kw = lanes(jnp.exp(lgc * jnp.stack([c - 1.0 - pos, pos])[:, :, None]))
    gch = jnp.exp(lg * c)
    gc = jnp.broadcast_to(jnp.repeat(gch, RET_DK, axis=-1)[:, :, None], (2, RET_HEADS * RET_DK, LANES))
    return dm, qw, kw, gc


def _na_kernel(*refs, n_ctx):
    w = NA_STEP_BLOCKS
    q_ref = refs[0]
    k_refs, v_refs = refs[1:1 + w], refs[1 + w:1 + 2 * w]
    kc_ref, vc_ref, bias_ref, o_ref, s_scr, p_scr, den_scr = refs[1 + 2 * w:]
    tq = q_ref.shape[1]
    n_loc = w * NA_QBLOCK
    n_lane_blocks = (n_loc + n_ctx) // LANES
    lo = _low_half((tq, LANES))
    for h in range(NA_HEADS):
        sl = slice(LANES * (h // 2), LANES * (h // 2 + 1))
        qp = q_ref[0, :, sl]
        qm = jnp.where(lo, qp, 0) if h % 2 == 0 else jnp.where(lo, 0, qp)
        kwin = jnp.concatenate([r[0, :, sl] for r in k_refs], axis=0)
        s_scr[h, :, 0:n_loc] = _dot_nt(qm, kwin) + bias_ref[0, h]
        s_scr[h, :, n_loc:] = _dot_nt(qm, kc_ref[0, :, sl])
    for h in range(NA_HEADS):
        part = s_scr[h, :, 0:LANES]
        for c in range(1, n_lane_blocks):
            part = jnp.maximum(part, s_scr[h, :, LANES * c:LANES * (c + 1)])
        m = jnp.max(part, axis=-1, keepdims=True)
        tot = None
        for c in range(n_lane_blocks):
            cs = slice(LANES * c, LANES * (c + 1))
            e = jnp.exp(s_scr[h, :, cs] - m)
            tot = e if tot is None else tot + e
            p_scr[h, :, cs] = e.astype(BF16)
        den_scr[h] = jnp.broadcast_to(jnp.sum(tot, axis=-1, keepdims=True), (tq, LANES))
    for p in range(NA_HEADS // 2):
        sl = slice(LANES * p, LANES * (p + 1))
        vwin = jnp.concatenate([r[0, :, sl] for r in v_refs], axis=0)
        halves = []
        for h in (2 * p, 2 * p + 1):
            o = _dot(p_scr[h, :, 0:n_loc], vwin) + _dot(p_scr[h, :, n_loc:], vc_ref[0, :, sl])
            halves.append(o / den_scr[h])
        o_ref[0, :, sl] = jnp.where(lo, halves[0], halves[1]).astype(BF16)


def _na_step_variant(i, n_steps):
    return jnp.where(i == 0, 0, jnp.where(i == n_steps - 1, 2, 1))


def _na_attention(na, na_ctx, bias):
    b, l, _ = na.shape
    n_ctx = na_ctx.shape[1]
    nblk = l // NA_QBLOCK
    n_steps = nblk // NA_SUB
    tq = NA_SUB * NA_QBLOCK
    n_loc = NA_STEP_BLOCKS * NA_QBLOCK
    n_keys = n_loc + n_ctx

    def win(col, j):
        return pl.BlockSpec((1, NA_QBLOCK, 512), lambda bi, i: (
            bi, jnp.clip(NA_SUB * i - 2, 0, nblk - NA_STEP_BLOCKS) + j, col))

    in_specs = ([pl.BlockSpec((1, tq, 512), lambda bi, i: (bi, i, 0))]
                + [win(1, j) for j in range(NA_STEP_BLOCKS)]
                + [win(2, j) for j in range(NA_STEP_BLOCKS)]
                + [pl.BlockSpec((1, n_ctx, 512), lambda bi, i: (bi, 0, 1)),
                   pl.BlockSpec((1, n_ctx, 512), lambda bi, i: (bi, 0, 2)),
                   pl.BlockSpec((1, NA_HEADS, tq, n_loc),
                                lambda bi, i: (_na_step_variant(i, n_steps), 0, 0, 0))])
    args = [na] * (1 + 2 * NA_STEP_BLOCKS) + [na_ctx, na_ctx, bias]
    return pl.pallas_call(
        functools.partial(_na_kernel, n_ctx=n_ctx),
        grid=(b, n_steps),
        in_specs=in_specs,
        out_specs=pl.BlockSpec((1, tq, 512), lambda bi, i: (bi, i, 0)),
        out_shape=jax.ShapeDtypeStruct((b, l, 512), BF16),
        scratch_shapes=[pltpu.VMEM((NA_HEADS, tq, n_keys), F32),
                        pltpu.VMEM((NA_HEADS, tq, n_keys), BF16),
                        pltpu.VMEM((NA_HEADS, tq, LANES), F32)],
        compiler_params=_cparams(("arbitrary", "arbitrary")),
        name="neighbourhood_attention",
    )(*args)


def _na_bias_plan(rows):
    nblk = rows * GRID_W // NA_QBLOCK
    kh = min(NA_KH, rows)
    q_rows = NA_QBLOCK // GRID_W
    k_rows = NA_WIN_BLOCKS * q_rows
    plan = []
    for i in (0, 1, 2, nblk - 2, nblk - 1):
        kb = min(max(i - 2, 0), nblk - NA_WIN_BLOCKS)
        per_q = []
        for rr in range(q_rows):
            r = q_rows * i + rr
            r0 = min(max(r - kh // 2, 0), rows - kh)
            per_q.append([(q_rows * kb + kj - r + NA_KH - 1) if r0 <= q_rows * kb + kj < r0 + kh else None
                          for kj in range(k_rows)])
        plan.append(per_q)
    return plan


def _na_bias(rpb, plan):
    c = np.arange(GRID_W)
    c0 = np.clip(c - NA_KW // 2, 0, GRID_W - NA_KW)
    kc = np.arange(GRID_W)
    ok = (kc[None, :] >= c0[:, None]) & (kc[None, :] < c0[:, None] + NA_KW)
    reach = GRID_W - NA_KW
    padded = jnp.pad(rpb.astype(F32), ((0, 0), (0, 0), (reach, reach)))
    cols = jnp.stack([padded[:, :, GRID_W - 1 - ci:2 * GRID_W - 1 - ci] for ci in range(GRID_W)], axis=2)
    cols = jnp.where(jnp.asarray(ok)[None, None], cols, NEG_BIG)
    outside = jnp.full((NA_HEADS, GRID_W, GRID_W), NEG_BIG, F32)
    variants = []
    for per_q in plan:
        q_parts = [jnp.concatenate([outside if dr is None else cols[:, dr] for dr in per_k], axis=-1)
                   for per_k in per_q]
        variants.append(jnp.concatenate(q_parts, axis=1))

    hidden = jnp.full((NA_HEADS, NA_QBLOCK, NA_QBLOCK), NEG_BIG, F32)

    def place(blk, off):
        return jnp.concatenate([hidden] * off + [blk] + [hidden] * (1 - off), axis=-1)

    steps = [(0, 0, 1, 0), (2, 0, 2, 1), (3, 1, 4, 1)]
    return jnp.stack([jnp.concatenate([place(variants[va], oa), place(variants[vb], ob)], axis=1)
                      for va, oa, vb, ob in steps])


def _latent_attn_kernel(*refs, tk, q_tiles, n_ctx):
    if n_ctx:
        q_ref, k_ref, v_ref, kc_ref, vc_ref, o_ref, s_scr, p_scr, al_scr, m_scr, acc_scr = refs
    else:
        q_ref, k_ref, v_ref, o_ref, s_scr, p_scr, al_scr, m_scr, acc_scr = refs
    n_k = k_ref.shape[1] // tk + (1 if n_ctx else 0)
    n_heads = v_ref.shape[2] // LANES
    tq = q_ref.shape[1] // q_tiles
    assert n_k % 2 == 1, "static slot parities below assume an odd number of key tiles"

    def is_ctx(j):
        return bool(n_ctx) and isinstance(j, int) and j == n_k - 1

    def key_rows(j):
        return pl.ds(pl.multiple_of(j * tk, tk), tk)

    def slot_of(a, j_parity):
        return (a * n_k + j_parity) % 2

    def scores(a, j, slot):
        for h in range(n_heads):
            sl = slice(LANES * h, LANES * (h + 1))
            qh = q_ref[0, tq * a:tq * (a + 1), sl]
            if is_ctx(j):
                s_scr[slot, h, :, 0:n_ctx] = _dot_nt(qh, kc_ref[0, :, sl])
            else:
                s_scr[slot, h] = _dot_nt(qh, k_ref[0, key_rows(j), sl])

    def softmax(a, slot, ctx_tile=False):
        n_lane_blocks = (n_ctx if ctx_tile else tk) // LANES
        for h in range(n_heads):
            part = s_scr[slot, h, :, 0:LANES]
            for c in range(1, n_lane_blocks):
                part = jnp.maximum(part, s_scr[slot, h, :, LANES * c:LANES * (c + 1)])
            m_old = m_scr[a, h]
            m_new = jnp.maximum(m_old, jnp.max(part, axis=-1, keepdims=True))
            al_scr[slot, h] = jnp.exp2(m_old - m_new)
            m_scr[a, h] = m_new
            for c in range(n_lane_blocks):
                cs = slice(LANES * c, LANES * (c + 1))
                p_scr[slot, h, :, cs] = jnp.exp2(s_scr[slot, h, :, cs] - m_new).astype(BF16)

    def values(a, j, slot):
        for h in range(n_heads):
            sl = slice(LANES * h, LANES * (h + 1))
            if is_ctx(j):
                pv = _dot(p_scr[slot, h, :, 0:n_ctx], vc_ref[0, :, sl])
            else:
                pv = _dot(p_scr[slot, h], v_ref[0, key_rows(j), sl])
            acc_scr[a, h] = al_scr[slot, h] * acc_scr[a, h] + pv

    def time_step(nxt, cur, prv):
        if nxt is not None:
            scores(*nxt)
        softmax(*cur)
        if prv is not None:
            values(*prv)

    m_scr[...] = jnp.full(m_scr.shape, -jnp.inf, F32)
    acc_scr[...] = jnp.zeros(acc_scr.shape, F32)
    scores(0, 0, 0)
    for a in range(q_tiles):
        s_even, s_odd = slot_of(a, 0), slot_of(a, 1)
        first_of_next = (a + 1, 0, slot_of(a + 1, 0)) if a + 1 < q_tiles else None
        last_of_prev = (a - 1, n_k - 1, slot_of(a - 1, 0)) if a > 0 else None
        if n_k == 1:
            time_step(first_of_next, (a, s_even), last_of_prev)
            continue
        time_step((a, 1, s_odd), (a, s_even), last_of_prev)

        def body(i, carry, a=a, s_even=s_even, s_odd=s_odd):
            j = 2 * i + 1
            time_step((a, j + 1, s_even), (a, s_odd), (a, j - 1, s_even))
            time_step((a, j + 2, s_odd), (a, s_even), (a, j, s_odd))
            return carry

        lax.fori_loop(0, (n_k - 3) // 2, body, 0)
        time_step((a, n_k - 1, s_even), (a, s_odd), (a, n_k - 3, s_even))
        time_step(first_of_next, (a, s_even, bool(n_ctx)), (a, n_k - 2, s_odd))
    values(q_tiles - 1, n_k - 1, slot_of(q_tiles - 1, 0))
    for a in range(q_tiles):
        outs = []
        for h in range(n_heads):
            acc = acc_scr[a, h]
            outs.append(acc[:, 0:HEAD] / acc[:, HEAD:HEAD + 1])
        o_ref[0, tq * a:tq * (a + 1), :] = jnp.concatenate(outs, axis=1).astype(BF16)


def _latent_attention(q, k, v, ctx_kv=None, *, tq, q_tiles, tk, name):
    b, lq, w = q.shape
    lk = k.shape[1]
    n_heads = w // LANES
    rows = tq * q_tiles
    n_ctx = 0 if ctx_kv is None else ctx_kv[0].shape[1]
    resident = lambda n: pl.BlockSpec((1, n, w), lambda bi, i: (bi, 0, 0), pipeline_mode=pl.Buffered(1))
    stat = pltpu.VMEM((q_tiles, n_heads, tq, LANES), F32)
    in_specs = [pl.BlockSpec((1, rows, w), lambda bi, i: (bi, i, 0)), resident(lk), resident(lk)]
    args = [q, k, v]
    if n_ctx:
        assert n_ctx <= tk and n_ctx % LANES == 0
        in_specs += [resident(n_ctx), resident(n_ctx)]
        args += list(ctx_kv)
    return pl.pallas_call(
        functools.partial(_latent_attn_kernel, tk=tk, q_tiles=q_tiles, n_ctx=n_ctx),
        grid=(b, lq // rows),
        in_specs=in_specs,
        out_specs=pl.BlockSpec((1, rows, n_heads * HEAD), lambda bi, i: (bi, i, 0)),
        out_shape=jax.ShapeDtypeStruct((b, lq, n_heads * HEAD), BF16),
        scratch_shapes=[pltpu.VMEM((2, n_heads, tq, tk), F32), pltpu.VMEM((2, n_heads, tq, tk), BF16),
                        pltpu.VMEM((2, n_heads, tq, LANES), F32), stat, stat],
        compiler_params=_cparams(("arbitrary", "arbitrary")),
        name=name,
    )(*args)


def _pair_attn_kernel(q_ref, k_ref, v_ref, o_ref):
    tq = q_ref.shape[1]
    lo = _low_half((tq, LANES))
    for p in range(v_ref.shape[2] // LANES):
        sl = slice(LANES * p, LANES * (p + 1))
        qp, kp, vp = q_ref[0, :, sl], k_ref[0, :, sl], v_ref[0, :, sl]
        halves = []
        for half in range(2):
            qm = jnp.where(lo, qp, 0) if half == 0 else jnp.where(lo, 0, qp)
            s = _dot_nt(qm, kp)
            e = jnp.exp(s - jnp.max(s, axis=-1, keepdims=True))
            halves.append(_dot(e.astype(BF16), vp) / jnp.sum(e, axis=-1, keepdims=True))
        o_ref[0, :, sl] = jnp.where(lo, halves[0], halves[1]).astype(BF16)


def _pair_attention(q, k, v, *, tq, name):
    b, lq, w = q.shape
    lk = k.shape[1]
    return pl.pallas_call(
        _pair_attn_kernel,
        grid=(b, lq // tq),
        in_specs=[pl.BlockSpec((1, tq, w), lambda bi, i: (bi, i, 0)),
                  pl.BlockSpec((1, lk, w), lambda bi, i: (bi, 0, 0)),
                  pl.BlockSpec((1, lk, w), lambda bi, i: (bi, 0, 0))],
        out_specs=pl.BlockSpec((1, tq, w), lambda bi, i: (bi, i, 0)),
        out_shape=jax.ShapeDtypeStruct((b, lq, w), BF16),
        compiler_params=_cparams(("arbitrary", "arbitrary")),
        name=name,
    )(q, k, v)


def _gates(logits, bias):
    scores = _sigmoid(logits)
    sel = scores + bias
    lane = lax.broadcasted_iota(jnp.int32, logits.shape, 1)
    e4 = lane & (EXPERTS_PER_GROUP - 1)
    grp = (lane >> 2) & (N_GROUPS - 1)
    one, zero = jnp.float32(1.0), jnp.float32(0.0)

    rank = jnp.zeros(logits.shape, F32)
    for d in range(1, EXPERTS_PER_GROUP):
        other = _win_roll(sel, d, EXPERTS_PER_GROUP)
        other_first = ((e4 + d) & (EXPERTS_PER_GROUP - 1)) < e4
        ahead = jnp.logical_or(other > sel, jnp.logical_and(other == sel, other_first))
        rank = rank + jnp.where(ahead, one, zero)
    top2 = rank < float(TOP_K)

    def group_sum(a):
        tot = a
        for d in range(1, EXPERTS_PER_GROUP):
            tot = tot + _win_roll(a, d, EXPERTS_PER_GROUP)
        return tot

    gscore = group_sum(jnp.where(top2, sel, zero))
    beaten = jnp.zeros(logits.shape, F32)
    for d in range(1, N_GROUPS):
        other = _win_roll(gscore, d * EXPERTS_PER_GROUP, N_EXPERTS)
        other_first = ((grp + d) & (N_GROUPS - 1)) < grp
        ahead = jnp.logical_or(other > gscore, jnp.logical_and(other == gscore, other_first))
        beaten = beaten + jnp.where(ahead, one, zero)
    chosen = jnp.logical_and(jnp.logical_and(beaten < 0.5, top2), lane < N_EXPERTS)
    w = jnp.where(chosen, scores, zero)
    den = group_sum(w)
    return jnp.where(chosen, w / den, zero)


def _outproj_kernel(ret_ref, na_ref, mla_ref, h_ref, w_ref, g1_ref, sh_ref, sc_ref, n2_ref,
                    rw_ref, hn_ref, u_ref, logit_ref):
    mix = (_dot(ret_ref[0], w_ref[0:256, :]) + _dot(na_ref[0], w_ref[256:768, :])
           + _dot(mla_ref[0], w_ref[768:1024, :]))
    hn = h_ref[0] + g1_ref[0] * mix
    hn_ref[0] = hn
    u = _rms(hn, n2_ref[...]) * (1.0 + sc_ref[0]) + sh_ref[0]
    u_hi = u.astype(BF16)
    u_ref[0] = u_hi
    tm = u.shape[0]
    u_lo = (u - u_hi.astype(F32)).astype(BF16)
    both = _dot(jnp.concatenate([u_hi, u_lo], axis=0), rw_ref[...])
    logits = both[:tm, :LANES] + both[:tm, LANES:] + both[tm:, :LANES]
    logit_ref[0] = logits


def _out_projection(ret_o, na_o, mla_o, h, w_out, g1, sh2, sc2, n2, rw, tm):
    b, l, d = h.shape
    per_batch = g1.shape[0] > 1
    mod_map = (lambda i, j: (i, 0, 0)) if per_batch else (lambda i, j: (0, 0, 0))
    const2 = lambda i, j: (0, 0)
    tok = lambda w: pl.BlockSpec((1, tm, w), lambda i, j: (i, j, 0))
    stream = lambda w: pl.BlockSpec((1, tm, w), lambda i, j: (i, j, 0), pipeline_mode=pl.Buffered(3))
    mod = pl.BlockSpec((1, 1, d), mod_map)
    in_specs = [stream(256), stream(512), stream(256), stream(d),
                pl.BlockSpec(w_out.shape, const2), mod, mod, mod,
                pl.BlockSpec((1, d), const2),
                pl.BlockSpec(rw.shape, const2)]
    out_specs = [tok(d), tok(d), tok(LANES)]

    def outer(*refs):
        pltpu.emit_pipeline(_outproj_kernel, grid=(b, l // tm), in_specs=in_specs,
                            out_specs=out_specs)(*refs)

    n_in = len(in_specs)
    return pl.pallas_call(
        outer,
        in_specs=[pl.BlockSpec(memory_space=pl.ANY)] * n_in,
        out_specs=[pl.BlockSpec(memory_space=pl.ANY)] * len(out_specs),
        out_shape=[jax.ShapeDtypeStruct((b, l, d), F32),
                   jax.ShapeDtypeStruct((b, l, d), BF16),
                   jax.ShapeDtypeStruct((b, l, LANES), F32)],
        compiler_params=pltpu.CompilerParams(vmem_limit_bytes=VMEM_LIMIT),
        name="out_projection",
    )(ret_o, na_o, mla_o, h, w_out, g1, sh2, sc2, n2, rw)


def _moe_kernel(u_ref, logit_ref, logit_next_ref, rb_ref, h_ref, w13_ref, w2_ref, g2_ref, fg_ref,
                o_ref, gate_scr, *, final_norm):
    u = u_ref[...]

    @pl.when(pl.program_id(0) == 0)
    def _():
        gate_scr[...] = _gates(logit_ref[...], rb_ref[...])

    gates = gate_scr[...]
    gates_next = _gates(logit_next_ref[...], rb_ref[...])
    lane = lax.broadcasted_iota(jnp.int32, gates.shape, 1)
    y = None
    for e in range(N_EXPERTS):
        a = _dot(u, w13_ref[e])
        a1, a3 = a[:, :EXPERT_FF], a[:, EXPERT_FF:]
        gcol = jnp.sum(jnp.where(lane == e, gates, 0.0), axis=-1, keepdims=True)
        hdn = ((a1 * _sigmoid(a1)) * a3 * gcol).astype(BF16)
        ye = _dot(hdn, w2_ref[e])
        y = ye if y is None else y + ye
    hn = h_ref[...] + g2_ref[0] * y
    if final_norm:
        hn = _rms(hn, fg_ref[...])
    o_ref[...] = hn
    gate_scr[...] = gates_next


def _moe(u, logits, rb, h, w13, w2, g2, fg, tokens_per_mod, tm, final_norm):
    t, d = h.shape
    per_batch = g2.shape[0] > 1
    tiles_per_mod = tokens_per_mod // tm
    n_tiles = t // tm
    mod_map = (lambda i: (i // tiles_per_mod, 0, 0)) if per_batch else (lambda i: (0, 0, 0))
    tok = lambda w: pl.BlockSpec((tm, w), lambda i: (i, 0))
    resident = lambda a: pl.BlockSpec(a.shape, lambda i: (0, 0, 0), pipeline_mode=pl.Buffered(1))
    return pl.pallas_call(
        functools.partial(_moe_kernel, final_norm=final_norm),
        grid=(n_tiles,),
        in_specs=[tok(d), tok(LANES),
                  pl.BlockSpec((tm, LANES), lambda i: (jnp.minimum(i + 1, n_tiles - 1), 0)),
                  pl.BlockSpec(rb.shape, lambda i: (0, 0)), tok(d),
                  resident(w13), resident(w2),
                  pl.BlockSpec((1, 1, d), mod_map),
                  pl.BlockSpec((1, d), lambda i: (0, 0))],
        out_specs=tok(d),
        out_shape=jax.ShapeDtypeStruct((t, d), F32),
        scratch_shapes=[pltpu.VMEM((tm, LANES), F32)],
        compiler_params=_cparams(("arbitrary",)),
        name="mixture_of_experts",
    )(u, logits, logits, rb, h, w13, w2, g2, fg)


def _rope_tables(seq, positional):
    if not positional:
        ones = jnp.ones((seq, LANES), F32)
        zeros = jnp.zeros((seq, LANES), F32)
        return ones, zeros, ones, zeros
    t = jnp.arange(seq)
    pos_r = (t // GRID_W).astype(F32)[:, None]
    pos_c = (t % GRID_W).astype(F32)[:, None]

    def axial(width):
        h = width // 2
        half = h // 2
        inv = ROPE_BASE ** (-jnp.arange(half, dtype=F32) / half)
        inv2 = jnp.concatenate([inv, inv])[None, :]
        ang = jnp.concatenate([pos_r * inv2, pos_c * inv2], axis=1)
        sign = jnp.tile(jnp.concatenate([-jnp.ones(half), jnp.ones(half)]), 2)[None, :]
        return jnp.cos(ang), jnp.sin(ang) * sign

    rc, rs = axial(RET_DK)
    ra = jnp.concatenate([rc, rc], axis=1)
    rb = jnp.concatenate([rs, rs], axis=1)
    mc, ms = axial(MLA_ROPE)
    ma = jnp.concatenate([jnp.ones((seq, MLA_NOPE), F32), mc, jnp.zeros((seq, 32), F32)], axis=1)
    mb = jnp.concatenate([jnp.zeros((seq, MLA_NOPE), F32), ms, jnp.zeros((seq, 32), F32)], axis=1)
    return ra, rb, ma, mb


def _pad_heads(w, heads, width):
    r = w.shape[0]
    w = w.reshape(r, heads, width)
    return jnp.pad(w, ((0, 0), (0, 0), (0, LANES - width))).reshape(r, heads * LANES)


def _layer_weights(w_in, w_uq, w_ukv):
    d = w_in.shape[0]
    kpe = w_in[:, _C_KPE:_C_KPE + MLA_ROPE]
    blk = jnp.concatenate([jnp.zeros((d, MLA_NOPE), w_in.dtype), kpe,
                           jnp.zeros((d, LANES - MLA_NOPE - MLA_ROPE), w_in.dtype)], axis=1)
    w_ext = jnp.concatenate([w_in[:, :_C_KPE], blk], axis=1).astype(BF16)
    wuq = _pad_heads(w_uq, MLA_HEADS, MLA_NOPE + MLA_ROPE).astype(BF16)
    ukv = w_ukv.reshape(w_ukv.shape[0], MLA_HEADS, MLA_NOPE + MLA_VD)
    wuk = _pad_heads(ukv[:, :, :MLA_NOPE].reshape(w_ukv.shape[0], -1), MLA_HEADS, MLA_NOPE).astype(BF16)
    wuv = _pad_heads(ukv[:, :, MLA_NOPE:].reshape(w_ukv.shape[0], -1), MLA_HEADS, MLA_VD).astype(BF16)
    return w_ext, wuq, wuk, wuv


def _latent_key_tile(n_keys, n_ctx):
    for tk in range(MLA_MAX_KEY_TILE, 0, -LANES):
        if tk >= n_ctx and n_keys % tk == 0 and (n_keys // tk) % 2 == 0:
            return tk
    raise ValueError(f"no even tiling of {n_keys} keys")


def kernel(x, c, ctx, c_ctx, w_mod, b_mod, norm1_g, norm2_g, w_in, ret_decay_f, ret_decay_b, na_rpb,
           mla_q_norm, mla_kv_norm, w_uq, w_ukv, w_out, router_w, router_b, w1, w3, w2, final_norm_g):
    b, s, d = x.shape
    lc = ctx.shape[1]
    depth = w_mod.shape[0]
    rows = s // GRID_W
    tm_lat = TOKEN_TILE
    tm_ctx = min(CTX_TILE, lc)
    q_tiles = MLA_Q_TILES if s % (MLA_Q_TILE * MLA_Q_TILES) == 0 else 1

    n_mod = -(-(b + 1) // 8) * 8
    cv = jnp.concatenate([c, c_ctx[None, :], jnp.zeros((n_mod - b - 1, d), F32)], axis=0)
    mods = _modulation(cv, w_mod, b_mod).reshape(depth, n_mod, 6, d)

    tabs_lat = _rope_tables(s, True)
    tabs_ctx = _rope_tables(lc, False)
    na_plan = _na_bias_plan(rows)
    rw_f32 = jnp.pad(router_w.astype(F32), ((0, 0), (0, LANES - N_EXPERTS)))
    rw_hi = rw_f32.astype(BF16)
    rw = jnp.concatenate([rw_hi, (rw_f32 - rw_hi.astype(F32)).astype(BF16)], axis=1)
    rb = jnp.pad(router_b.astype(F32), (0, LANES - N_EXPERTS)).reshape(1, LANES)
    zero_state = (jnp.zeros((b, 2 * LANES, LANES), F32),) * 2

    h, hc = x, ctx
    for l in range(depth):
        last = l == depth - 1
        m_lat = [mods[l, :b, i][:, None, :] for i in range(6)]
        m_ctx = [mods[l, b:b + 1, i][:, None, :] for i in range(6)]
        w_ext, wuq, wuk, wuv = _layer_weights(w_in[l], w_uq[l], w_ukv[l])
        n1 = norm1_g[l].reshape(1, d)
        n2 = norm2_g[l].reshape(1, d)
        qn = mla_q_norm[l].reshape(1, -1)
        kvn = mla_kv_norm[l].reshape(1, -1)
        wo = w_out[l].astype(BF16)
        w13 = jnp.concatenate([w1[l], w3[l]], axis=-1).astype(BF16)
        w2b = w2[l].astype(BF16)
        fg = final_norm_g.reshape(1, d)

        ret_c, na_c, mq_c, mk_c, mv_c = _in_projection(
            hc, m_ctx[0], m_ctx[1], n1, w_ext, qn, kvn, wuq, wuk, wuv, tabs_ctx, tm_ctx)
        ret_l, na_l, mq_l, mk_l, mv_l = _in_projection(
            h, m_lat[0], m_lat[1], n1, w_ext, qn, kvn, wuq, wuk, wuv, tabs_lat, tm_lat)

        rtabs = _retention_tables(ret_decay_f[l], ret_decay_b[l])
        ro_c, st = _retention(ret_c, rtabs, zero_state)
        ro_l, _ = _retention(ret_l, rtabs, st)

        na_o = _na_attention(na_l, na_c, _na_bias(na_rpb[l], na_plan))
        mla_o = _latent_attention(mq_l, mk_l, mv_l, (mk_c, mv_c), tq=MLA_Q_TILE, q_tiles=q_tiles,
                                  tk=_latent_key_tile(s, lc), name="latent_attention")

        hn, u2, logits = _out_projection(ro_l, na_o, mla_o, h, wo, m_lat[2], m_lat[3], m_lat[4],
                                         n2, rw, tm_lat)
        h = _moe(u2.reshape(b * s, d), logits.reshape(b * s, LANES), rb, hn.reshape(b * s, d),
                 w13, w2b, m_lat[5], fg, s, TOKEN_TILE, last).reshape(b, s, d)

        if not last:
            nac_o = _pair_attention(na_c[:, :, 0:512], na_c[:, :, 512:1024], na_c[:, :, 1024:1536],
                                    tq=tm_ctx, name="context_na_attention")
            mlac_o = _latent_attention(mq_c, mk_c, mv_c, tq=tm_ctx, q_tiles=1, tk=lc,
                                       name="context_latent_attention")
            hcn, uc2, logits_c = _out_projection(ro_c, nac_o, mlac_o, hc, wo, m_ctx[2], m_ctx[3],
                                                 m_ctx[4], n2, rw, tm_ctx)
            hc = _moe(uc2.reshape(b * lc, d), logits_c.reshape(b * lc, LANES), rb,
                      hcn.reshape(b * lc, d), w13, w2b, m_ctx[5], fg, lc, min(TOKEN_TILE, lc),
                      False).reshape(b, lc, d)
    return h
```
